```python
import jax, jax.numpy as jnp
from jax import lax
import numpy as np

D_MODEL = 1024
BATCH = 2
SEQ = 8192
DEPTH = 2
DEC_BATCH = 8
DEC_SEQ = 32
PAST_LEN = 1024

CHUNK = 64
N_META = 16
MIX_W = D_MODEL
ATTN_W = MIX_W // 2
LRU_W = MIX_W - ATTN_W
V_DIM = 64
N_HEADS = ATTN_W // V_DIM
QK_NOPE = 64
QK_ROPE = 32
Q_LORA = 768
KV_LORA = 256
LRU_BLOCKS = 8
LRU_BW = LRU_W // LRU_BLOCKS
LRU_C = 8.0
CONV_W = 4
D_FF = 4 * D_MODEL
ROPE_THETA = 10000.0
Q_BLOCK = 128
EPS = 1e-6
SM_SCALE = (QK_NOPE + QK_ROPE) ** -0.5
IN_SPLITS = (Q_LORA, Q_LORA + KV_LORA, Q_LORA + KV_LORA + QK_ROPE, Q_LORA + KV_LORA + QK_ROPE + LRU_W)
IN_W = Q_LORA + KV_LORA + QK_ROPE + 2 * LRU_W

kernel_name = "hybrid_streaming_mla_rglru_step"


def rmsnorm(x, g):
    xf = x.astype(jnp.float32)
    y = xf * lax.rsqrt(jnp.mean(xf * xf, axis=-1, keepdims=True) + EPS)
    return (y * g.astype(jnp.float32)).astype(x.dtype)


def rope(x, pos):
    inv = ROPE_THETA ** (-jnp.arange(0, QK_ROPE, 2, dtype=jnp.float32) / QK_ROPE)
    ang = pos.astype(jnp.float32)[:, None] * inv[None, :]
    cos, sin = jnp.cos(ang), jnp.sin(ang)
    if x.ndim == 4:
        cos, sin = cos[:, None, :], sin[:, None, :]
    x1, x2 = jnp.split(x.astype(jnp.float32), 2, axis=-1)
    return jnp.concatenate([x1 * cos - x2 * sin, x1 * sin + x2 * cos], axis=-1).astype(x.dtype)


def expand_rows(rows, b):
    return jnp.broadcast_to(rows, (b,) + rows.shape[1:])


def lru_combine(left, right):
    a1, b1 = left
    a2, b2 = right
    return a1 * a2, a2 * b1 + b2


def mixer_project(x, pos, h0, conv_buf, p):
    B, T, _ = x.shape
    z = rmsnorm(x, p["norm_mix_g"]) @ p["w_in"]
    c_q, c_kv, k_r, x_r, g_r = jnp.split(z, IN_SPLITS, axis=-1)
    q = (rmsnorm(c_q, p["q_norm_g"]) @ p["w_uq"]).reshape(B, T, N_HEADS, QK_NOPE + QK_ROPE)
    q_nope = q[..., :QK_NOPE]
    q_pe = rope(q[..., QK_NOPE:], pos)
    ckv = rmsnorm(c_kv, p["kv_norm_g"])
    kpe = rope(k_r, pos)
    xpad = jnp.concatenate([expand_rows(conv_buf, B).astype(x.dtype), x_r], axis=1)
    xc = p["conv_b"] + sum(xpad[:, k:k + T] * p["conv_w"][k] for k in range(CONV_W))
    new_buf = xpad[:, T:]
    xb = xc.reshape(B, T, LRU_BLOCKS, LRU_BW)
    r = jax.nn.sigmoid(jnp.einsum("btnc,ncd->btnd", xb, p["w_gate_a"]).reshape(B, T, LRU_W) + p["b_gate_a"])
    i = jax.nn.sigmoid(jnp.einsum("btnc,ncd->btnd", xb, p["w_gate_x"]).reshape(B, T, LRU_W) + p["b_gate_x"])
    log_a = -LRU_C * r.astype(jnp.float32) * jax.nn.softplus(-p["lru_lambda"].astype(jnp.float32))
    a = jnp.exp(log_a)
    b = jnp.sqrt(-jnp.expm1(2.0 * log_a)) * (i * xc).astype(jnp.float32)
    b = b.at[:, 0].add(a[:, 0] * expand_rows(h0, B).astype(jnp.float32))
    _, h = lax.associative_scan(lru_combine, (a, b), axis=1)
    y = h.astype(x.dtype) * jax.nn.gelu(g_r)
    return q_nope, q_pe, ckv, kpe, y, h[:, -1].astype(x.dtype), new_buf


def attend_block(q_nope, q_pe, q_chunk, k_nope, k_pe, v, k_chunk):
    s = jnp.einsum("bqhd,bkhd->bhqk", q_nope, k_nope) + jnp.einsum("bqhr,bkr->bhqk", q_pe, k_pe)
    s = s.astype(jnp.float32) * SM_SCALE
    mask = k_chunk[None, :] <= q_chunk[:, None]
    s = jnp.where(mask[None, None], s, -1e30)
    w = jax.nn.softmax(s, axis=-1).astype(v.dtype)
    return jnp.einsum("bhqk,bkhd->bqhd", w, v)


def chunk_causal_attention(q_nope, q_pe, q_chunk, k_nope, k_pe, v, k_chunk):
    B, T = q_nope.shape[0], q_nope.shape[1]
    if T <= Q_BLOCK or T % Q_BLOCK != 0:
        return attend_block(q_nope, q_pe, q_chunk, k_nope, k_pe, v, k_chunk)
    nb = T // Q_BLOCK
    qn = q_nope.reshape(B, nb, Q_BLOCK, N_HEADS, QK_NOPE).swapaxes(0, 1)
    qp = q_pe.reshape(B, nb, Q_BLOCK, N_HEADS, QK_ROPE).swapaxes(0, 1)
    qc = q_chunk.reshape(nb, Q_BLOCK)

    def one_block(args):
        bn, bp, bc = args
        return attend_block(bn, bp, bc, k_nope, k_pe, v, k_chunk)

    o = lax.map(one_block, (qn, qp, qc))
    return o.swapaxes(0, 1).reshape(B, T, N_HEADS, V_DIM)


def layer_output(x, q_nope, q_pe, pos, ckv_keys, kpe_keys, key_pos, y_lru, p):
    B, T, _ = x.shape
    kv = jnp.einsum("bkc,chd->bkhd", ckv_keys, p["w_ukv"].reshape(KV_LORA, N_HEADS, QK_NOPE + V_DIM))
    k_nope, v = kv[..., :QK_NOPE], kv[..., QK_NOPE:]
    o = chunk_causal_attention(q_nope, q_pe, jnp.floor_divide(pos, CHUNK), k_nope, kpe_keys, v,
                               jnp.floor_divide(key_pos, CHUNK))
    merged = jnp.concatenate([rmsnorm(o.reshape(B, T, ATTN_W), p["attn_out_g"]),
                              rmsnorm(y_lru, p["lru_out_g"])], axis=-1)
    h = x + merged @ p["w_out"]
    f = jnp.square(jax.nn.relu(rmsnorm(h, p["norm_mlp_g"]) @ p["w_up"])) @ p["w_down"]
    return h + f


def setup_inputs(seed: int = 0) -> dict:
    key = jax.random.key(seed)
    ks = jax.random.split(key, 27)
    f32 = jnp.float32

    def nrm(k, shape, scale):
        return jax.random.normal(k, shape, f32) * scale

    def gain(k, shape):
        return 1.0 + 0.02 * jax.random.normal(k, shape, f32)

    u = jax.random.uniform(ks[19], (DEPTH, LRU_W), f32, 0.9, 0.999)
    return {
        "x_prompt": nrm(ks[0], (BATCH, SEQ, D_MODEL), 1.0),
        "x_sample": nrm(ks[1], (DEC_BATCH, DEC_SEQ, D_MODEL), 1.0),
        "cache_ckv": nrm(ks[2], (DEPTH, DEC_BATCH, PAST_LEN, KV_LORA), 1.0),
        "cache_kpe": nrm(ks[3], (DEPTH, DEC_BATCH, PAST_LEN, QK_ROPE), 1.0),
        "state_lru_h": nrm(ks[4], (DEPTH, DEC_BATCH, LRU_W), 0.5),
        "state_conv": nrm(ks[5], (DEPTH, DEC_BATCH, CONV_W - 1, LRU_W), 1.0),
        "meta_tokens": nrm(ks[6], (N_META, D_MODEL), 1.0),
        "norm_mix_g": gain(ks[7], (DEPTH, D_MODEL)),
        "w_in": nrm(ks[8], (DEPTH, D_MODEL, IN_W), D_MODEL ** -0.5),
        "q_norm_g": gain(ks[9], (DEPTH, Q_LORA)),
        "w_uq": nrm(ks[10], (DEPTH, Q_LORA, N_HEADS * (QK_NOPE + QK_ROPE)), Q_LORA ** -0.5),
        "kv_norm_g": gain(ks[11], (DEPTH, KV_LORA)),
        "w_ukv": nrm(ks[12], (DEPTH, KV_LORA, N_HEADS * (QK_NOPE + V_DIM)), KV_LORA ** -0.5),
        "conv_w": nrm(ks[13], (DEPTH, CONV_W, LRU_W), CONV_W ** -0.5),
        "conv_b": nrm(ks[14], (DEPTH, LRU_W), 0.02),
        "w_gate_a": nrm(ks[15], (DEPTH, LRU_BLOCKS, LRU_BW, LRU_BW), LRU_BW ** -0.5),
        "b_gate_a": nrm(ks[16], (DEPTH, LRU_W), 0.02),
        "w_gate_x": nrm(ks[17], (DEPTH, LRU_BLOCKS, LRU_BW, LRU_BW), LRU_BW ** -0.5),
        "b_gate_x": nrm(ks[18], (DEPTH, LRU_W), 0.02),
        "lru_lambda": jnp.log(u) - jnp.log1p(-u),
        "attn_out_g": gain(ks[20], (DEPTH, ATTN_W)),
        "lru_out_g": gain(ks[21], (DEPTH, LRU_W)),
        "w_out": nrm(ks[22], (DEPTH, MIX_W, D_MODEL), MIX_W ** -0.5),
        "norm_mlp_g": gain(ks[23], (DEPTH, D_MODEL)),
        "w_up": nrm(ks[24], (DEPTH, D_MODEL, D_FF), D_MODEL ** -0.5),
        "w_down": nrm(ks[25], (DEPTH, D_FF, D_MODEL), D_FF ** -0.5),
        "final_norm_g": gain(ks[26], (D_MODEL,)),
    }


def reference(x_prompt, x_sample, cache_ckv, cache_kpe, state_lru_h, state_conv, meta_tokens,
              norm_mix_g, w_in, q_norm_g, w_uq, kv_norm_g, w_ukv, conv_w, conv_b,
              w_gate_a, b_gate_a, w_gate_x, b_gate_x, lru_lambda, attn_out_g, lru_out_g,
              w_out, norm_mlp_g, w_up, w_down, final_norm_g):
    dt = x_prompt.dtype
    b_p, seq = x_prompt.shape[0], x_prompt.shape[1]
    b_s, dec_seq = x_sample.shape[0], x_sample.shape[1]
    past_len = cache_ckv.shape[2]
    meta_pos = jnp.arange(-N_META, 0, dtype=jnp.int32)
    prompt_pos = jnp.arange(seq, dtype=jnp.int32)
    sample_pos = past_len + jnp.arange(dec_seq, dtype=jnp.int32)
    prompt_key_pos = jnp.concatenate([meta_pos, prompt_pos])
    sample_key_pos = jnp.concatenate([meta_pos, jnp.arange(past_len, dtype=jnp.int32), sample_pos])

    h_meta = meta_tokens[None].astype(dt)
    h_p, h_s = x_prompt, x_sample
    ckv_p_l, kpe_p_l, lru_p_l, conv_p_l = [], [], [], []
    ckv_s_l, kpe_s_l, lru_s_l, conv_s_l = [], [], [], []
    for l in range(DEPTH):
        p = {"norm_mix_g": norm_mix_g[l], "w_in": w_in[l], "q_norm_g": q_norm_g[l], "w_uq": w_uq[l],
             "kv_norm_g": kv_norm_g[l], "w_ukv": w_ukv[l], "conv_w": conv_w[l], "conv_b": conv_b[l],
             "w_gate_a": w_gate_a[l], "b_gate_a": b_gate_a[l], "w_gate_x": w_gate_x[l],
             "b_gate_x": b_gate_x[l], "lru_lambda": lru_lambda[l], "attn_out_g": attn_out_g[l],
             "lru_out_g": lru_out_g[l], "w_out": w_out[l], "norm_mlp_g": norm_mlp_g[l],
             "w_up": w_up[l], "w_down": w_down[l]}
        mq_n, mq_p, m_ckv, m_kpe, m_y, m_h, m_buf = mixer_project(
            h_meta, meta_pos, jnp.zeros((1, LRU_W), dt), jnp.zeros((1, CONV_W - 1, LRU_W), dt), p)
        pq_n, pq_p, p_ckv, p_kpe, p_y, p_h, p_buf = mixer_project(h_p, prompt_pos, m_h, m_buf, p)
        h_p = layer_output(h_p, pq_n, pq_p, prompt_pos,
                           jnp.concatenate([expand_rows(m_ckv, b_p), p_ckv], axis=1),
                           jnp.concatenate([expand_rows(m_kpe, b_p), p_kpe], axis=1),
                           prompt_key_pos, p_y, p)
        sq_n, sq_p, s_ckv, s_kpe, s_y, s_h, s_buf = mixer_project(h_s, sample_pos, state_lru_h[l], state_conv[l], p)
        h_s = layer_output(h_s, sq_n, sq_p, sample_pos,
                           jnp.concatenate([expand_rows(m_ckv, b_s), cache_ckv[l].astype(dt), s_ckv], axis=1),
                           jnp.concatenate([expand_rows(m_kpe, b_s), cache_kpe[l].astype(dt), s_kpe], axis=1),
                           sample_key_pos, s_y, p)
        if l + 1 < DEPTH:
            h_meta = layer_output(h_meta, mq_n, mq_p, meta_pos, m_ckv, m_kpe, meta_pos, m_y, p)
        ckv_p_l.append(p_ckv); kpe_p_l.append(p_kpe); lru_p_l.append(p_h); conv_p_l.append(p_buf)
        ckv_s_l.append(s_ckv); kpe_s_l.append(s_kpe); lru_s_l.append(s_h); conv_s_l.append(s_buf)

    y_prompt = rmsnorm(h_p, final_norm_g)
    y_sample = rmsnorm(h_s, final_norm_g)
    new_ckv_prompt = jnp.stack(ckv_p_l)
    new_kpe_prompt = jnp.stack(kpe_p_l)
    new_lru_h_prompt = jnp.stack(lru_p_l)
    new_conv_prompt = jnp.stack(conv_p_l)
    new_ckv_sample = jnp.stack(ckv_s_l)
    new_kpe_sample = jnp.stack(kpe_s_l)
    new_lru_h_sample = jnp.stack(lru_s_l)
    new_conv_sample = jnp.stack(conv_s_l)
    return (y_prompt, y_sample, new_ckv_prompt, new_kpe_prompt, new_lru_h_prompt, new_conv_prompt,
            new_ckv_sample, new_kpe_sample, new_lru_h_sample, new_conv_sample)
```

```python
import functools

import jax
import jax.numpy as jnp
from jax import lax
from jax.experimental import pallas as pl
from jax.experimental.pallas import tpu as pltpu

D_MODEL = 1024
DEPTH = 2
CHUNK = 64
N_META = 16
ATTN_W = 512
LRU_W = 512
V_DIM = 64
N_HEADS = 8
QK_NOPE = 64
QK_ROPE = 32
Q_LORA = 768
KV_LORA = 256
LRU_BLOCKS = 8
LRU_BW = LRU_W // LRU_BLOCKS
LRU_C = 8.0
CONV_W = 4
D_FF = 4 * D_MODEL
ROPE_THETA = 10000.0
EPS = 1e-6
SM_SCALE = (QK_NOPE + QK_ROPE) ** -0.5
NEG_INF = -1e30

LANES = 128
SUBLANES = 8
HEAD_PAD = LANES
VMEM_LIMIT_BYTES = 56 * 1024 * 1024

ROPE_PAD = LANES
COL_CQ = 0
COL_CKV = Q_LORA
COL_KR = Q_LORA + KV_LORA
COL_XR = COL_KR + ROPE_PAD
COL_GR = COL_XR + LRU_W
IN_W_PAD = COL_GR + LRU_W

HEADS_PER_STEP = 2
ONES_ROWS = 16
V_AUG = V_DIM + ONES_ROWS
F_CHUNK = 1024


def _rms(x, g):
    return x * lax.rsqrt(jnp.mean(x * x, axis=-1, keepdims=True) + EPS) * g


def _rope_lanes(x, tab_ref):
    half = QK_ROPE // 2
    return (x * tab_ref[0] + pltpu.roll(x, LANES - half, 1) * tab_ref[1]
            + pltpu.roll(x, half, 1) * tab_ref[2])


def _dot(a, b):
    return jnp.dot(a, b, preferred_element_type=jnp.float32)


def _dot_nt(a, b):
    return lax.dot_general(a, b, (((1,), (1,)), ((), ())), preferred_element_type=jnp.float32)


def _mixer_kernel(x_ref, h0_ref, buf0_ref, tabq_ref, tabk_ref, g_mix_ref, w_in_ref, g_q_ref,
                  w_uq_ref, g_kv_ref, w_uk_ref, w_uvt_ref, conv_w_ref, conv_b_ref, wg_ref, bg_ref,
                  lam_ref, g_lru_ref,
                  q_ref, k_ref, vt_ref, ckv_ref, kpe_ref, y_ref, hlast_ref, buf_ref,
                  xpad_scr, hcar_scr, *, tm):
    t = pl.program_id(1)

    @pl.when(t == 0)
    def _():
        xpad_scr[0:SUBLANES, :] = buf0_ref[0]
        hcar_scr[...] = h0_ref[0]

    x = x_ref[0]
    xn = _rms(x, g_mix_ref[...]).astype(jnp.bfloat16)
    z = _dot(xn, w_in_ref[...])

    cqn = _rms(z[:, COL_CQ:COL_CQ + Q_LORA], g_q_ref[...]).astype(jnp.bfloat16)
    q = _dot(cqn, w_uq_ref[...])
    for h in range(N_HEADS):
        qh = q[:, h * HEAD_PAD:(h + 1) * HEAD_PAD]
        q_ref[0, h] = _rope_lanes(qh, tabq_ref).astype(jnp.bfloat16)

    ckv = _rms(z[:, COL_CKV:COL_CKV + KV_LORA], g_kv_ref[...])
    ckv_ref[0] = ckv
    ckv_b = ckv.astype(jnp.bfloat16)
    kpe = _rope_lanes(z[:, COL_KR:COL_KR + ROPE_PAD], tabk_ref)
    kpe_ref[0] = kpe[:, :QK_ROPE]
    kpe_shift = pltpu.roll(kpe, QK_NOPE, 1)
    kk = _dot(ckv_b, w_uk_ref[...])
    for h in range(N_HEADS):
        k_ref[0, h] = (kk[:, h * HEAD_PAD:(h + 1) * HEAD_PAD] + kpe_shift).astype(jnp.bfloat16)
    vt_ref[0] = _dot_nt(w_uvt_ref[...], ckv_b).astype(jnp.bfloat16)

    x_r = z[:, COL_XR:COL_XR + LRU_W]
    g_r = z[:, COL_GR:COL_GR + LRU_W]
    xpad_scr[SUBLANES:SUBLANES + tm, :] = x_r
    xc = conv_b_ref[...] + x_r * conv_w_ref[CONV_W - 1:CONV_W, :]
    for kk_i in range(CONV_W - 1):
        back = CONV_W - 1 - kk_i
        xc = xc + xpad_scr[SUBLANES - back:SUBLANES - back + tm, :] * conv_w_ref[kk_i:kk_i + 1, :]
    buf_ref[0] = xpad_scr[tm + SUBLANES - (CONV_W - 1):tm + SUBLANES, :]
    xpad_scr[0:SUBLANES, :] = xpad_scr[tm:tm + SUBLANES, :]

    half_w = LRU_W // 2
    gates = [_dot(xc[:, s * half_w:(s + 1) * half_w].astype(jnp.bfloat16), wg_ref[s]) for s in range(2)]
    pre_a = jnp.concatenate([gates[0][:, :half_w], gates[1][:, :half_w]], axis=1)
    pre_x = jnp.concatenate([gates[0][:, half_w:], gates[1][:, half_w:]], axis=1)
    r = jax.nn.sigmoid(pre_a + bg_ref[0:1, :])
    gi = jax.nn.sigmoid(pre_x + bg_ref[1:2, :])
    neg_lam = -lam_ref[...]
    softplus = jnp.maximum(neg_lam, 0.0) + jnp.log1p(jnp.exp(-jnp.abs(neg_lam)))
    log_a = (-LRU_C) * r * softplus
    a = jnp.exp(log_a)
    b = jnp.sqrt(-jnp.tanh(log_a) * (a * a + 1.0)) * (gi * xc)

    rows = lax.broadcasted_iota(jnp.int32, (tm, LRU_W), 0)
    shift = 1
    while shift < tm:
        keep = rows >= shift
        a_prev = jnp.where(keep, pltpu.roll(a, shift, 0), 1.0)
        b_prev = jnp.where(keep, pltpu.roll(b, shift, 0), 0.0)
        b = a * b_prev + b
        a = a * a_prev
        shift *= 2
    hh = a * hcar_scr[...] + b
    h_last = hh[tm - 1:tm, :]
    hcar_scr[...] = h_last
    hlast_ref[0] = h_last
    y = hh * jax.nn.gelu(g_r)
    y_ref[0] = _rms(y, g_lru_ref[...]).astype(jnp.bfloat16)


def _mixer_call(x, h0, buf0p, tabq, tabk, lw, *, tm, shared_state):
    bsz, seq, _ = x.shape
    nt = seq // tm
    const2 = lambda b, t: (0, 0)
    const3 = lambda b, t: (0, 0, 0)
    state_idx = (lambda b, t: (0, 0, 0)) if shared_state else (lambda b, t: (b, 0, 0))
    in_specs = [
        pl.BlockSpec((1, tm, D_MODEL), lambda b, t: (b, t, 0)),
        pl.BlockSpec((1, 1, LRU_W), state_idx),
        pl.BlockSpec((1, SUBLANES, LRU_W), state_idx),
        pl.BlockSpec((3, tm, LANES), lambda b, t: (0, t, 0)),
        pl.BlockSpec((3, tm, LANES), lambda b, t: (0, t, 0)),
        pl.BlockSpec((1, D_MODEL), const2),
        pl.BlockSpec((D_MODEL, IN_W_PAD), const2),
        pl.BlockSpec((1, Q_LORA), const2),
        pl.BlockSpec((Q_LORA, N_HEADS * HEAD_PAD), const2),
        pl.BlockSpec((1, KV_LORA), const2),
        pl.BlockSpec((KV_LORA, N_HEADS * HEAD_PAD), const2),
        pl.BlockSpec((ATTN_W, KV_LORA), const2),
        pl.BlockSpec((CONV_W, LRU_W), const2),
        pl.BlockSpec((1, LRU_W), const2),
        pl.BlockSpec((2, LRU_W // 2, LRU_W), const3),
        pl.BlockSpec((2, LRU_W), const2),
        pl.BlockSpec((1, LRU_W), const2),
        pl.BlockSpec((1, LRU_W), const2),
    ]
    out_shape = [
        jax.ShapeDtypeStruct((bsz, N_HEADS, seq, HEAD_PAD), jnp.bfloat16),
        jax.ShapeDtypeStruct((bsz, N_HEADS, seq, HEAD_PAD), jnp.bfloat16),
        jax.ShapeDtypeStruct((bsz, ATTN_W, seq), jnp.bfloat16),
        jax.ShapeDtypeStruct((bsz, seq, KV_LORA), jnp.float32),
        jax.ShapeDtypeStruct((bsz, seq, QK_ROPE), jnp.float32),
        jax.ShapeDtypeStruct((bsz, seq, LRU_W), jnp.bfloat16),
        jax.ShapeDtypeStruct((bsz, 1, LRU_W), jnp.float32),
        jax.ShapeDtypeStruct((bsz, CONV_W - 1, LRU_W), jnp.float32),
    ]
    out_specs = [
        pl.BlockSpec((1, N_HEADS, tm, HEAD_PAD), lambda b, t: (b, 0, t, 0)),
        pl.BlockSpec((1, N_HEADS, tm, HEAD_PAD), lambda b, t: (b, 0, t, 0)),
        pl.BlockSpec((1, ATTN_W, tm), lambda b, t: (b, 0, t)),
        pl.BlockSpec((1, tm, KV_LORA), lambda b, t: (b, t, 0)),
        pl.BlockSpec((1, tm, QK_ROPE), lambda b, t: (b, t, 0)),
        pl.BlockSpec((1, tm, LRU_W), lambda b, t: (b, t, 0)),
        pl.BlockSpec((1, 1, LRU_W), lambda b, t: (b, 0, 0)),
        pl.BlockSpec((1, CONV_W - 1, LRU_W), lambda b, t: (b, 0, 0)),
    ]
    return pl.pallas_call(
        functools.partial(_mixer_kernel, tm=tm),
        grid=(bsz, nt),
        in_specs=in_specs,
        out_specs=out_specs,
        out_shape=out_shape,
        scratch_shapes=[pltpu.VMEM((tm + SUBLANES, LRU_W), jnp.float32),
                        pltpu.VMEM((1, LRU_W), jnp.float32)],
        compiler_params=pltpu.CompilerParams(
            dimension_semantics=("arbitrary", "arbitrary"), vmem_limit_bytes=VMEM_LIMIT_BYTES),
        name="mixer",
    )(x, h0, buf0p, tabq, tabk, lw["g_mix"], lw["w_in"], lw["g_q"], lw["w_uq"], lw["g_kv"],
      lw["w_uk"], lw["w_uvt"], lw["conv_w"], lw["conv_b"], lw["wg"], lw["bg"], lw["lam"], lw["g_lru"])


def _kvup_kernel(ckv_ref, kpe_ref, w_uk_ref, w_uvt_ref, k_ref, vt_ref):
    ckv_b = ckv_ref[0].astype(jnp.bfloat16)
    kpe_shift = pltpu.roll(kpe_ref[0], QK_NOPE, 1)
    kk = _dot(ckv_b, w_uk_ref[...])
    for h in range(N_HEADS):
        k_ref[0, h] = (kk[:, h * HEAD_PAD:(h + 1) * HEAD_PAD] + kpe_shift).astype(jnp.bfloat16)
    vt_ref[0] = _dot_nt(w_uvt_ref[...], ckv_b).astype(jnp.bfloat16)


def _kvup_call(ckv, kpe_pad, lw, *, tm):
    bsz, seq, _ = ckv.shape
    return pl.pallas_call(
        _kvup_kernel,
        grid=(bsz, seq // tm),
        in_specs=[
            pl.BlockSpec((1, tm, KV_LORA), lambda b, t: (b, t, 0)),
            pl.BlockSpec((1, tm, LANES), lambda b, t: (b, t, 0)),
            pl.BlockSpec((KV_LORA, N_HEADS * HEAD_PAD), lambda b, t: (0, 0)),
            pl.BlockSpec((ATTN_W, KV_LORA), lambda b, t: (0, 0)),
        ],
        out_specs=[
            pl.BlockSpec((1, N_HEADS, tm, HEAD_PAD), lambda b, t: (b, 0, t, 0)),
            pl.BlockSpec((1, ATTN_W, tm), lambda b, t: (b, 0, t)),
        ],
        out_shape=[
            jax.ShapeDtypeStruct((bsz, N_HEADS, seq, HEAD_PAD), jnp.bfloat16),
            jax.ShapeDtypeStruct((bsz, ATTN_W, seq), jnp.bfloat16),
        ],
        compiler_params=pltpu.CompilerParams(
            dimension_semantics=("arbitrary", "arbitrary"), vmem_limit_bytes=VMEM_LIMIT_BYTES),
        name="kv_up",
    )(ckv, kpe_pad, lw["w_uk"], lw["w_uvt"])


def _softmax_block(st, vt_aug, m_old, acc_old):
    m_blk = jnp.max(st, axis=0, keepdims=True)
    if m_old is None:
        m_new = m_blk
        p = jnp.exp(st - m_new).astype(jnp.bfloat16)
        return m_new, _dot(vt_aug, p)
    m_new = jnp.maximum(m_old, m_blk)
    p = jnp.exp(st - m_new).astype(jnp.bfloat16)
    return m_new, jnp.exp(m_old - m_new) * acc_old + _dot(vt_aug, p)


def _attn_kernel(*refs, tq, tk, n_prefix, n_prefix_valid, causal):
    if causal:
        q_ref, kp_ref, vtp_ref, k_ref, vt_ref, o_ref, m_scr, acc_scr = refs
    else:
        q_ref, kp_ref, vtp_ref, o_ref = refs
    i = pl.program_id(2)
    ones_p = jnp.ones((ONES_ROWS, n_prefix), jnp.bfloat16)

    ms, accs = [], []
    for hh in range(HEADS_PER_STEP):
        st = _dot_nt(kp_ref[0, hh], q_ref[0, hh])
        if n_prefix_valid < n_prefix:
            key_idx = lax.broadcasted_iota(jnp.int32, (n_prefix, tq), 0)
            st = jnp.where(key_idx < n_prefix_valid, st, NEG_INF)
        vt_aug = jnp.concatenate([vtp_ref[0, hh * V_DIM:(hh + 1) * V_DIM, :], ones_p], axis=0)
        m, acc = _softmax_block(st, vt_aug, None, None)
        ms.append(m)
        accs.append(acc)

    if causal:
        ones_k = jnp.ones((ONES_ROWS, tk), jnp.bfloat16)
        for hh in range(HEADS_PER_STEP):
            m_scr[hh] = ms[hh]
            acc_scr[hh] = accs[hh]

        def kv_step(start, mask):
            for hh in range(HEADS_PER_STEP):
                k_t = k_ref[0, hh, pl.ds(start, tk), :]
                vt_t = vt_ref[0, hh * V_DIM:(hh + 1) * V_DIM, pl.ds(start, tk)]
                st = _dot_nt(k_t, q_ref[0, hh])
                if mask is not None:
                    st = jnp.where(mask, st, NEG_INF)
                vt_aug = jnp.concatenate([vt_t, ones_k], axis=0)
                m, acc = _softmax_block(st, vt_aug, m_scr[hh], acc_scr[hh])
                m_scr[hh] = m
                acc_scr[hh] = acc

        def body(j, carry):
            kv_step(pl.multiple_of(j * tk, tk), None)
            return carry

        lax.fori_loop(0, i, body, 0)
        key_chunk = lax.broadcasted_iota(jnp.int32, (tk, tq), 0) // CHUNK
        qry_chunk = lax.broadcasted_iota(jnp.int32, (tk, tq), 1) // CHUNK
        kv_step(pl.multiple_of(i * tk, tk), key_chunk <= qry_chunk)
        accs = [acc_scr[hh] for hh in range(HEADS_PER_STEP)]

    outs = [acc[:V_DIM, :] / acc[V_DIM:V_DIM + 1, :] for acc in accs]
    o_t = jnp.concatenate(outs, axis=0)
    o_ref[0] = o_t.T.astype(jnp.bfloat16)


def _attn_call(q, kp, vtp, k, vt, *, tq, n_prefix_valid, prefix_shared):
    bsz, _, seq, _ = q.shape
    n_prefix = kp.shape[2]
    causal = k is not None
    hp_n = N_HEADS // HEADS_PER_STEP
    pidx4 = (lambda b, hp, i: (0, hp, 0, 0)) if prefix_shared else (lambda b, hp, i: (b, hp, 0, 0))
    pidx3 = (lambda b, hp, i: (0, hp, 0)) if prefix_shared else (lambda b, hp, i: (b, hp, 0))
    in_specs = [
        pl.BlockSpec((1, HEADS_PER_STEP, tq, HEAD_PAD), lambda b, hp, i: (b, hp, i, 0)),
        pl.BlockSpec((1, HEADS_PER_STEP, n_prefix, HEAD_PAD), pidx4),
        pl.BlockSpec((1, HEADS_PER_STEP * V_DIM, n_prefix), pidx3),
    ]
    args = [q, kp, vtp]
    scratch = []
    if causal:
        in_specs += [
            pl.BlockSpec((1, HEADS_PER_STEP, seq, HEAD_PAD), lambda b, hp, i: (b, hp, 0, 0)),
            pl.BlockSpec((1, HEADS_PER_STEP * V_DIM, seq), lambda b, hp, i: (b, hp, 0)),
        ]
        args += [k, vt]
        scratch = [pltpu.VMEM((HEADS_PER_STEP, 1, tq), jnp.float32),
                   pltpu.VMEM((HEADS_PER_STEP, V_AUG, tq), jnp.float32)]
    return pl.pallas_call(
        functools.partial(_attn_kernel, tq=tq, tk=tq, n_prefix=n_prefix,
                          n_prefix_valid=n_prefix_valid, causal=causal),
        grid=(bsz, hp_n, seq // tq),
        in_specs=in_specs,
        out_specs=pl.BlockSpec((1, tq, HEADS_PER_STEP * V_DIM), lambda b, hp, i: (b, i, hp)),
        out_shape=jax.ShapeDtypeStruct((bsz, seq, ATTN_W), jnp.bfloat16),
        scratch_shapes=scratch,
        compiler_params=pltpu.CompilerParams(
            dimension_semantics=("arbitrary", "arbitrary", "arbitrary"),
            vmem_limit_bytes=VMEM_LIMIT_BYTES),
        name="attn_causal" if causal else "attn_dense",
    )(*args)


def _out_mlp_kernel(o_ref, y_ref, x_ref, g_attn_ref, w_out_ref, g_mlp_ref, w_up_ref, w_down_ref,
                    g_fin_ref, out_ref, hn_scr, *, final):
    c = pl.program_id(1)

    @pl.when(c == 0)
    def _():
        on = _rms(o_ref[...].astype(jnp.float32), g_attn_ref[...]).astype(jnp.bfloat16)
        h = (x_ref[...] + _dot(on, w_out_ref[0:ATTN_W, :])
             + _dot(y_ref[...], w_out_ref[ATTN_W:ATTN_W + LRU_W, :]))
        hn_scr[...] = _rms(h, g_mlp_ref[...]).astype(jnp.bfloat16)
        out_ref[...] = h

    u = _dot(hn_scr[...], w_up_ref[...])
    u = jnp.square(jnp.maximum(u, 0.0)).astype(jnp.bfloat16)
    out_ref[...] += _dot(u, w_down_ref[...])

    if final:
        @pl.when(c == pl.num_programs(1) - 1)
        def _():
            out_ref[...] = _rms(out_ref[...], g_fin_ref[...])


def _out_mlp_call(o, y, x, lw, g_fin, *, tm, final):
    rows = x.shape[0]
    const = lambda r, c: (0, 0)
    return pl.pallas_call(
        functools.partial(_out_mlp_kernel, final=final),
        grid=(rows // tm, D_FF // F_CHUNK),
        in_specs=[
            pl.BlockSpec((tm, ATTN_W), lambda r, c: (r, 0)),
            pl.BlockSpec((tm, LRU_W), lambda r, c: (r, 0)),
            pl.BlockSpec((tm, D_MODEL), lambda r, c: (r, 0)),
            pl.BlockSpec((1, ATTN_W), const),
            pl.BlockSpec((D_MODEL, D_MODEL), const),
            pl.BlockSpec((1, D_MODEL), const),
            pl.BlockSpec((D_MODEL, F_CHUNK), lambda r, c: (0, c)),
            pl.BlockSpec((F_CHUNK, D_MODEL), lambda r, c: (c, 0)),
            pl.BlockSpec((1, D_MODEL), const),
        ],
        out_specs=pl.BlockSpec((tm, D_MODEL), lambda r, c: (r, 0)),
        out_shape=jax.ShapeDtypeStruct((rows, D_MODEL), jnp.float32),
        scratch_shapes=[pltpu.VMEM((tm, D_MODEL), jnp.bfloat16)],
        compiler_params=pltpu.CompilerParams(
            dimension_semantics=("arbitrary", "arbitrary"), vmem_limit_bytes=VMEM_LIMIT_BYTES),
        name="out_mlp",
    )(o, y, x, lw["g_attn"], lw["w_out"], lw["g_mlp"], lw["w_up"], lw["w_down"], g_fin)


def _rope_tables(pos):
    inv = ROPE_THETA ** (-jnp.arange(0, QK_ROPE, 2, dtype=jnp.float32) / QK_ROPE)
    ang = pos.astype(jnp.float32)[:, None] * inv[None, :]
    cos, sin = jnp.cos(ang), jnp.sin(ang)
    zeros = jnp.zeros((pos.shape[0], LANES - QK_ROPE), jnp.float32)
    z16 = jnp.zeros_like(sin)
    tab_c = jnp.concatenate([cos, cos, zeros], axis=1)
    tab_m = jnp.concatenate([-sin, z16, zeros], axis=1)
    tab_p = jnp.concatenate([z16, sin, zeros], axis=1)
    return jnp.stack([tab_c, tab_m, tab_p])


def _query_tables(tabk):
    tq = jnp.roll(tabk, QK_NOPE, axis=2)
    tq = tq.at[0, :, :QK_NOPE].set(1.0)
    return tq * SM_SCALE


def _block_diag_gates(w_a, w_x):
    per_half = LRU_BLOCKS // 2
    halves = []
    for s in range(2):
        bd_a = jax.scipy.linalg.block_diag(*[w_a[s * per_half + i] for i in range(per_half)])
        bd_x = jax.scipy.linalg.block_diag(*[w_x[s * per_half + i] for i in range(per_half)])
        halves.append(jnp.concatenate([bd_a, bd_x], axis=1))
    return jnp.stack(halves).astype(jnp.bfloat16)


def _layer_weights(l, norm_mix_g, w_in, q_norm_g, w_uq, kv_norm_g, w_ukv, conv_w, conv_b,
                   w_gate_a, b_gate_a, w_gate_x, b_gate_x, lru_lambda, attn_out_g, lru_out_g,
                   w_out, norm_mlp_g, w_up, w_down):
    bf = jnp.bfloat16
    row = lambda v: v.reshape(1, -1).astype(jnp.float32)
    w_in_p = jnp.concatenate(
        [w_in[l][:, :COL_KR + QK_ROPE], jnp.zeros((D_MODEL, ROPE_PAD - QK_ROPE), w_in.dtype),
         w_in[l][:, COL_KR + QK_ROPE:]], axis=1).astype(bf)
    w_uq_p = jnp.pad(w_uq[l].reshape(Q_LORA, N_HEADS, QK_NOPE + QK_ROPE),
                     ((0, 0), (0, 0), (0, HEAD_PAD - QK_NOPE - QK_ROPE))).reshape(Q_LORA, -1).astype(bf)
    w_ukv_h = w_ukv[l].reshape(KV_LORA, N_HEADS, QK_NOPE + V_DIM)
    w_uk_p = jnp.pad(w_ukv_h[:, :, :QK_NOPE],
                     ((0, 0), (0, 0), (0, HEAD_PAD - QK_NOPE))).reshape(KV_LORA, -1).astype(bf)
    w_uvt = w_ukv_h[:, :, QK_NOPE:].reshape(KV_LORA, ATTN_W).T.astype(bf)
    return {
        "g_mix": row(norm_mix_g[l]), "w_in": w_in_p, "g_q": row(q_norm_g[l]), "w_uq": w_uq_p,
        "g_kv": row(kv_norm_g[l]), "w_uk": w_uk_p, "w_uvt": w_uvt,
        "conv_w": conv_w[l].astype(jnp.float32), "conv_b": row(conv_b[l]),
        "wg": _block_diag_gates(w_gate_a[l], w_gate_x[l]),
        "bg": jnp.stack([b_gate_a[l], b_gate_x[l]]).astype(jnp.float32),
        "lam": row(lru_lambda[l]), "g_lru": row(lru_out_g[l]), "g_attn": row(attn_out_g[l]),
        "w_out": w_out[l].astype(bf), "g_mlp": row(norm_mlp_g[l]),
        "w_up": w_up[l].astype(bf), "w_down": w_down[l].astype(bf),
    }


def _pad_buf(buf):
    return jnp.pad(buf, ((0, 0), (SUBLANES - (CONV_W - 1), 0), (0, 0)))


def _pad_axis(a, axis, size):
    pad = [(0, 0)] * a.ndim
    pad[axis] = (0, size - a.shape[axis])
    return jnp.pad(a, pad)


def kernel(x_prompt, x_sample, cache_ckv, cache_kpe, state_lru_h, state_conv, meta_tokens,
           norm_mix_g, w_in, q_norm_g, w_uq, kv_norm_g, w_ukv, conv_w, conv_b,
           w_gate_a, b_gate_a, w_gate_x, b_gate_x, lru_lambda, attn_out_g, lru_out_g,
           w_out, norm_mlp_g, w_up, w_down, final_norm_g):
    b_p, seq, _ = x_prompt.shape
    b_s, dec_seq, _ = x_sample.shape
    past_len = cache_ckv.shape[2]
    assert (past_len + dec_seq - 1) // CHUNK == past_len // CHUNK
    tm_p = 512
    tq_p = 512
    tm_o = 1024
    assert seq % tm_p == 0 and seq % tq_p == 0 and tq_p % CHUNK == 0

    meta_pos = jnp.arange(-N_META, 0, dtype=jnp.int32)
    prompt_pos = jnp.arange(seq, dtype=jnp.int32)
    sample_pos = past_len + jnp.arange(dec_seq, dtype=jnp.int32)
    tabk_m, tabk_p, tabk_s = _rope_tables(meta_pos), _rope_tables(prompt_pos), _rope_tables(sample_pos)
    tabq_m, tabq_p, tabq_s = _query_tables(tabk_m), _query_tables(tabk_p), _query_tables(tabk_s)
    g_fin = final_norm_g.reshape(1, -1).astype(jnp.float32)

    n_keys_s = N_META + past_len + dec_seq
    n_keys_s_pad = -(-n_keys_s // LANES) * LANES
    tq_small = LANES

    h_meta = meta_tokens[None].astype(jnp.float32)
    h_p, h_s = x_prompt, x_sample
    zero_h = jnp.zeros((1, 1, LRU_W), jnp.float32)
    zero_buf = jnp.zeros((1, SUBLANES, LRU_W), jnp.float32)
    outs = {name: [] for name in ("ckv_p", "kpe_p", "lru_p", "conv_p", "ckv_s", "kpe_s", "lru_s", "conv_s")}
    for l in range(DEPTH):
        lw = _layer_weights(l, norm_mix_g, w_in, q_norm_g, w_uq, kv_norm_g, w_ukv, conv_w, conv_b,
                            w_gate_a, b_gate_a, w_gate_x, b_gate_x, lru_lambda, attn_out_g,
                            lru_out_g, w_out, norm_mlp_g, w_up, w_down)
        last = l + 1 == DEPTH
        mq, mk, mvt, m_ckv, m_kpe, m_y, m_h, m_buf = _mixer_call(
            h_meta, zero_h, zero_buf, tabq_m, tabk_m, lw, tm=N_META, shared_state=True)
        pq, pk, pvt, p_ckv, p_kpe, p_y, p_h, p_buf = _mixer_call(
            h_p, m_h, _pad_buf(m_buf), tabq_p, tabk_p, lw, tm=tm_p, shared_state=True)
        mk_p, mvt_p = _pad_axis(mk, 2, LANES), _pad_axis(mvt, 2, LANES)
        p_o = _attn_call(pq, mk_p, mvt_p, pk, pvt, tq=tq_p, n_prefix_valid=N_META, prefix_shared=True)
        h_p = _out_mlp_call(p_o.reshape(b_p * seq, ATTN_W), p_y.reshape(b_p * seq, LRU_W),
                            h_p.reshape(b_p * seq, D_MODEL), lw, g_fin, tm=tm_o,
                            final=last).reshape(b_p, seq, D_MODEL)
        sq, sk, svt, s_ckv, s_kpe, s_y, s_h, s_buf = _mixer_call(
            h_s, state_lru_h[l][:, None, :], _pad_buf(state_conv[l]), tabq_s, tabk_s, lw,
            tm=dec_seq, shared_state=False)
        ck, cvt = _kvup_call(cache_ckv[l], _pad_axis(cache_kpe[l], 2, LANES), lw, tm=past_len)
        k_all = jnp.concatenate([jnp.broadcast_to(mk, (b_s,) + mk.shape[1:]), ck, sk], axis=2)
        vt_all = jnp.concatenate([jnp.broadcast_to(mvt, (b_s,) + mvt.shape[1:]), cvt, svt], axis=2)
        s_o = _attn_call(_pad_axis(sq, 2, tq_small), _pad_axis(k_all, 2, n_keys_s_pad),
                         _pad_axis(vt_all, 2, n_keys_s_pad), None, None, tq=tq_small,
                         n_prefix_valid=n_keys_s, prefix_shared=False)[:, :dec_seq]
        h_s = _out_mlp_call(s_o.reshape(b_s * dec_seq, ATTN_W), s_y.reshape(b_s * dec_seq, LRU_W),
                            h_s.reshape(b_s * dec_seq, D_MODEL), lw, g_fin, tm=b_s * dec_seq,
                            final=last).reshape(b_s, dec_seq, D_MODEL)
        if not last:
            m_o = _attn_call(_pad_axis(mq, 2, tq_small), mk_p, mvt_p, None, None, tq=tq_small,
                             n_prefix_valid=N_META, prefix_shared=False)[:, :N_META]
            h_meta = _out_mlp_call(m_o.reshape(N_META, ATTN_W), m_y.reshape(N_META, LRU_W),
                                   h_meta.reshape(N_META, D_MODEL), lw, g_fin, tm=N_META,
                                   final=False).reshape(1, N_META, D_MODEL)
        outs["ckv_p"].append(p_ckv); outs["kpe_p"].append(p_kpe)
        outs["lru_p"].append(p_h[:, 0]); outs["conv_p"].append(p_buf)
        outs["ckv_s"].append(s_ckv); outs["kpe_s"].append(s_kpe)
        outs["lru_s"].append(s_h[:, 0]); outs["conv_s"].append(s_buf)

    return (h_p, h_s, jnp.stack(outs["ckv_p"]), jnp.stack(outs["kpe_p"]), jnp.stack(outs["lru_p"]),
            jnp.stack(outs["conv_p"]), jnp.stack(outs["ckv_s"]), jnp.stack(outs["kpe_s"]),
            jnp.stack(outs["lru_s"]), jnp.stack(outs["conv_s"]))
```

```python
import functools

import jax
import jax.numpy as jnp
from jax import lax
from jax.experimental import pallas as pl
from jax.experimental.pallas import tpu as pltpu

D_MODEL = 1024
DEPTH = 2
CHUNK = 64
N_META = 16
ATTN_W = 512
LRU_W = 512
V_DIM = 64
N_HEADS = 8
QK_NOPE = 64
QK_ROPE = 32
Q_LORA = 768
KV_LORA = 256
LRU_BLOCKS = 8
LRU_BW = LRU_W // LRU_BLOCKS
LRU_C = 8.0
CONV_W = 4
D_FF = 4 * D_MODEL
ROPE_THETA = 10000.0
EPS = 1e-6
SM_SCALE = (QK_NOPE + QK_ROPE) ** -0.5
NEG_INF = -1e30
LOG2_E = 1.4426950408889634

LANES = 128
SUBLANES = 8
HEAD_PAD = LANES
VMEM_LIMIT_BYTES = 56 * 1024 * 1024

ROPE_PAD = LANES
COL_CQ = 0
COL_CKV = Q_LORA
COL_KR = Q_LORA + KV_LORA
COL_XR = COL_KR + ROPE_PAD
COL_GR = COL_XR + LRU_W
IN_W_PAD = COL_GR + LRU_W

HEADS_PER_STEP = 2
ONES_ROWS = 16
V_AUG = V_DIM + ONES_ROWS
F_CHUNK = 1024


def _rms(x, g):
    return x * lax.rsqrt(jnp.mean(x * x, axis=-1, keepdims=True) + EPS) * g


def _rope_lanes(x, tab_ref):
    half = QK_ROPE // 2
    return (x * tab_ref[0] + pltpu.roll(x, LANES - half, 1) * tab_ref[1]
            + pltpu.roll(x, half, 1) * tab_ref[2])


def _dot(a, b):
    return jnp.dot(a, b, preferred_element_type=jnp.float32)


def _dot_nt(a, b):
    return lax.dot_general(a, b, (((1,), (1,)), ((), ())), preferred_element_type=jnp.float32)


def _mixer_kernel(x_ref, h0_ref, buf0_ref, tabq_ref, tabk_ref, g_mix_ref, w_in_ref, g_q_ref,
                  w_uq_ref, g_kv_ref, w_uk_ref, w_uvt_ref, conv_w_ref, conv_b_ref, wg_ref, bg_ref,
                  lam_ref, g_lru_ref,
                  q_ref, k_ref, vt_ref, ckv_ref, kpe_ref, y_ref, hlast_ref, buf_ref,
                  xpad_scr, hcar_scr, *, tm):
    t = pl.program_id(1)

    @pl.when(t == 0)
    def _():
        xpad_scr[0:SUBLANES, :] = buf0_ref[0]
        hcar_scr[...] = h0_ref[0]

    x = x_ref[0]
    xn = _rms(x, g_mix_ref[...]).astype(jnp.bfloat16)
    z = _dot(xn, w_in_ref[...])

    cqn = _rms(z[:, COL_CQ:COL_CQ + Q_LORA], g_q_ref[...]).astype(jnp.bfloat16)
    q = _dot(cqn, w_uq_ref[...])
    for h in range(N_HEADS):
        qh = q[:, h * HEAD_PAD:(h + 1) * HEAD_PAD]
        q_ref[0, h] = _rope_lanes(qh, tabq_ref).astype(jnp.bfloat16)

    ckv = _rms(z[:, COL_CKV:COL_CKV + KV_LORA], g_kv_ref[...])
    ckv_ref[0] = ckv
    ckv_b = ckv.astype(jnp.bfloat16)
    kpe = _rope_lanes(z[:, COL_KR:COL_KR + ROPE_PAD], tabk_ref)
    kpe_ref[0] = kpe[:, :QK_ROPE]
    kpe_shift = pltpu.roll(kpe, QK_NOPE, 1)
    kk = _dot(ckv_b, w_uk_ref[...])
    for h in range(N_HEADS):
        k_ref[0, h] = (kk[:, h * HEAD_PAD:(h + 1) * HEAD_PAD] + kpe_shift).astype(jnp.bfloat16)
    vt_ref[0] = _dot_nt(w_uvt_ref[...], ckv_b).astype(jnp.bfloat16)

    x_r = z[:, COL_XR:COL_XR + LRU_W]
    g_r = z[:, COL_GR:COL_GR + LRU_W]
    xpad_scr[SUBLANES:SUBLANES + tm, :] = x_r
    xc = conv_b_ref[...] + x_r * conv_w_ref[CONV_W - 1:CONV_W, :]
    for kk_i in range(CONV_W - 1):
        back = CONV_W - 1 - kk_i
        xc = xc + xpad_scr[SUBLANES - back:SUBLANES - back + tm, :] * conv_w_ref[kk_i:kk_i + 1, :]
    buf_ref[0] = xpad_scr[tm + SUBLANES - (CONV_W - 1):tm + SUBLANES, :]
    xpad_scr[0:SUBLANES, :] = xpad_scr[tm:tm + SUBLANES, :]

    half_w = LRU_W // 2
    gates = [_dot(xc[:, s * half_w:(s + 1) * half_w].astype(jnp.bfloat16), wg_ref[s]) for s in range(2)]
    pre_a = jnp.concatenate([gates[0][:, :half_w], gates[1][:, :half_w]], axis=1)
    pre_x = jnp.concatenate([gates[0][:, half_w:], gates[1][:, half_w:]], axis=1)
    r = jax.nn.sigmoid(pre_a + bg_ref[0:1, :])
    gi = jax.nn.sigmoid(pre_x + bg_ref[1:2, :])
    neg_lam = -lam_ref[...]
    softplus = jnp.maximum(neg_lam, 0.0) + jnp.log1p(jnp.exp(-jnp.abs(neg_lam)))
    log_a = (-LRU_C) * r * softplus
    a = jnp.exp(log_a)
    b = jnp.sqrt(-jnp.tanh(log_a) * (a * a + 1.0)) * (gi * xc)

    rows = lax.broadcasted_iota(jnp.int32, (tm, LRU_W), 0)
    shift = 1
    while shift < tm:
        keep = rows >= shift
        a_prev = jnp.where(keep, pltpu.roll(a, shift, 0), 1.0)
        b_prev = jnp.where(keep, pltpu.roll(b, shift, 0), 0.0)
        b = a * b_prev + b
        a = a * a_prev
        shift *= 2
    hh = a * hcar_scr[...] + b
    h_last = hh[tm - 1:tm, :]
    hcar_scr[...] = h_last
    hlast_ref[0] = h_last
    y = hh * jax.nn.gelu(g_r)
    y_ref[0] = _rms(y, g_lru_ref[...]).astype(jnp.bfloat16)


def _mixer_call(x, h0, buf0p, tabq, tabk, lw, *, tm, shared_state):
    bsz, seq, _ = x.shape
    nt = seq // tm
    const2 = lambda b, t: (0, 0)
    const3 = lambda b, t: (0, 0, 0)
    state_idx = (lambda b, t: (0, 0, 0)) if shared_state else (lambda b, t: (b, 0, 0))
    in_specs = [
        pl.BlockSpec((1, tm, D_MODEL), lambda b, t: (b, t, 0)),
        pl.BlockSpec((1, 1, LRU_W), state_idx),
        pl.BlockSpec((1, SUBLANES, LRU_W), state_idx),
        pl.BlockSpec((3, tm, LANES), lambda b, t: (0, t, 0)),
        pl.BlockSpec((3, tm, LANES), lambda b, t: (0, t, 0)),
        pl.BlockSpec((1, D_MODEL), const2),
        pl.BlockSpec((D_MODEL, IN_W_PAD), const2),
        pl.BlockSpec((1, Q_LORA), const2),
        pl.BlockSpec((Q_LORA, N_HEADS * HEAD_PAD), const2),
        pl.BlockSpec((1, KV_LORA), const2),
        pl.BlockSpec((KV_LORA, N_HEADS * HEAD_PAD), const2),
        pl.BlockSpec((ATTN_W, KV_LORA), const2),
        pl.BlockSpec((CONV_W, LRU_W), const2),
        pl.BlockSpec((1, LRU_W), const2),
        pl.BlockSpec((2, LRU_W // 2, LRU_W), const3),
        pl.BlockSpec((2, LRU_W), const2),
        pl.BlockSpec((1, LRU_W), const2),
        pl.BlockSpec((1, LRU_W), const2),
    ]
    out_shape = [
        jax.ShapeDtypeStruct((bsz, N_HEADS, seq, HEAD_PAD), jnp.bfloat16),
        jax.ShapeDtypeStruct((bsz, N_HEADS, seq, HEAD_PAD), jnp.bfloat16),
        jax.ShapeDtypeStruct((bsz, ATTN_W, seq), jnp.bfloat16),
        jax.ShapeDtypeStruct((bsz, seq, KV_LORA), jnp.float32),
        jax.ShapeDtypeStruct((bsz, seq, QK_ROPE), jnp.float32),
        jax.ShapeDtypeStruct((bsz, seq, LRU_W), jnp.bfloat16),
        jax.ShapeDtypeStruct((bsz, 1, LRU_W), jnp.float32),
        jax.ShapeDtypeStruct((bsz, CONV_W - 1, LRU_W), jnp.float32),
    ]
    out_specs = [
        pl.BlockSpec((1, N_HEADS, tm, HEAD_PAD), lambda b, t: (b, 0, t, 0)),
        pl.BlockSpec((1, N_HEADS, tm, HEAD_PAD), lambda b, t: (b, 0, t, 0)),
        pl.BlockSpec((1, ATTN_W, tm), lambda b, t: (b, 0, t)),
        pl.BlockSpec((1, tm, KV_LORA), lambda b, t: (b, t, 0)),
        pl.BlockSpec((1, tm, QK_ROPE), lambda b, t: (b, t, 0)),
        pl.BlockSpec((1, tm, LRU_W), lambda b, t: (b, t, 0)),
        pl.BlockSpec((1, 1, LRU_W), lambda b, t: (b, 0, 0)),
        pl.BlockSpec((1, CONV_W - 1, LRU_W), lambda b, t: (b, 0, 0)),
    ]
    return pl.pallas_call(
        functools.partial(_mixer_kernel, tm=tm),
        grid=(bsz, nt),
        in_specs=in_specs,
        out_specs=out_specs,
        out_shape=out_shape,
        scratch_shapes=[pltpu.VMEM((tm + SUBLANES, LRU_W), jnp.float32),
                        pltpu.VMEM((1, LRU_W), jnp.float32)],
        compiler_params=pltpu.CompilerParams(
            dimension_semantics=("arbitrary", "arbitrary"), vmem_limit_bytes=VMEM_LIMIT_BYTES),
        name="mixer",
    )(x, h0, buf0p, tabq, tabk, lw["g_mix"], lw["w_in"], lw["g_q"], lw["w_uq"], lw["g_kv"],
      lw["w_uk"], lw["w_uvt"], lw["conv_w"], lw["conv_b"], lw["wg"], lw["bg"], lw["lam"], lw["g_lru"])


def _kvup_kernel(ckv_ref, kpe_ref, w_uk_ref, w_uvt_ref, k_ref, vt_ref):
    ckv_b = ckv_ref[0].astype(jnp.bfloat16)
    kpe_shift = pltpu.roll(kpe_ref[0], QK_NOPE, 1)
    kk = _dot(ckv_b, w_uk_ref[...])
    for h in range(N_HEADS):
        k_ref[0, h] = (kk[:, h * HEAD_PAD:(h + 1) * HEAD_PAD] + kpe_shift).astype(jnp.bfloat16)
    vt_ref[0] = _dot_nt(w_uvt_ref[...], ckv_b).astype(jnp.bfloat16)


def _kvup_call(ckv, kpe_pad, lw, *, tm):
    bsz, seq, _ = ckv.shape
    return pl.pallas_call(
        _kvup_kernel,
        grid=(bsz, seq // tm),
        in_specs=[
            pl.BlockSpec((1, tm, KV_LORA), lambda b, t: (b, t, 0)),
            pl.BlockSpec((1, tm, LANES), lambda b, t: (b, t, 0)),
            pl.BlockSpec((KV_LORA, N_HEADS * HEAD_PAD), lambda b, t: (0, 0)),
            pl.BlockSpec((ATTN_W, KV_LORA), lambda b, t: (0, 0)),
        ],
        out_specs=[
            pl.BlockSpec((1, N_HEADS, tm, HEAD_PAD), lambda b, t: (b, 0, t, 0)),
            pl.BlockSpec((1, ATTN_W, tm), lambda b, t: (b, 0, t)),
        ],
        out_shape=[
            jax.ShapeDtypeStruct((bsz, N_HEADS, seq, HEAD_PAD), jnp.bfloat16),
            jax.ShapeDtypeStruct((bsz, ATTN_W, seq), jnp.bfloat16),
        ],
        compiler_params=pltpu.CompilerParams(
            dimension_semantics=("arbitrary", "arbitrary"), vmem_limit_bytes=VMEM_LIMIT_BYTES),
        name="kv_up",
    )(ckv, kpe_pad, lw["w_uk"], lw["w_uvt"])


def _col_max(st):
    rows = st.shape[0]
    while rows > SUBLANES and rows % (2 * SUBLANES) == 0:
        rows //= 2
        st = jnp.maximum(st[:rows], st[rows:])
    return jnp.max(st, axis=0, keepdims=True)


def _softmax_block(st, vt_aug, m_old, acc_old):
    m_blk = _col_max(st)
    if m_old is None:
        m_new = m_blk
        p = jnp.exp2(st - m_new).astype(jnp.bfloat16)
        return m_new, _dot(vt_aug, p)
    m_new = jnp.maximum(m_old, m_blk)
    p = jnp.exp2(st - m_new).astype(jnp.bfloat16)
    return m_new, jnp.exp2(m_old - m_new) * acc_old + _dot(vt_aug, p)


def _attn_kernel(*refs, tq, tk, n_prefix, n_prefix_valid, causal):
    if causal:
        q_ref, kp_ref, vtp_ref, k_ref, vt_ref, o_ref, sa_scr, sb_scr, m_scr, acc_scr = refs
    else:
        q_ref, kp_ref, vtp_ref, o_ref = refs
    i = pl.program_id(2)
    ones_p = jnp.ones((ONES_ROWS, n_prefix), jnp.bfloat16)

    ms, accs = [], []
    for hh in range(HEADS_PER_STEP):
        st = _dot_nt(kp_ref[0, hh], q_ref[0, hh])
        if n_prefix_valid < n_prefix:
            key_idx = lax.broadcasted_iota(jnp.int32, (n_prefix, tq), 0)
            st = jnp.where(key_idx < n_prefix_valid, st, NEG_INF)
        vt_aug = jnp.concatenate([vtp_ref[0, hh * V_DIM:(hh + 1) * V_DIM, :], ones_p], axis=0)
        m, acc = _softmax_block(st, vt_aug, None, None)
        ms.append(m)
        accs.append(acc)

    if causal:
        ones_k = jnp.ones((ONES_ROWS, tk), jnp.bfloat16)
        for hh in range(HEADS_PER_STEP):
            m_scr[hh] = ms[hh]
            acc_scr[hh] = accs[hh]

        def scores(tile, s_ref):
            start = pl.multiple_of(tile * tk, tk)
            for hh in range(HEADS_PER_STEP):
                s_ref[hh] = _dot_nt(k_ref[0, hh, pl.ds(start, tk), :], q_ref[0, hh])

        def softmax_pv(tile, s_ref, mask):
            start = pl.multiple_of(tile * tk, tk)
            for hh in range(HEADS_PER_STEP):
                st = s_ref[hh]
                if mask is not None:
                    st = jnp.where(mask, st, NEG_INF)
                vt_t = vt_ref[0, hh * V_DIM:(hh + 1) * V_DIM, pl.ds(start, tk)]
                vt_aug = jnp.concatenate([vt_t, ones_k], axis=0)
                m, acc = _softmax_block(st, vt_aug, m_scr[hh], acc_scr[hh])
                m_scr[hh] = m
                acc_scr[hh] = acc

        s_bufs = (sa_scr, sb_scr)
        scores(0, sa_scr)

        def body(j, carry):
            for parity in range(2):
                @pl.when(lax.rem(j, 2) == parity)
                def _():
                    scores(j + 1, s_bufs[1 - parity])
                    softmax_pv(j, s_bufs[parity], None)
            return carry

        lax.fori_loop(0, i, body, 0)
        key_chunk = lax.broadcasted_iota(jnp.int32, (tk, tq), 0) // CHUNK
        qry_chunk = lax.broadcasted_iota(jnp.int32, (tk, tq), 1) // CHUNK
        for parity in range(2):
            @pl.when(lax.rem(i, 2) == parity)
            def _():
                softmax_pv(i, s_bufs[parity], key_chunk <= qry_chunk)
        accs = [acc_scr[hh] for hh in range(HEADS_PER_STEP)]

    outs = [acc[:V_DIM, :] / acc[V_DIM:V_DIM + 1, :] for acc in accs]
    o_t = jnp.concatenate(outs, axis=0)
    o_ref[0] = o_t.T.astype(jnp.bfloat16)


def _attn_call(q, kp, vtp, k, vt, *, tq, n_prefix_valid, prefix_shared):
    bsz, _, seq, _ = q.shape
    n_prefix = kp.shape[2]
    causal = k is not None
    hp_n = N_HEADS // HEADS_PER_STEP
    pidx4 = (lambda b, hp, i: (0, hp, 0, 0)) if prefix_shared else (lambda b, hp, i: (b, hp, 0, 0))
    pidx3 = (lambda b, hp, i: (0, hp, 0)) if prefix_shared else (lambda b, hp, i: (b, hp, 0))
    in_specs = [
        pl.BlockSpec((1, HEADS_PER_STEP, tq, HEAD_PAD), lambda b, hp, i: (b, hp, i, 0)),
        pl.BlockSpec((1, HEADS_PER_STEP, n_prefix, HEAD_PAD), pidx4),
        pl.BlockSpec((1, HEADS_PER_STEP * V_DIM, n_prefix), pidx3),
    ]
    args = [q, kp, vtp]
    scratch = []
    if causal:
        in_specs += [
            pl.BlockSpec((1, HEADS_PER_STEP, seq, HEAD_PAD), lambda b, hp, i: (b, hp, 0, 0)),
            pl.BlockSpec((1, HEADS_PER_STEP * V_DIM, seq), lambda b, hp, i: (b, hp, 0)),
        ]
        args += [k, vt]
        scratch = [pltpu.VMEM((HEADS_PER_STEP, tq, tq), jnp.float32),
                   pltpu.VMEM((HEADS_PER_STEP, tq, tq), jnp.float32),
                   pltpu.VMEM((HEADS_PER_STEP, 1, tq), jnp.float32),
                   pltpu.VMEM((HEADS_PER_STEP, V_AUG, tq), jnp.float32)]
    return pl.pallas_call(
        functools.partial(_attn_kernel, tq=tq, tk=tq, n_prefix=n_prefix,
                          n_prefix_valid=n_prefix_valid, causal=causal),
        grid=(bsz, hp_n, seq // tq),
        in_specs=in_specs,
        out_specs=pl.BlockSpec((1, tq, HEADS_PER_STEP * V_DIM), lambda b, hp, i: (b, i, hp)),
        out_shape=jax.ShapeDtypeStruct((bsz, seq, ATTN_W), jnp.bfloat16),
        scratch_shapes=scratch,
        compiler_params=pltpu.CompilerParams(
            dimension_semantics=("arbitrary", "arbitrary", "arbitrary"),
            vmem_limit_bytes=VMEM_LIMIT_BYTES),
        name="attn_causal" if causal else "attn_dense",
    )(*args)


def _out_mlp_kernel(o_ref, y_ref, x_ref, g_attn_ref, w_out_ref, g_mlp_ref, w_up_ref, w_down_ref,
                    g_fin_ref, out_ref, hn_scr, *, final):
    c = pl.program_id(1)

    @pl.when(c == 0)
    def _():
        on = _rms(o_ref[...].astype(jnp.float32), g_attn_ref[...]).astype(jnp.bfloat16)
        h = (x_ref[...] + _dot(on, w_out_ref[0:ATTN_W, :])
             + _dot(y_ref[...], w_out_ref[ATTN_W:ATTN_W + LRU_W, :]))
        hn_scr[...] = _rms(h, g_mlp_ref[...]).astype(jnp.bfloat16)
        out_ref[...] = h

    u = _dot(hn_scr[...], w_up_ref[...])
    u = jnp.square(jnp.maximum(u, 0.0)).astype(jnp.bfloat16)
    out_ref[...] += _dot(u, w_down_ref[...])

    if final:
        @pl.when(c == pl.num_programs(1) - 1)
        def _():
            out_ref[...] = _rms(out_ref[...], g_fin_ref[...])


def _out_mlp_call(o, y, x, lw, g_fin, *, tm, final):
    rows = x.shape[0]
    const = lambda r, c: (0, 0)
    return pl.pallas_call(
        functools.partial(_out_mlp_kernel, final=final),
        grid=(rows // tm, D_FF // F_CHUNK),
        in_specs=[
            pl.BlockSpec((tm, ATTN_W), lambda r, c: (r, 0)),
            pl.BlockSpec((tm, LRU_W), lambda r, c: (r, 0)),
            pl.BlockSpec((tm, D_MODEL), lambda r, c: (r, 0)),
            pl.BlockSpec((1, ATTN_W), const),
            pl.BlockSpec((D_MODEL, D_MODEL), const),
            pl.BlockSpec((1, D_MODEL), const),
            pl.BlockSpec((D_MODEL, F_CHUNK), lambda r, c: (0, c)),
            pl.BlockSpec((F_CHUNK, D_MODEL), lambda r, c: (c, 0)),
            pl.BlockSpec((1, D_MODEL), const),
        ],
        out_specs=pl.BlockSpec((tm, D_MODEL), lambda r, c: (r, 0)),
        out_shape=jax.ShapeDtypeStruct((rows, D_MODEL), jnp.float32),
        scratch_shapes=[pltpu.VMEM((tm, D_MODEL), jnp.bfloat16)],
        compiler_params=pltpu.CompilerParams(
            dimension_semantics=("arbitrary", "arbitrary"), vmem_limit_bytes=VMEM_LIMIT_BYTES),
        name="out_mlp",
    )(o, y, x, lw["g_attn"], lw["w_out"], lw["g_mlp"], lw["w_up"], lw["w_down"], g_fin)


def _rope_tables(pos):
    inv = ROPE_THETA ** (-jnp.arange(0, QK_ROPE, 2, dtype=jnp.float32) / QK_ROPE)
    ang = pos.astype(jnp.float32)[:, None] * inv[None, :]
    cos, sin = jnp.cos(ang), jnp.sin(ang)
    zeros = jnp.zeros((pos.shape[0], LANES - QK_ROPE), jnp.float32)
    z16 = jnp.zeros_like(sin)
    tab_c = jnp.concatenate([cos, cos, zeros], axis=1)
    tab_m = jnp.concatenate([-sin, z16, zeros], axis=1)
    tab_p = jnp.concatenate([z16, sin, zeros], axis=1)
    return jnp.stack([tab_c, tab_m, tab_p])


def _query_tables(tabk):
    tq = jnp.roll(tabk, QK_NOPE, axis=2)
    tq = tq.at[0, :, :QK_NOPE].set(1.0)
    return tq * (SM_SCALE * LOG2_E)


def _block_diag_gates(w_a, w_x):
    per_half = LRU_BLOCKS // 2
    halves = []
    for s in range(2):
        bd_a = jax.scipy.linalg.block_diag(*[w_a[s * per_half + i] for i in range(per_half)])
        bd_x = jax.scipy.linalg.block_diag(*[w_x[s * per_half + i] for i in range(per_half)])
        halves.append(jnp.concatenate([bd_a, bd_x], axis=1))
    return jnp.stack(halves).astype(jnp.bfloat16)


def _layer_weights(l, norm_mix_g, w_in, q_norm_g, w_uq, kv_norm_g, w_ukv, conv_w, conv_b,
                   w_gate_a, b_gate_a, w_gate_x, b_gate_x, lru_lambda, attn_out_g, lru_out_g,
                   w_out, norm_mlp_g, w_up, w_down):
    bf = jnp.bfloat16
    row = lambda v: v.reshape(1, -1).astype(jnp.float32)
    w_in_p = jnp.concatenate(
        [w_in[l][:, :COL_KR + QK_ROPE], jnp.zeros((D_MODEL, ROPE_PAD - QK_ROPE), w_in.dtype),
         w_in[l][:, COL_KR + QK_ROPE:]], axis=1).astype(bf)
    w_uq_p = jnp.pad(w_uq[l].reshape(Q_LORA, N_HEADS, QK_NOPE + QK_ROPE),
                     ((0, 0), (0, 0), (0, HEAD_PAD - QK_NOPE - QK_ROPE))).reshape(Q_LORA, -1).astype(bf)
    w_ukv_h = w_ukv[l].reshape(KV_LORA, N_HEADS, QK_NOPE + V_DIM)
    w_uk_p = jnp.pad(w_ukv_h[:, :, :QK_NOPE],
                     ((0, 0), (0, 0), (0, HEAD_PAD - QK_NOPE))).reshape(KV_LORA, -1).astype(bf)
    w_uvt = w_ukv_h[:, :, QK_NOPE:].reshape(KV_LORA, ATTN_W).T.astype(bf)
    return {
        "g_mix": row(norm_mix_g[l]), "w_in": w_in_p, "g_q": row(q_norm_g[l]), "w_uq": w_uq_p,
        "g_kv": row(kv_norm_g[l]), "w_uk": w_uk_p, "w_uvt": w_uvt,
        "conv_w": conv_w[l].astype(jnp.float32), "conv_b": row(conv_b[l]),
        "wg": _block_diag_gates(w_gate_a[l], w_gate_x[l]),
        "bg": jnp.stack([b_gate_a[l], b_gate_x[l]]).astype(jnp.float32),
        "lam": row(lru_lambda[l]), "g_lru": row(lru_out_g[l]), "g_attn": row(attn_out_g[l]),
        "w_out": w_out[l].astype(bf), "g_mlp": row(norm_mlp_g[l]),
        "w_up": w_up[l].astype(bf), "w_down": w_down[l].astype(bf),
    }


def _pad_buf(buf):
    return jnp.pad(buf, ((0, 0), (SUBLANES - (CONV_W - 1), 0), (0, 0)))


def _pad_axis(a, axis, size):
    pad = [(0, 0)] * a.ndim
    pad[axis] = (0, size - a.shape[axis])
    return jnp.pad(a, pad)


def kernel(x_prompt, x_sample, cache_ckv, cache_kpe, state_lru_h, state_conv, meta_tokens,
           norm_mix_g, w_in, q_norm_g, w_uq, kv_norm_g, w_ukv, conv_w, conv_b,
           w_gate_a, b_gate_a, w_gate_x, b_gate_x, lru_lambda, attn_out_g, lru_out_g,
           w_out, norm_mlp_g, w_up, w_down, final_norm_g):
    b_p, seq, _ = x_prompt.shape
    b_s, dec_seq, _ = x_sample.shape
    past_len = cache_ckv.shape[2]
    assert (past_len + dec_seq - 1) // CHUNK == past_len // CHUNK
    tm_p = 512
    tq_p = 512
    tm_o = 1024
    assert seq % tm_p == 0 and seq % tq_p == 0 and tq_p % CHUNK == 0

    meta_pos = jnp.arange(-N_META, 0, dtype=jnp.int32)
    prompt_pos = jnp.arange(seq, dtype=jnp.int32)
    sample_pos = past_len + jnp.arange(dec_seq, dtype=jnp.int32)
    tabk_m, tabk_p, tabk_s = _rope_tables(meta_pos), _rope_tables(prompt_pos), _rope_tables(sample_pos)
    tabq_m, tabq_p, tabq_s = _query_tables(tabk_m), _query_tables(tabk_p), _query_tables(tabk_s)
    g_fin = final_norm_g.reshape(1, -1).astype(jnp.float32)

    n_keys_s = N_META + past_len + dec_seq
    n_keys_s_pad = -(-n_keys_s // LANES) * LANES
    tq_small = LANES

    h_meta = meta_tokens[None].astype(jnp.float32)
    h_p, h_s = x_prompt, x_sample
    zero_h = jnp.zeros((1, 1, LRU_W), jnp.float32)
    zero_buf = jnp.zeros((1, SUBLANES, LRU_W), jnp.float32)
    outs = {name: [] for name in ("ckv_p", "kpe_p", "lru_p", "conv_p", "ckv_s", "kpe_s", "lru_s", "conv_s")}
    for l in range(DEPTH):
        lw = _layer_weights(l, norm_mix_g, w_in, q_norm_g, w_uq, kv_norm_g, w_ukv, conv_w, conv_b,
                            w_gate_a, b_gate_a, w_gate_x, b_gate_x, lru_lambda, attn_out_g,
                            lru_out_g, w_out, norm_mlp_g, w_up, w_down)
        last = l + 1 == DEPTH
        mq, mk, mvt, m_ckv, m_kpe, m_y, m_h, m_buf = _mixer_call(
            h_meta, zero_h, zero_buf, tabq_m, tabk_m, lw, tm=N_META, shared_state=True)
        pq, pk, pvt, p_ckv, p_kpe, p_y, p_h, p_buf = _mixer_call(
            h_p, m_h, _pad_buf(m_buf), tabq_p, tabk_p, lw, tm=tm_p, shared_state=True)
        mk_p, mvt_p = _pad_axis(mk, 2, LANES), _pad_axis(mvt, 2, LANES)
        p_o = _attn_call(pq, mk_p, mvt_p, pk, pvt, tq=tq_p, n_prefix_valid=N_META, prefix_shared=True)
        h_p = _out_mlp_call(p_o.reshape(b_p * seq, ATTN_W), p_y.reshape(b_p * seq, LRU_W),
                            h_p.reshape(b_p * seq, D_MODEL), lw, g_fin, tm=tm_o,
                            final=last).reshape(b_p, seq, D_MODEL)
        sq, sk, svt, s_ckv, s_kpe, s_y, s_h, s_buf = _mixer_call(
            h_s, state_lru_h[l][:, None, :], _pad_buf(state_conv[l]), tabq_s, tabk_s, lw,
            tm=dec_seq, shared_state=False)
        ck, cvt = _kvup_call(cache_ckv[l], _pad_axis(cache_kpe[l], 2, LANES), lw, tm=past_len)
        k_all = jnp.concatenate([jnp.broadcast_to(mk, (b_s,) + mk.shape[1:]), ck, sk], axis=2)
        vt_all = jnp.concatenate([jnp.broadcast_to(mvt, (b_s,) + mvt.shape[1:]), cvt, svt], axis=2)
        s_o = _attn_call(_pad_axis(sq, 2, tq_small), _pad_axis(k_all, 2, n_keys_s_pad),
                         _pad_axis(vt_all, 2, n_keys_s_pad), None, None, tq=tq_small,
                         n_prefix_valid=n_keys_s, prefix_shared=False)[:, :dec_seq]
        h_s = _out_mlp_call(s_o.reshape(b_s * dec_seq, ATTN_W), s_y.reshape(b_s * dec_seq, LRU_W),
                            h_s.reshape(b_s * dec_seq, D_MODEL), lw, g_fin, tm=b_s * dec_seq,
                            final=last).reshape(b_s, dec_seq, D_MODEL)
        if not last:
            m_o = _attn_call(_pad_axis(mq, 2, tq_small), mk_p, mvt_p, None, None, tq=tq_small,
                             n_prefix_valid=N_META, prefix_shared=False)[:, :N_META]
            h_meta = _out_mlp_call(m_o.reshape(N_META, ATTN_W), m_y.reshape(N_META, LRU_W),
                                   h_meta.reshape(N_META, D_MODEL), lw, g_fin, tm=N_META,
                                   final=False).reshape(1, N_META, D_MODEL)
        outs["ckv_p"].append(p_ckv); outs["kpe_p"].append(p_kpe)
        outs["lru_p"].append(p_h[:, 0]); outs["conv_p"].append(p_buf)
        outs["ckv_s"].append(s_ckv); outs["kpe_s"].append(s_kpe)
        outs["lru_s"].append(s_h[:, 0]); outs["conv_s"].append(s_buf)

    return (h_p, h_s, jnp.stack(outs["ckv_p"]), jnp.stack(outs["kpe_p"]), jnp.stack(outs["lru_p"]),
            jnp.stack(outs["conv_p"]), jnp.stack(outs["ckv_s"]), jnp.stack(outs["kpe_s"]),
            jnp.stack(outs["lru_s"]), jnp.stack(outs["conv_s"]))
```

```python
import functools

import jax
import jax.numpy as jnp
from jax import lax
from jax.experimental import pallas as pl
from jax.experimental.pallas import tpu as pltpu

D_MODEL = 1024
DEPTH = 2
CHUNK = 64
N_META = 16
ATTN_W = 512
LRU_W = 512
V_DIM = 64
N_HEADS = 8
QK_NOPE = 64
QK_ROPE = 32
Q_LORA = 768
KV_LORA = 256
LRU_BLOCKS = 8
LRU_BW = LRU_W // LRU_BLOCKS
LRU_C = 8.0
CONV_W = 4
D_FF = 4 * D_MODEL
ROPE_THETA = 10000.0
EPS = 1e-6
SM_SCALE = (QK_NOPE + QK_ROPE) ** -0.5
NEG_INF = -1e30
LOG2_E = 1.4426950408889634

LANES = 128
SUBLANES = 8
HEAD_PAD = LANES
VMEM_LIMIT_BYTES = 56 * 1024 * 1024

ROPE_PAD = LANES
COL_CQ = 0
COL_CKV = Q_LORA
COL_KR = Q_LORA + KV_LORA
COL_XR = COL_KR + ROPE_PAD
COL_GR = COL_XR + LRU_W
IN_W_PAD = COL_GR + LRU_W

HEADS_PER_STEP = 2
ONES_ROWS = 16
V_AUG = V_DIM + ONES_ROWS
F_CHUNK = 1024


def _rms(x, g):
    return x * lax.rsqrt(jnp.mean(x * x, axis=-1, keepdims=True) + EPS) * g


def _rope_lanes(x, tab_ref):
    half = QK_ROPE // 2
    return (x * tab_ref[0] + pltpu.roll(x, LANES - half, 1) * tab_ref[1]
            + pltpu.roll(x, half, 1) * tab_ref[2])


def _dot(a, b):
    return jnp.dot(a, b, preferred_element_type=jnp.float32)


def _dot_nt(a, b):
    return lax.dot_general(a, b, (((1,), (1,)), ((), ())), preferred_element_type=jnp.float32)


def _mixer_kernel(x_ref, h0_ref, buf0_ref, tabq_ref, tabk_ref, g_mix_ref, w_in_ref, g_q_ref,
                  w_uq_ref, g_kv_ref, w_uk_ref, w_uvt_ref, conv_w_ref, conv_b_ref, wg_ref, bg_ref,
                  lam_ref, g_lru_ref,
                  q_ref, k_ref, vt_ref, ckv_ref, kpe_ref, y_ref, hlast_ref, buf_ref,
                  tail_scr, hcar_scr, *, tm):
    t = pl.program_id(1)

    @pl.when(t == 0)
    def _():
        tail_scr[...] = buf0_ref[0]
        hcar_scr[...] = h0_ref[0]

    x = x_ref[0]
    xn = _rms(x, g_mix_ref[...]).astype(jnp.bfloat16)
    z = _dot(xn, w_in_ref[...])

    cqn = _rms(z[:, COL_CQ:COL_CQ + Q_LORA], g_q_ref[...]).astype(jnp.bfloat16)
    q = _dot(cqn, w_uq_ref[...])
    for h in range(N_HEADS):
        qh = q[:, h * HEAD_PAD:(h + 1) * HEAD_PAD]
        q_ref[0, h] = _rope_lanes(qh, tabq_ref).astype(jnp.bfloat16)

    ckv = _rms(z[:, COL_CKV:COL_CKV + KV_LORA], g_kv_ref[...])
    ckv_ref[0] = ckv
    ckv_b = ckv.astype(jnp.bfloat16)
    kpe = _rope_lanes(z[:, COL_KR:COL_KR + ROPE_PAD], tabk_ref)
    kpe_ref[0] = kpe[:, :QK_ROPE]
    kpe_shift = pltpu.roll(kpe, QK_NOPE, 1)
    kk = _dot(ckv_b, w_uk_ref[...])
    for h in range(N_HEADS):
        k_ref[0, h] = (kk[:, h * HEAD_PAD:(h + 1) * HEAD_PAD] + kpe_shift).astype(jnp.bfloat16)
    vt_ref[0] = _dot_nt(w_uvt_ref[...], ckv_b).astype(jnp.bfloat16)

    x_r = z[:, COL_XR:COL_XR + LRU_W]
    g_r = z[:, COL_GR:COL_GR + LRU_W]
    n_slabs = tm // SUBLANES
    rows8 = lax.broadcasted_iota(jnp.int32, (SUBLANES, LRU_W), 0)
    slabs = [tail_scr[...]] + [x_r[g * SUBLANES:(g + 1) * SUBLANES, :] for g in range(n_slabs)]
    xc = conv_b_ref[...] + x_r * conv_w_ref[CONV_W - 1:CONV_W, :]
    for back in range(1, CONV_W):
        rolled = [pltpu.roll(s, back, 0) for s in slabs]
        shifted = jnp.concatenate(
            [jnp.where(rows8 < back, rolled[g], rolled[g + 1]) for g in range(n_slabs)], axis=0)
        xc = xc + shifted * conv_w_ref[CONV_W - 1 - back:CONV_W - back, :]
    tail_scr[...] = slabs[-1]
    buf_ref[0] = tail_scr[SUBLANES - (CONV_W - 1):SUBLANES, :]

    half_w = LRU_W // 2
    gates = [_dot(xc[:, s * half_w:(s + 1) * half_w].astype(jnp.bfloat16), wg_ref[s]) for s in range(2)]
    pre_a = jnp.concatenate([gates[0][:, :half_w], gates[1][:, :half_w]], axis=1)
    pre_x = jnp.concatenate([gates[0][:, half_w:], gates[1][:, half_w:]], axis=1)
    r = jax.nn.sigmoid(pre_a + bg_ref[0:1, :])
    gi = jax.nn.sigmoid(pre_x + bg_ref[1:2, :])
    neg_lam = -lam_ref[...]
    softplus = jnp.maximum(neg_lam, 0.0) + jnp.log1p(jnp.exp(-jnp.abs(neg_lam)))
    log_a = (-LRU_C) * r * softplus
    a = jnp.exp(log_a)
    one_m_a2 = -jnp.tanh(log_a) * (a * a + 1.0)
    root = jnp.where(one_m_a2 > 0.0, one_m_a2 * lax.rsqrt(one_m_a2), 0.0)
    b = root * (gi * xc)

    carry = jnp.broadcast_to(hcar_scr[...], (SUBLANES, LRU_W))
    h_slabs = []
    for g in range(n_slabs):
        a_g = a[g * SUBLANES:(g + 1) * SUBLANES, :]
        b_g = b[g * SUBLANES:(g + 1) * SUBLANES, :]
        shift = 1
        while shift < SUBLANES:
            keep = rows8 >= shift
            a_prev = jnp.where(keep, pltpu.roll(a_g, shift, 0), 1.0)
            b_prev = jnp.where(keep, pltpu.roll(b_g, shift, 0), 0.0)
            b_g = a_g * b_prev + b_g
            a_g = a_g * a_prev
            shift *= 2
        h_g = a_g * carry + b_g
        h_slabs.append(h_g)
        carry = jnp.broadcast_to(h_g[SUBLANES - 1:SUBLANES, :], (SUBLANES, LRU_W))
    hh = jnp.concatenate(h_slabs, axis=0)
    h_last = carry[0:1, :]
    hcar_scr[...] = h_last
    hlast_ref[0] = h_last
    y = hh * jax.nn.gelu(g_r)
    y_ref[0] = _rms(y, g_lru_ref[...]).astype(jnp.bfloat16)


def _mixer_call(x, h0, buf0p, tabq, tabk, lw, *, tm, shared_state):
    bsz, seq, _ = x.shape
    nt = seq // tm
    const2 = lambda b, t: (0, 0)
    const3 = lambda b, t: (0, 0, 0)
    state_idx = (lambda b, t: (0, 0, 0)) if shared_state else (lambda b, t: (b, 0, 0))
    in_specs = [
        pl.BlockSpec((1, tm, D_MODEL), lambda b, t: (b, t, 0)),
        pl.BlockSpec((1, 1, LRU_W), state_idx),
        pl.BlockSpec((1, SUBLANES, LRU_W), state_idx),
        pl.BlockSpec((3, tm, LANES), lambda b, t: (0, t, 0)),
        pl.BlockSpec((3, tm, LANES), lambda b, t: (0, t, 0)),
        pl.BlockSpec((1, D_MODEL), const2),
        pl.BlockSpec((D_MODEL, IN_W_PAD), const2),
        pl.BlockSpec((1, Q_LORA), const2),
        pl.BlockSpec((Q_LORA, N_HEADS * HEAD_PAD), const2),
        pl.BlockSpec((1, KV_LORA), const2),
        pl.BlockSpec((KV_LORA, N_HEADS * HEAD_PAD), const2),
        pl.BlockSpec((ATTN_W, KV_LORA), const2),
        pl.BlockSpec((CONV_W, LRU_W), const2),
        pl.BlockSpec((1, LRU_W), const2),
        pl.BlockSpec((2, LRU_W // 2, LRU_W), const3),
        pl.BlockSpec((2, LRU_W), const2),
        pl.BlockSpec((1, LRU_W), const2),
        pl.BlockSpec((1, LRU_W), const2),
    ]
    out_shape = [
        jax.ShapeDtypeStruct((bsz, N_HEADS, seq, HEAD_PAD), jnp.bfloat16),
        jax.ShapeDtypeStruct((bsz, N_HEADS, seq, HEAD_PAD), jnp.bfloat16),
        jax.ShapeDtypeStruct((bsz, ATTN_W, seq), jnp.bfloat16),
        jax.ShapeDtypeStruct((bsz, seq, KV_LORA), jnp.float32),
        jax.ShapeDtypeStruct((bsz, seq, QK_ROPE), jnp.float32),
        jax.ShapeDtypeStruct((bsz, seq, LRU_W), jnp.bfloat16),
        jax.ShapeDtypeStruct((bsz, 1, LRU_W), jnp.float32),
        jax.ShapeDtypeStruct((bsz, CONV_W - 1, LRU_W), jnp.float32),
    ]
    out_specs = [
        pl.BlockSpec((1, N_HEADS, tm, HEAD_PAD), lambda b, t: (b, 0, t, 0)),
        pl.BlockSpec((1, N_HEADS, tm, HEAD_PAD), lambda b, t: (b, 0, t, 0)),
        pl.BlockSpec((1, ATTN_W, tm), lambda b, t: (b, 0, t)),
        pl.BlockSpec((1, tm, KV_LORA), lambda b, t: (b, t, 0)),
        pl.BlockSpec((1, tm, QK_ROPE), lambda b, t: (b, t, 0)),
        pl.BlockSpec((1, tm, LRU_W), lambda b, t: (b, t, 0)),
        pl.BlockSpec((1, 1, LRU_W), lambda b, t: (b, 0, 0)),
        pl.BlockSpec((1, CONV_W - 1, LRU_W), lambda b, t: (b, 0, 0)),
    ]
    return pl.pallas_call(
        functools.partial(_mixer_kernel, tm=tm),
        grid=(bsz, nt),
        in_specs=in_specs,
        out_specs=out_specs,
        out_shape=out_shape,
        scratch_shapes=[pltpu.VMEM((SUBLANES, LRU_W), jnp.float32),
                        pltpu.VMEM((1, LRU_W), jnp.float32)],
        compiler_params=pltpu.CompilerParams(
            dimension_semantics=("arbitrary", "arbitrary"), vmem_limit_bytes=VMEM_LIMIT_BYTES),
        name="mixer",
    )(x, h0, buf0p, tabq, tabk, lw["g_mix"], lw["w_in"], lw["g_q"], lw["w_uq"], lw["g_kv"],
      lw["w_uk"], lw["w_uvt"], lw["conv_w"], lw["conv_b"], lw["wg"], lw["bg"], lw["lam"], lw["g_lru"])


def _kvup_kernel(ckv_ref, kpe_ref, w_uk_ref, w_uvt_ref, k_ref, vt_ref):
    ckv_b = ckv_ref[0].astype(jnp.bfloat16)
    kpe_shift = pltpu.roll(kpe_ref[0], QK_NOPE, 1)
    kk = _dot(ckv_b, w_uk_ref[...])
    for h in range(N_HEADS):
        k_ref[0, h] = (kk[:, h * HEAD_PAD:(h + 1) * HEAD_PAD] + kpe_shift).astype(jnp.bfloat16)
    vt_ref[0] = _dot_nt(w_uvt_ref[...], ckv_b).astype(jnp.bfloat16)


def _kvup_call(ckv, kpe_pad, lw, *, tm):
    bsz, seq, _ = ckv.shape
    return pl.pallas_call(
        _kvup_kernel,
        grid=(bsz, seq // tm),
        in_specs=[
            pl.BlockSpec((1, tm, KV_LORA), lambda b, t: (b, t, 0)),
            pl.BlockSpec((1, tm, LANES), lambda b, t: (b, t, 0)),
            pl.BlockSpec((KV_LORA, N_HEADS * HEAD_PAD), lambda b, t: (0, 0)),
            pl.BlockSpec((ATTN_W, KV_LORA), lambda b, t: (0, 0)),
        ],
        out_specs=[
            pl.BlockSpec((1, N_HEADS, tm, HEAD_PAD), lambda b, t: (b, 0, t, 0)),
            pl.BlockSpec((1, ATTN_W, tm), lambda b, t: (b, 0, t)),
        ],
        out_shape=[
            jax.ShapeDtypeStruct((bsz, N_HEADS, seq, HEAD_PAD), jnp.bfloat16),
            jax.ShapeDtypeStruct((bsz, ATTN_W, seq), jnp.bfloat16),
        ],
        compiler_params=pltpu.CompilerParams(
            dimension_semantics=("arbitrary", "arbitrary"), vmem_limit_bytes=VMEM_LIMIT_BYTES),
        name="kv_up",
    )(ckv, kpe_pad, lw["w_uk"], lw["w_uvt"])


def _col_max(st):
    rows, cols = st.shape
    slab = SUBLANES
    if rows % slab == 0 and rows > slab:
        st = jnp.max(st.reshape(rows // slab, slab, cols), axis=0)
    return jnp.max(st, axis=0, keepdims=True)


def _prefix_state(q_ref, kp_ref, vtp_ref, n_prefix, n_prefix_valid, tq):
    ones_p = jnp.ones((ONES_ROWS, n_prefix), jnp.bfloat16)
    ms, accs = [], []
    for hh in range(q_ref.shape[1]):
        st = _dot_nt(kp_ref[0, hh], q_ref[0, hh])
        if n_prefix_valid < n_prefix:
            key_idx = lax.broadcasted_iota(jnp.int32, (n_prefix, tq), 0)
            st = jnp.where(key_idx < n_prefix_valid, st, NEG_INF)
        vt_aug = jnp.concatenate([vtp_ref[0, hh * V_DIM:(hh + 1) * V_DIM, :], ones_p], axis=0)
        m = _col_max(st)
        ms.append(m)
        accs.append(_dot(vt_aug, jnp.exp2(st - m).astype(jnp.bfloat16)))
    return ms, accs


def _store_normalised(accs, o_ref):
    outs = [acc[:V_DIM, :] / acc[V_DIM:V_DIM + 1, :] for acc in accs]
    o_ref[0] = jnp.concatenate(outs, axis=0).T.astype(jnp.bfloat16)


def _attn_dense_kernel(q_ref, kp_ref, vtp_ref, o_ref, *, tq, n_prefix, n_prefix_valid):
    _, accs = _prefix_state(q_ref, kp_ref, vtp_ref, n_prefix, n_prefix_valid, tq)
    _store_normalised(accs, o_ref)


def _attn_causal_kernel(q_ref, kp_ref, vtp_ref, k_ref, vt_ref, o_ref,
                        sa_scr, sb_scr, ma_scr, mb_scr, m_scr, acc_scr,
                        *, tq, tk, n_prefix, n_prefix_valid):
    qi = pl.program_id(2)
    ms, accs = _prefix_state(q_ref, kp_ref, vtp_ref, n_prefix, n_prefix_valid, tq)
    for hh in range(HEADS_PER_STEP):
        m_scr[hh] = ms[hh]
        acc_scr[hh] = accs[hh]
    ones_k = jnp.ones((ONES_ROWS, tk), jnp.bfloat16)

    def scores(tile, s_ref, mx_ref, q0=0, qn=tq):
        start = pl.multiple_of(tile * tk, tk)
        for hh in range(HEADS_PER_STEP):
            st = _dot_nt(k_ref[0, hh, pl.ds(start, tk), :], q_ref[0, hh, q0:q0 + qn, :])
            s_ref[hh, :, q0:q0 + qn] = st
            mx_ref[hh, :, q0:q0 + qn] = _col_max(st)

    def softmax_pv(tile, s_ref, mx_ref, mask=None, q0=0, qn=tq):
        start = pl.multiple_of(tile * tk, tk)
        for hh in range(HEADS_PER_STEP):
            st = s_ref[hh, :, q0:q0 + qn]
            if mask is None:
                m_blk = mx_ref[hh, :, q0:q0 + qn]
            else:
                st = jnp.where(mask, st, NEG_INF)
                m_blk = _col_max(st)
            m_old = m_scr[hh, :, q0:q0 + qn]
            m_new = jnp.maximum(m_old, m_blk)
            p = jnp.exp2(st - m_new).astype(jnp.bfloat16)
            vt_t = vt_ref[0, hh * V_DIM:(hh + 1) * V_DIM, pl.ds(start, tk)]
            vt_aug = jnp.concatenate([vt_t, ones_k], axis=0)
            acc_scr[hh, :, q0:q0 + qn] = (jnp.exp2(m_old - m_new) * acc_scr[hh, :, q0:q0 + qn]
                                          + _dot(vt_aug, p))
            m_scr[hh, :, q0:q0 + qn] = m_new

    scores(0, sa_scr, ma_scr)

    def body(jj, carry):
        t0 = 2 * jj
        scores(t0 + 1, sb_scr, mb_scr)
        softmax_pv(t0, sa_scr, ma_scr)
        scores(t0 + 2, sa_scr, ma_scr)
        softmax_pv(t0 + 1, sb_scr, mb_scr)
        return carry

    lax.fori_loop(0, qi, body, 0)

    d1 = 2 * qi
    key_chunk = lax.broadcasted_iota(jnp.int32, (tk, tq), 0) // CHUNK
    qry_chunk = lax.broadcasted_iota(jnp.int32, (tk, tq), 1) // CHUNK
    scores(d1 + 1, sb_scr, mb_scr, q0=tk, qn=tk)
    softmax_pv(d1, sa_scr, ma_scr, mask=key_chunk <= qry_chunk)
    softmax_pv(d1 + 1, sb_scr, mb_scr, mask=(key_chunk <= qry_chunk)[:, :tk], q0=tk, qn=tk)
    _store_normalised([acc_scr[hh] for hh in range(HEADS_PER_STEP)], o_ref)


def _attn_dense_call(q, kp, vtp, *, n_prefix_valid):
    bsz, _, tq, _ = q.shape
    n_prefix = kp.shape[2]
    return pl.pallas_call(
        functools.partial(_attn_dense_kernel, tq=tq, n_prefix=n_prefix, n_prefix_valid=n_prefix_valid),
        grid=(bsz,),
        in_specs=[
            pl.BlockSpec((1, N_HEADS, tq, HEAD_PAD), lambda b: (b, 0, 0, 0)),
            pl.BlockSpec((1, N_HEADS, n_prefix, HEAD_PAD), lambda b: (b, 0, 0, 0)),
            pl.BlockSpec((1, ATTN_W, n_prefix), lambda b: (b, 0, 0)),
        ],
        out_specs=pl.BlockSpec((1, tq, ATTN_W), lambda b: (b, 0, 0)),
        out_shape=jax.ShapeDtypeStruct((bsz, tq, ATTN_W), jnp.bfloat16),
        compiler_params=pltpu.CompilerParams(
            dimension_semantics=("arbitrary",), vmem_limit_bytes=VMEM_LIMIT_BYTES),
        name="attn_dense",
    )(q, kp, vtp)


def _attn_causal_call(q, kp, vtp, k, vt, *, tq, n_prefix_valid):
    bsz, _, seq, _ = q.shape
    n_prefix = kp.shape[2]
    tk = tq // 2
    f32 = jnp.float32
    return pl.pallas_call(
        functools.partial(_attn_causal_kernel, tq=tq, tk=tk, n_prefix=n_prefix,
                          n_prefix_valid=n_prefix_valid),
        grid=(bsz, N_HEADS // HEADS_PER_STEP, seq // tq),
        in_specs=[
            pl.BlockSpec((1, HEADS_PER_STEP, tq, HEAD_PAD), lambda b, hp, i: (b, hp, i, 0)),
            pl.BlockSpec((1, HEADS_PER_STEP, n_prefix, HEAD_PAD), lambda b, hp, i: (0, hp, 0, 0)),
            pl.BlockSpec((1, HEADS_PER_STEP * V_DIM, n_prefix), lambda b, hp, i: (0, hp, 0)),
            pl.BlockSpec((1, HEADS_PER_STEP, seq, HEAD_PAD), lambda b, hp, i: (b, hp, 0, 0)),
            pl.BlockSpec((1, HEADS_PER_STEP * V_DIM, seq), lambda b, hp, i: (b, hp, 0)),
        ],
        out_specs=pl.BlockSpec((1, tq, HEADS_PER_STEP * V_DIM), lambda b, hp, i: (b, i, hp)),
        out_shape=jax.ShapeDtypeStruct((bsz, seq, ATTN_W), jnp.bfloat16),
        scratch_shapes=[pltpu.VMEM((HEADS_PER_STEP, tk, tq), f32),
                        pltpu.VMEM((HEADS_PER_STEP, tk, tq), f32),
                        pltpu.VMEM((HEADS_PER_STEP, 1, tq), f32),
                        pltpu.VMEM((HEADS_PER_STEP, 1, tq), f32),
                        pltpu.VMEM((HEADS_PER_STEP, 1, tq), f32),
                        pltpu.VMEM((HEADS_PER_STEP, V_AUG, tq), f32)],
        compiler_params=pltpu.CompilerParams(
            dimension_semantics=("arbitrary", "arbitrary", "arbitrary"),
            vmem_limit_bytes=VMEM_LIMIT_BYTES),
        name="attn_causal",
    )(q, kp, vtp, k, vt)


def _out_mlp_kernel(o_ref, y_ref, x_ref, g_attn_ref, w_out_ref, g_mlp_ref, w_up_ref, w_down_ref,
                    g_fin_ref, out_ref, hn_scr, *, final):
    c = pl.program_id(1)

    @pl.when(c == 0)
    def _():
        on = _rms(o_ref[...].astype(jnp.float32), g_attn_ref[...]).astype(jnp.bfloat16)
        h = (x_ref[...] + _dot(on, w_out_ref[0:ATTN_W, :])
             + _dot(y_ref[...], w_out_ref[ATTN_W:ATTN_W + LRU_W, :]))
        hn_scr[...] = _rms(h, g_mlp_ref[...]).astype(jnp.bfloat16)
        out_ref[...] = h

    u = _dot(hn_scr[...], w_up_ref[...])
    u = jnp.square(jnp.maximum(u, 0.0)).astype(jnp.bfloat16)
    out_ref[...] += _dot(u, w_down_ref[...])

    if final:
        @pl.when(c == pl.num_programs(1) - 1)
        def _():
            out_ref[...] = _rms(out_ref[...], g_fin_ref[...])


def _out_mlp_call(o, y, x, lw, g_fin, *, tm, final):
    rows = x.shape[0]
    const = lambda r, c: (0, 0)
    return pl.pallas_call(
        functools.partial(_out_mlp_kernel, final=final),
        grid=(rows // tm, D_FF // F_CHUNK),
        in_specs=[
            pl.BlockSpec((tm, ATTN_W), lambda r, c: (r, 0)),
            pl.BlockSpec((tm, LRU_W), lambda r, c: (r, 0)),
            pl.BlockSpec((tm, D_MODEL), lambda r, c: (r, 0)),
            pl.BlockSpec((1, ATTN_W), const),
            pl.BlockSpec((D_MODEL, D_MODEL), const),
            pl.BlockSpec((1, D_MODEL), const),
            pl.BlockSpec((D_MODEL, F_CHUNK), lambda r, c: (0, c)),
            pl.BlockSpec((F_CHUNK, D_MODEL), lambda r, c: (c, 0)),
            pl.BlockSpec((1, D_MODEL), const),
        ],
        out_specs=pl.BlockSpec((tm, D_MODEL), lambda r, c: (r, 0)),
        out_shape=jax.ShapeDtypeStruct((rows, D_MODEL), jnp.float32),
        scratch_shapes=[pltpu.VMEM((tm, D_MODEL), jnp.bfloat16)],
        compiler_params=pltpu.CompilerParams(
            dimension_semantics=("arbitrary", "arbitrary"), vmem_limit_bytes=VMEM_LIMIT_BYTES),
        name="out_mlp",
    )(o, y, x, lw["g_attn"], lw["w_out"], lw["g_mlp"], lw["w_up"], lw["w_down"], g_fin)


def _rope_tables(pos):
    inv = ROPE_THETA ** (-jnp.arange(0, QK_ROPE, 2, dtype=jnp.float32) / QK_ROPE)
    ang = pos.astype(jnp.float32)[:, None] * inv[None, :]
    cos, sin = jnp.cos(ang), jnp.sin(ang)
    zeros = jnp.zeros((pos.shape[0], LANES - QK_ROPE), jnp.float32)
    z16 = jnp.zeros_like(sin)
    tab_c = jnp.concatenate([cos, cos, zeros], axis=1)
    tab_m = jnp.concatenate([-sin, z16, zeros], axis=1)
    tab_p = jnp.concatenate([z16, sin, zeros], axis=1)
    return jnp.stack([tab_c, tab_m, tab_p])


def _query_tables(tabk):
    tq = jnp.roll(tabk, QK_NOPE, axis=2)
    tq = tq.at[0, :, :QK_NOPE].set(1.0)
    return tq * (SM_SCALE * LOG2_E)


def _block_diag_gates(w_a, w_x):
    per_half = LRU_BLOCKS // 2
    halves = []
    for s in range(2):
        bd_a = jax.scipy.linalg.block_diag(*[w_a[s * per_half + i] for i in range(per_half)])
        bd_x = jax.scipy.linalg.block_diag(*[w_x[s * per_half + i] for i in range(per_half)])
        halves.append(jnp.concatenate([bd_a, bd_x], axis=1))
    return jnp.stack(halves).astype(jnp.bfloat16)


def _layer_weights(l, norm_mix_g, w_in, q_norm_g, w_uq, kv_norm_g, w_ukv, conv_w, conv_b,
                   w_gate_a, b_gate_a, w_gate_x, b_gate_x, lru_lambda, attn_out_g, lru_out_g,
                   w_out, norm_mlp_g, w_up, w_down):
    bf = jnp.bfloat16
    row = lambda v: v.reshape(1, -1).astype(jnp.float32)
    w_in_p = jnp.concatenate(
        [w_in[l][:, :COL_KR + QK_ROPE], jnp.zeros((D_MODEL, ROPE_PAD - QK_ROPE), w_in.dtype),
         w_in[l][:, COL_KR + QK_ROPE:]], axis=1).astype(bf)
    w_uq_p = jnp.pad(w_uq[l].reshape(Q_LORA, N_HEADS, QK_NOPE + QK_ROPE),
                     ((0, 0), (0, 0), (0, HEAD_PAD - QK_NOPE - QK_ROPE))).reshape(Q_LORA, -1).astype(bf)
    w_ukv_h = w_ukv[l].reshape(KV_LORA, N_HEADS, QK_NOPE + V_DIM)
    w_uk_p = jnp.pad(w_ukv_h[:, :, :QK_NOPE],
                     ((0, 0), (0, 0), (0, HEAD_PAD - QK_NOPE))).reshape(KV_LORA, -1).astype(bf)
    w_uvt = w_ukv_h[:, :, QK_NOPE:].reshape(KV_LORA, ATTN_W).T.astype(bf)
    return {
        "g_mix": row(norm_mix_g[l]), "w_in": w_in_p, "g_q": row(q_norm_g[l]), "w_uq": w_uq_p,
        "g_kv": row(kv_norm_g[l]), "w_uk": w_uk_p, "w_uvt": w_uvt,
        "conv_w": conv_w[l].astype(jnp.float32), "conv_b": row(conv_b[l]),
        "wg": _block_diag_gates(w_gate_a[l], w_gate_x[l]),
        "bg": jnp.stack([b_gate_a[l], b_gate_x[l]]).astype(jnp.float32),
        "lam": row(lru_lambda[l]), "g_lru": row(lru_out_g[l]), "g_attn": row(attn_out_g[l]),
        "w_out": w_out[l].astype(bf), "g_mlp": row(norm_mlp_g[l]),
        "w_up": w_up[l].astype(bf), "w_down": w_down[l].astype(bf),
    }


def _pad_buf(buf):
    return jnp.pad(buf, ((0, 0), (SUBLANES - (CONV_W - 1), 0), (0, 0)))


def _pad_axis(a, axis, size):
    pad = [(0, 0)] * a.ndim
    pad[axis] = (0, size - a.shape[axis])
    return jnp.pad(a, pad)


def kernel(x_prompt, x_sample, cache_ckv, cache_kpe, state_lru_h, state_conv, meta_tokens,
           norm_mix_g, w_in, q_norm_g, w_uq, kv_norm_g, w_ukv, conv_w, conv_b,
           w_gate_a, b_gate_a, w_gate_x, b_gate_x, lru_lambda, attn_out_g, lru_out_g,
           w_out, norm_mlp_g, w_up, w_down, final_norm_g):
    b_p, seq, _ = x_prompt.shape
    b_s, dec_seq, _ = x_sample.shape
    past_len = cache_ckv.shape[2]
    assert (past_len + dec_seq - 1) // CHUNK == past_len // CHUNK
    tm_p = 512
    tq_p = 1024
    tm_o = 1024
    assert seq % tm_p == 0 and seq % tq_p == 0 and (tq_p // 2) % CHUNK == 0

    meta_pos = jnp.arange(-N_META, 0, dtype=jnp.int32)
    prompt_pos = jnp.arange(seq, dtype=jnp.int32)
    sample_pos = past_len + jnp.arange(dec_seq, dtype=jnp.int32)
    tabk_m, tabk_p, tabk_s = _rope_tables(meta_pos), _rope_tables(prompt_pos), _rope_tables(sample_pos)
    tabq_m, tabq_p, tabq_s = _query_tables(tabk_m), _query_tables(tabk_p), _query_tables(tabk_s)
    g_fin = final_norm_g.reshape(1, -1).astype(jnp.float32)

    n_keys_s = N_META + past_len + dec_seq
    n_keys_s_pad = -(-n_keys_s // LANES) * LANES
    tq_small = LANES

    h_meta = meta_tokens[None].astype(jnp.float32)
    h_p, h_s = x_prompt, x_sample
    zero_h = jnp.zeros((1, 1, LRU_W), jnp.float32)
    zero_buf = jnp.zeros((1, SUBLANES, LRU_W), jnp.float32)
    outs = {name: [] for name in ("ckv_p", "kpe_p", "lru_p", "conv_p", "ckv_s", "kpe_s", "lru_s", "conv_s")}
    for l in range(DEPTH):
        lw = _layer_weights(l, norm_mix_g, w_in, q_norm_g, w_uq, kv_norm_g, w_ukv, conv_w, conv_b,
                            w_gate_a, b_gate_a, w_gate_x, b_gate_x, lru_lambda, attn_out_g,
                            lru_out_g, w_out, norm_mlp_g, w_up, w_down)
        last = l + 1 == DEPTH
        mq, mk, mvt, m_ckv, m_kpe, m_y, m_h, m_buf = _mixer_call(
            h_meta, zero_h, zero_buf, tabq_m, tabk_m, lw, tm=N_META, shared_state=True)
        pq, pk, pvt, p_ckv, p_kpe, p_y, p_h, p_buf = _mixer_call(
            h_p, m_h, _pad_buf(m_buf), tabq_p, tabk_p, lw, tm=tm_p, shared_state=True)
        mk_p, mvt_p = _pad_axis(mk, 2, LANES), _pad_axis(mvt, 2, LANES)
        p_o = _attn_causal_call(pq, mk_p, mvt_p, pk, pvt, tq=tq_p, n_prefix_valid=N_META)
        h_p = _out_mlp_call(p_o.reshape(b_p * seq, ATTN_W), p_y.reshape(b_p * seq, LRU_W),
                            h_p.reshape(b_p * seq, D_MODEL), lw, g_fin, tm=tm_o,
                            final=last).reshape(b_p, seq, D_MODEL)
        sq, sk, svt, s_ckv, s_kpe, s_y, s_h, s_buf = _mixer_call(
            h_s, state_lru_h[l][:, None, :], _pad_buf(state_conv[l]), tabq_s, tabk_s, lw,
            tm=dec_seq, shared_state=False)
        ck, cvt = _kvup_call(cache_ckv[l], _pad_axis(cache_kpe[l], 2, LANES), lw, tm=past_len)
        k_all = jnp.concatenate([jnp.broadcast_to(mk, (b_s,) + mk.shape[1:]), ck, sk], axis=2)
        vt_all = jnp.concatenate([jnp.broadcast_to(mvt, (b_s,) + mvt.shape[1:]), cvt, svt], axis=2)
        s_o = _attn_dense_call(_pad_axis(sq, 2, tq_small), _pad_axis(k_all, 2, n_keys_s_pad),
                               _pad_axis(vt_all, 2, n_keys_s_pad), n_prefix_valid=n_keys_s)[:, :dec_seq]
        h_s = _out_mlp_call(s_o.reshape(b_s * dec_seq, ATTN_W), s_y.reshape(b_s * dec_seq, LRU_W),
                            h_s.reshape(b_s * dec_seq, D_MODEL), lw, g_fin, tm=b_s * dec_seq,
                            final=last).reshape(b_s, dec_seq, D_MODEL)
        if not last:
            m_o = _attn_dense_call(_pad_axis(mq, 2, tq_small), mk_p, mvt_p,
                                   n_prefix_valid=N_META)[:, :N_META]
            h_meta = _out_mlp_call(m_o.reshape(N_META, ATTN_W), m_y.reshape(N_META, LRU_W),
                                   h_meta.reshape(N_META, D_MODEL), lw, g_fin, tm=N_META,
                                   final=False).reshape(1, N_META, D_MODEL)
        outs["ckv_p"].append(p_ckv); outs["kpe_p"].append(p_kpe)
        outs["lru_p"].append(p_h[:, 0]); outs["conv_p"].append(p_buf)
        outs["ckv_s"].append(s_ckv); outs["kpe_s"].append(s_kpe)
        outs["lru_s"].append(s_h[:, 0]); outs["conv_s"].append(s_buf)

    return (h_p, h_s, jnp.stack(outs["ckv_p"]), jnp.stack(outs["kpe_p"]), jnp.stack(outs["lru_p"]),
            jnp.stack(outs["conv_p"]), jnp.stack(outs["ckv_s"]), jnp.stack(outs["kpe_s"]),
            jnp.stack(outs["lru_s"]), jnp.stack(outs["conv_s"]))
```

```python
import functools

import jax
import jax.numpy as jnp
from jax import lax
from jax.experimental import pallas as pl
from jax.experimental.pallas import tpu as pltpu

D_MODEL = 1024
DEPTH = 2
CHUNK = 64
N_META = 16
ATTN_W = 512
LRU_W = 512
V_DIM = 64
N_HEADS = 8
QK_NOPE = 64
QK_ROPE = 32
Q_LORA = 768
KV_LORA = 256
LRU_BLOCKS = 8
LRU_BW = LRU_W // LRU_BLOCKS
LRU_C = 8.0
CONV_W = 4
D_FF = 4 * D_MODEL
ROPE_THETA = 10000.0
EPS = 1e-6
SM_SCALE = (QK_NOPE + QK_ROPE) ** -0.5
NEG_INF = -1e30
LOG2_E = 1.4426950408889634

LANES = 128
SUBLANES = 8
HEAD_PAD = LANES
VMEM_LIMIT_BYTES = 56 * 1024 * 1024

ROPE_PAD = LANES
COL_CQ = 0
COL_CKV = Q_LORA
COL_KR = Q_LORA + KV_LORA
COL_XR = COL_KR + ROPE_PAD
COL_GR = COL_XR + LRU_W
IN_W_PAD = COL_GR + LRU_W

HEADS_PER_STEP = 2
ONES_ROWS = 16
V_AUG = V_DIM + ONES_ROWS
F_CHUNK = 1024


def _rms(x, g):
    return x * lax.rsqrt(jnp.mean(x * x, axis=-1, keepdims=True) + EPS) * g


def _rope_lanes(x, tab_ref):
    half = QK_ROPE // 2
    return (x * tab_ref[0] + pltpu.roll(x, LANES - half, 1) * tab_ref[1]
            + pltpu.roll(x, half, 1) * tab_ref[2])


def _dot(a, b):
    return jnp.dot(a, b, preferred_element_type=jnp.float32)


def _dot_nt(a, b):
    return lax.dot_general(a, b, (((1,), (1,)), ((), ())), preferred_element_type=jnp.float32)


def _mixer_kernel(x_ref, h0_ref, buf0_ref, tabq_ref, tabk_ref, g_mix_ref, w_in_ref, g_q_ref,
                  w_uq_ref, g_kv_ref, w_uk_ref, w_uvt_ref, conv_w_ref, conv_b_ref, wg_ref, bg_ref,
                  lam_ref, g_lru_ref,
                  q_ref, k_ref, vt_ref, ckv_ref, kpe_ref, y_ref, hlast_ref, buf_ref,
                  tail_scr, hcar_scr, *, tm):
    t = pl.program_id(1)

    @pl.when(t == 0)
    def _():
        tail_scr[...] = buf0_ref[0]
        hcar_scr[...] = h0_ref[0]

    x = x_ref[0]
    xn = _rms(x, g_mix_ref[...]).astype(jnp.bfloat16)
    z = _dot(xn, w_in_ref[...])

    cqn = _rms(z[:, COL_CQ:COL_CQ + Q_LORA], g_q_ref[...]).astype(jnp.bfloat16)
    q = _dot(cqn, w_uq_ref[...])
    for h in range(N_HEADS):
        qh = q[:, h * HEAD_PAD:(h + 1) * HEAD_PAD]
        q_ref[0, h] = _rope_lanes(qh, tabq_ref).astype(jnp.bfloat16)

    ckv = _rms(z[:, COL_CKV:COL_CKV + KV_LORA], g_kv_ref[...])
    ckv_ref[0] = ckv
    ckv_b = ckv.astype(jnp.bfloat16)
    kpe = _rope_lanes(z[:, COL_KR:COL_KR + ROPE_PAD], tabk_ref)
    kpe_ref[0] = kpe
    kpe_shift = pltpu.roll(kpe, QK_NOPE, 1)
    kk = _dot(ckv_b, w_uk_ref[...])
    for h in range(N_HEADS):
        k_ref[0, h] = (kk[:, h * HEAD_PAD:(h + 1) * HEAD_PAD] + kpe_shift).astype(jnp.bfloat16)
    vt_ref[0] = _dot_nt(w_uvt_ref[...], ckv_b).astype(jnp.bfloat16)

    x_r = z[:, COL_XR:COL_XR + LRU_W]
    g_r = z[:, COL_GR:COL_GR + LRU_W]
    n_slabs = tm // SUBLANES
    rows8 = lax.broadcasted_iota(jnp.int32, (SUBLANES, LRU_W), 0)
    slabs = [tail_scr[...]] + [x_r[g * SUBLANES:(g + 1) * SUBLANES, :] for g in range(n_slabs)]
    xc = conv_b_ref[...] + x_r * conv_w_ref[CONV_W - 1:CONV_W, :]
    for back in range(1, CONV_W):
        rolled = [pltpu.roll(s, back, 0) for s in slabs]
        shifted = jnp.concatenate(
            [jnp.where(rows8 < back, rolled[g], rolled[g + 1]) for g in range(n_slabs)], axis=0)
        xc = xc + shifted * conv_w_ref[CONV_W - 1 - back:CONV_W - back, :]
    tail_scr[...] = slabs[-1]
    buf_ref[0] = tail_scr[SUBLANES - (CONV_W - 1):SUBLANES, :]

    half_w = LRU_W // 2
    gates = [_dot(xc[:, s * half_w:(s + 1) * half_w].astype(jnp.bfloat16), wg_ref[s]) for s in range(2)]
    pre_a = jnp.concatenate([gates[0][:, :half_w], gates[1][:, :half_w]], axis=1)
    pre_x = jnp.concatenate([gates[0][:, half_w:], gates[1][:, half_w:]], axis=1)
    r = jax.nn.sigmoid(pre_a + bg_ref[0:1, :])
    gi = jax.nn.sigmoid(pre_x + bg_ref[1:2, :])
    neg_lam = -lam_ref[...]
    softplus = jnp.maximum(neg_lam, 0.0) + jnp.log1p(jnp.exp(-jnp.abs(neg_lam)))
    log_a = (-LRU_C) * r * softplus
    a = jnp.exp(log_a)
    one_m_a2 = -jnp.tanh(log_a) * (a * a + 1.0)
    root = jnp.where(one_m_a2 > 0.0, one_m_a2 * lax.rsqrt(one_m_a2), 0.0)
    b = root * (gi * xc)

    carry = jnp.broadcast_to(hcar_scr[...], (SUBLANES, LRU_W))
    h_slabs = []
    for g in range(n_slabs):
        a_g = a[g * SUBLANES:(g + 1) * SUBLANES, :]
        b_g = b[g * SUBLANES:(g + 1) * SUBLANES, :]
        shift = 1
        while shift < SUBLANES:
            keep = rows8 >= shift
            a_prev = jnp.where(keep, pltpu.roll(a_g, shift, 0), 1.0)
            b_prev = jnp.where(keep, pltpu.roll(b_g, shift, 0), 0.0)
            b_g = a_g * b_prev + b_g
            a_g = a_g * a_prev
            shift *= 2
        h_g = a_g * carry + b_g
        h_slabs.append(h_g)
        carry = jnp.broadcast_to(h_g[SUBLANES - 1:SUBLANES, :], (SUBLANES, LRU_W))
    hh = jnp.concatenate(h_slabs, axis=0)
    h_last = carry[0:1, :]
    hcar_scr[...] = h_last
    hlast_ref[0] = h_last
    y = hh * jax.nn.gelu(g_r)
    y_ref[0] = _rms(y, g_lru_ref[...]).astype(jnp.bfloat16)


def _mixer_call(x, h0, buf0p, tabq, tabk, lw, *, tm, shared_state):
    bsz, seq, _ = x.shape
    nt = seq // tm
    const2 = lambda b, t: (0, 0)
    const3 = lambda b, t: (0, 0, 0)
    state_idx = (lambda b, t: (0, 0, 0)) if shared_state else (lambda b, t: (b, 0, 0))
    in_specs = [
        pl.BlockSpec((1, tm, D_MODEL), lambda b, t: (b, t, 0)),
        pl.BlockSpec((1, 1, LRU_W), state_idx),
        pl.BlockSpec((1, SUBLANES, LRU_W), state_idx),
        pl.BlockSpec((3, tm, LANES), lambda b, t: (0, t, 0)),
        pl.BlockSpec((3, tm, LANES), lambda b, t: (0, t, 0)),
        pl.BlockSpec((1, D_MODEL), const2),
        pl.BlockSpec((D_MODEL, IN_W_PAD), const2),
        pl.BlockSpec((1, Q_LORA), const2),
        pl.BlockSpec((Q_LORA, N_HEADS * HEAD_PAD), const2),
        pl.BlockSpec((1, KV_LORA), const2),
        pl.BlockSpec((KV_LORA, N_HEADS * HEAD_PAD), const2),
        pl.BlockSpec((ATTN_W, KV_LORA), const2),
        pl.BlockSpec((CONV_W, LRU_W), const2),
        pl.BlockSpec((1, LRU_W), const2),
        pl.BlockSpec((2, LRU_W // 2, LRU_W), const3),
        pl.BlockSpec((2, LRU_W), const2),
        pl.BlockSpec((1, LRU_W), const2),
        pl.BlockSpec((1, LRU_W), const2),
    ]
    out_shape = [
        jax.ShapeDtypeStruct((bsz, N_HEADS, seq, HEAD_PAD), jnp.bfloat16),
        jax.ShapeDtypeStruct((bsz, N_HEADS, seq, HEAD_PAD), jnp.bfloat16),
        jax.ShapeDtypeStruct((bsz, ATTN_W, seq), jnp.bfloat16),
        jax.ShapeDtypeStruct((bsz, seq, KV_LORA), jnp.float32),
        jax.ShapeDtypeStruct((bsz, seq, ROPE_PAD), jnp.float32),
        jax.ShapeDtypeStruct((bsz, seq, LRU_W), jnp.bfloat16),
        jax.ShapeDtypeStruct((bsz, 1, LRU_W), jnp.float32),
        jax.ShapeDtypeStruct((bsz, CONV_W - 1, LRU_W), jnp.float32),
    ]
    out_specs = [
        pl.BlockSpec((1, N_HEADS, tm, HEAD_PAD), lambda b, t: (b, 0, t, 0)),
        pl.BlockSpec((1, N_HEADS, tm, HEAD_PAD), lambda b, t: (b, 0, t, 0)),
        pl.BlockSpec((1, ATTN_W, tm), lambda b, t: (b, 0, t)),
        pl.BlockSpec((1, tm, KV_LORA), lambda b, t: (b, t, 0)),
        pl.BlockSpec((1, tm, ROPE_PAD), lambda b, t: (b, t, 0)),
        pl.BlockSpec((1, tm, LRU_W), lambda b, t: (b, t, 0)),
        pl.BlockSpec((1, 1, LRU_W), lambda b, t: (b, 0, 0)),
        pl.BlockSpec((1, CONV_W - 1, LRU_W), lambda b, t: (b, 0, 0)),
    ]
    return pl.pallas_call(
        functools.partial(_mixer_kernel, tm=tm),
        grid=(bsz, nt),
        in_specs=in_specs,
        out_specs=out_specs,
        out_shape=out_shape,
        scratch_shapes=[pltpu.VMEM((SUBLANES, LRU_W), jnp.float32),
                        pltpu.VMEM((1, LRU_W), jnp.float32)],
        compiler_params=pltpu.CompilerParams(
            dimension_semantics=("arbitrary", "arbitrary"), vmem_limit_bytes=VMEM_LIMIT_BYTES),
        name="mixer",
    )(x, h0, buf0p, tabq, tabk, lw["g_mix"], lw["w_in"], lw["g_q"], lw["w_uq"], lw["g_kv"],
      lw["w_uk"], lw["w_uvt"], lw["conv_w"], lw["conv_b"], lw["wg"], lw["bg"], lw["lam"], lw["g_lru"])


def _attn_sample_kernel(q_ref, ckv_m_ref, ckv_c_ref, ckv_s_ref, kpe_m_ref, kpe_c_ref, kpe_s_ref,
                        w_uk_ref, w_uvt_ref, o_ref, *, n_keys_pad):
    ckv_parts = [ckv_m_ref[0], ckv_c_ref[0, 0], ckv_s_ref[0]]
    kpe_parts = [kpe_m_ref[0], kpe_c_ref[0, 0], kpe_s_ref[0]]
    n_valid = sum(p.shape[0] for p in ckv_parts)
    pad = n_keys_pad - n_valid
    ckv = jnp.concatenate(ckv_parts + [jnp.zeros((pad, KV_LORA), jnp.float32)], axis=0)
    kpe = jnp.concatenate(kpe_parts + [jnp.zeros((pad, ROPE_PAD), jnp.float32)], axis=0)
    ckv_b = ckv.astype(jnp.bfloat16)
    kpe_shift = pltpu.roll(kpe, QK_NOPE, 1)
    kk = _dot(ckv_b, w_uk_ref[...])
    vt = _dot_nt(w_uvt_ref[...], ckv_b).astype(jnp.bfloat16)
    n_q = q_ref.shape[2]
    q_pad = jnp.zeros((LANES - n_q, HEAD_PAD), jnp.bfloat16)
    q_tiles = [jnp.concatenate([q_ref[0, h], q_pad], axis=0) for h in range(N_HEADS)]
    k_tiles = [(kk[:, h * HEAD_PAD:(h + 1) * HEAD_PAD] + kpe_shift).astype(jnp.bfloat16)
               for h in range(N_HEADS)]
    vt_tiles = [vt[h * V_DIM:(h + 1) * V_DIM, :] for h in range(N_HEADS)]
    _, accs = _dense_state(q_tiles, k_tiles, vt_tiles, n_valid)
    _store_normalised(accs, o_ref)


def _attn_sample_call(q, ckv_m, ckv_cache, ckv_s, kpe_m, kpe_cache, kpe_s, lw, *, layer):
    bsz, _, dec, _ = q.shape
    past = ckv_cache.shape[2]
    n_meta = ckv_m.shape[1]
    n_keys_pad = -(-(n_meta + past + dec) // LANES) * LANES
    return pl.pallas_call(
        functools.partial(_attn_sample_kernel, n_keys_pad=n_keys_pad),
        grid=(bsz,),
        in_specs=[
            pl.BlockSpec((1, N_HEADS, dec, HEAD_PAD), lambda b: (b, 0, 0, 0)),
            pl.BlockSpec((1, n_meta, KV_LORA), lambda b: (0, 0, 0)),
            pl.BlockSpec((1, 1, past, KV_LORA), lambda b: (layer, b, 0, 0)),
            pl.BlockSpec((1, dec, KV_LORA), lambda b: (b, 0, 0)),
            pl.BlockSpec((1, n_meta, ROPE_PAD), lambda b: (0, 0, 0)),
            pl.BlockSpec((1, 1, past, ROPE_PAD), lambda b: (layer, b, 0, 0)),
            pl.BlockSpec((1, dec, ROPE_PAD), lambda b: (b, 0, 0)),
            pl.BlockSpec((KV_LORA, N_HEADS * HEAD_PAD), lambda b: (0, 0)),
            pl.BlockSpec((ATTN_W, KV_LORA), lambda b: (0, 0)),
        ],
        out_specs=pl.BlockSpec((1, LANES, ATTN_W), lambda b: (b, 0, 0)),
        out_shape=jax.ShapeDtypeStruct((bsz, LANES, ATTN_W), jnp.bfloat16),
        compiler_params=pltpu.CompilerParams(
            dimension_semantics=("arbitrary",), vmem_limit_bytes=VMEM_LIMIT_BYTES),
        name="attn_sample",
    )(q, ckv_m, ckv_cache, ckv_s, kpe_m, kpe_cache, kpe_s, lw["w_uk"], lw["w_uvt"])


def _col_max(st):
    rows, cols = st.shape
    slab = SUBLANES
    if rows % slab == 0 and rows > slab:
        st = jnp.max(st.reshape(rows // slab, slab, cols), axis=0)
    return jnp.max(st, axis=0, keepdims=True)


def _dense_state(q_tiles, k_tiles, vt_tiles, n_valid):
    n_keys = k_tiles[0].shape[0]
    tq = q_tiles[0].shape[0]
    n_used = -(-n_valid // ONES_ROWS) * ONES_ROWS
    ones_p = jnp.ones((ONES_ROWS, n_keys), jnp.bfloat16)
    ms, accs = [], []
    for q_t, k_t, vt_t in zip(q_tiles, k_tiles, vt_tiles):
        st = _dot_nt(k_t[:n_used], q_t)
        if n_valid < n_used:
            key_idx = lax.broadcasted_iota(jnp.int32, (n_used, tq), 0)
            st = jnp.where(key_idx < n_valid, st, NEG_INF)
        m = _col_max(st)
        p = jnp.exp2(st - m).astype(jnp.bfloat16)
        if n_used < n_keys:
            p = jnp.concatenate([p, jnp.zeros((n_keys - n_used, tq), jnp.bfloat16)], axis=0)
        ms.append(m)
        accs.append(_dot(jnp.concatenate([vt_t, ones_p], axis=0), p))
    return ms, accs


def _prefix_state(q_tiles, kp_ref, vtp_ref, n_valid):
    heads = range(len(q_tiles))
    return _dense_state(q_tiles, [kp_ref[0, hh] for hh in heads],
                        [vtp_ref[0, hh * V_DIM:(hh + 1) * V_DIM, :] for hh in heads], n_valid)


def _store_normalised(accs, o_ref):
    outs = [acc[:V_DIM, :] / acc[V_DIM:V_DIM + 1, :] for acc in accs]
    o_ref[0] = jnp.concatenate(outs, axis=0).T.astype(jnp.bfloat16)


def _attn_dense_kernel(q_ref, kp_ref, vtp_ref, o_ref, *, n_valid):
    q_tiles = [q_ref[0, hh] for hh in range(q_ref.shape[1])]
    _, accs = _prefix_state(q_tiles, kp_ref, vtp_ref, n_valid)
    _store_normalised(accs, o_ref)


def _attn_causal_kernel(q_ref, kp_ref, vtp_ref, k_ref, vt_ref, o_ref,
                        sa_scr, sb_scr, ma_scr, mb_scr, m_scr, acc_scr,
                        *, tq, tk, n_prefix_valid):
    qi = pl.program_id(2)
    n_q = pl.num_programs(2)

    def q_rows(hh, q_idx, q0, qn):
        return q_ref[0, hh, pl.ds(pl.multiple_of(q_idx * tq + q0, tk), qn), :]

    m_scr[...] = jnp.full(m_scr.shape, NEG_INF, jnp.float32)
    acc_scr[...] = jnp.zeros(acc_scr.shape, jnp.float32)
    ones_k = jnp.ones((ONES_ROWS, tk), jnp.bfloat16)

    def scores(tile, s_ref, mx_ref, q_idx=qi, q0=0, qn=tq):
        start = pl.multiple_of(tile * tk, tk)
        for hh in range(HEADS_PER_STEP):
            st = _dot_nt(k_ref[0, hh, pl.ds(start, tk), :], q_rows(hh, q_idx, q0, qn))
            s_ref[hh, :, q0:q0 + qn] = st
            mx_ref[hh, :, q0:q0 + qn] = _col_max(st)

    def softmax_pv(tile, s_ref, mx_ref, mask=None, q0=0, qn=tq):
        start = pl.multiple_of(tile * tk, tk)
        for hh in range(HEADS_PER_STEP):
            st = s_ref[hh, :, q0:q0 + qn]
            if mask is None:
                m_blk = mx_ref[hh, :, q0:q0 + qn]
            else:
                st = jnp.where(mask, st, NEG_INF)
                m_blk = _col_max(st)
            m_old = m_scr[hh, :, q0:q0 + qn]
            m_new = jnp.maximum(m_old, m_blk)
            p = jnp.exp2(st - m_new).astype(jnp.bfloat16)
            vt_t = vt_ref[0, hh * V_DIM:(hh + 1) * V_DIM, pl.ds(start, tk)]
            vt_aug = jnp.concatenate([vt_t, ones_k], axis=0)
            acc_scr[hh, :, q0:q0 + qn] = (jnp.exp2(m_old - m_new) * acc_scr[hh, :, q0:q0 + qn]
                                          + _dot(vt_aug, p))
            m_scr[hh, :, q0:q0 + qn] = m_new

    @pl.when(qi == 0)
    def _():
        scores(0, sa_scr, ma_scr)

    def body(jj, carry):
        t0 = 2 * jj
        scores(t0 + 1, sb_scr, mb_scr)
        softmax_pv(t0, sa_scr, ma_scr)
        scores(t0 + 2, sa_scr, ma_scr)
        softmax_pv(t0 + 1, sb_scr, mb_scr)
        return carry

    lax.fori_loop(0, qi, body, 0)

    d1 = 2 * qi
    key_chunk = lax.broadcasted_iota(jnp.int32, (tk, tq), 0) // CHUNK
    qry_chunk = lax.broadcasted_iota(jnp.int32, (tk, tq), 1) // CHUNK
    scores(d1 + 1, sb_scr, mb_scr, q0=tk, qn=tk)
    softmax_pv(d1, sa_scr, ma_scr, mask=key_chunk <= qry_chunk)
    scores(0, sa_scr, ma_scr, q_idx=jnp.minimum(qi + 1, n_q - 1))
    softmax_pv(d1 + 1, sb_scr, mb_scr, mask=(key_chunk <= qry_chunk)[:, :tk], q0=tk, qn=tk)
    ms_p, accs_p = _prefix_state([q_rows(hh, qi, 0, tq) for hh in range(HEADS_PER_STEP)],
                                 kp_ref, vtp_ref, n_prefix_valid)
    accs = []
    for hh in range(HEADS_PER_STEP):
        m_run = m_scr[hh]
        m_all = jnp.maximum(m_run, ms_p[hh])
        accs.append(jnp.exp2(m_run - m_all) * acc_scr[hh] + jnp.exp2(ms_p[hh] - m_all) * accs_p[hh])
    _store_normalised(accs, o_ref)


def _attn_dense_call(q, kp, vtp, *, n_prefix_valid):
    bsz, _, tq, _ = q.shape
    n_prefix = kp.shape[2]
    return pl.pallas_call(
        functools.partial(_attn_dense_kernel, n_valid=n_prefix_valid),
        grid=(bsz,),
        in_specs=[
            pl.BlockSpec((1, N_HEADS, tq, HEAD_PAD), lambda b: (b, 0, 0, 0)),
            pl.BlockSpec((1, N_HEADS, n_prefix, HEAD_PAD), lambda b: (b, 0, 0, 0)),
            pl.BlockSpec((1, ATTN_W, n_prefix), lambda b: (b, 0, 0)),
        ],
        out_specs=pl.BlockSpec((1, tq, ATTN_W), lambda b: (b, 0, 0)),
        out_shape=jax.ShapeDtypeStruct((bsz, tq, ATTN_W), jnp.bfloat16),
        compiler_params=pltpu.CompilerParams(
            dimension_semantics=("arbitrary",), vmem_limit_bytes=VMEM_LIMIT_BYTES),
        name="attn_dense",
    )(q, kp, vtp)


def _attn_causal_call(q, kp, vtp, k, vt, *, tq, n_prefix_valid):
    bsz, _, seq, _ = q.shape
    n_prefix = kp.shape[2]
    tk = tq // 2
    f32 = jnp.float32
    return pl.pallas_call(
        functools.partial(_attn_causal_kernel, tq=tq, tk=tk, n_prefix_valid=n_prefix_valid),
        grid=(bsz, N_HEADS // HEADS_PER_STEP, seq // tq),
        in_specs=[
            pl.BlockSpec((1, HEADS_PER_STEP, seq, HEAD_PAD), lambda b, hp, i: (b, hp, 0, 0)),
            pl.BlockSpec((1, HEADS_PER_STEP, n_prefix, HEAD_PAD), lambda b, hp, i: (0, hp, 0, 0)),
            pl.BlockSpec((1, HEADS_PER_STEP * V_DIM, n_prefix), lambda b, hp, i: (0, hp, 0)),
            pl.BlockSpec((1, HEADS_PER_STEP, seq, HEAD_PAD), lambda b, hp, i: (b, hp, 0, 0)),
            pl.BlockSpec((1, HEADS_PER_STEP * V_DIM, seq), lambda b, hp, i: (b, hp, 0)),
        ],
        out_specs=pl.BlockSpec((1, tq, HEADS_PER_STEP * V_DIM), lambda b, hp, i: (b, i, hp)),
        out_shape=jax.ShapeDtypeStruct((bsz, seq, ATTN_W), jnp.bfloat16),
        scratch_shapes=[pltpu.VMEM((HEADS_PER_STEP, tk, tq), f32),
                        pltpu.VMEM((HEADS_PER_STEP, tk, tq), f32),
                        pltpu.VMEM((HEADS_PER_STEP, 1, tq), f32),
                        pltpu.VMEM((HEADS_PER_STEP, 1, tq), f32),
                        pltpu.VMEM((HEADS_PER_STEP, 1, tq), f32),
                        pltpu.VMEM((HEADS_PER_STEP, V_AUG, tq), f32)],
        compiler_params=pltpu.CompilerParams(
            dimension_semantics=("arbitrary", "arbitrary", "arbitrary"),
            vmem_limit_bytes=VMEM_LIMIT_BYTES),
        name="attn_causal",
    )(q, kp, vtp, k, vt)


def _out_mlp_kernel(o_ref, y_ref, x_ref, g_attn_ref, w_out_ref, g_mlp_ref, w_up_ref, w_down_ref,
                    g_fin_ref, out_ref, hn_scr, *, final):
    c = pl.program_id(1)

    @pl.when(c == 0)
    def _():
        on = _rms(o_ref[...].astype(jnp.float32), g_attn_ref[...]).astype(jnp.bfloat16)
        h = (x_ref[...] + _dot(on, w_out_ref[0:ATTN_W, :])
             + _dot(y_ref[...], w_out_ref[ATTN_W:ATTN_W + LRU_W, :]))
        hn_scr[...] = _rms(h, g_mlp_ref[...]).astype(jnp.bfloat16)
        out_ref[...] = h

    u = _dot(hn_scr[...], w_up_ref[...])
    u = jnp.square(jnp.maximum(u, 0.0)).astype(jnp.bfloat16)
    out_ref[...] += _dot(u, w_down_ref[...])

    if final:
        @pl.when(c == pl.num_programs(1) - 1)
        def _():
            out_ref[...] = _rms(out_ref[...], g_fin_ref[...])


def _out_mlp_call(o, y, x, lw, g_fin, *, tm, final):
    rows = x.shape[0]
    const = lambda r, c: (0, 0)
    return pl.pallas_call(
        functools.partial(_out_mlp_kernel, final=final),
        grid=(rows // tm, D_FF // F_CHUNK),
        in_specs=[
            pl.BlockSpec((tm, ATTN_W), lambda r, c: (r, 0)),
            pl.BlockSpec((tm, LRU_W), lambda r, c: (r, 0)),
            pl.BlockSpec((tm, D_MODEL), lambda r, c: (r, 0)),
            pl.BlockSpec((1, ATTN_W), const),
            pl.BlockSpec((D_MODEL, D_MODEL), const),
            pl.BlockSpec((1, D_MODEL), const),
            pl.BlockSpec((D_MODEL, F_CHUNK), lambda r, c: (0, c)),
            pl.BlockSpec((F_CHUNK, D_MODEL), lambda r, c: (c, 0)),
            pl.BlockSpec((1, D_MODEL), const),
        ],
        out_specs=pl.BlockSpec((tm, D_MODEL), lambda r, c: (r, 0)),
        out_shape=jax.ShapeDtypeStruct((rows, D_MODEL), jnp.float32),
        scratch_shapes=[pltpu.VMEM((tm, D_MODEL), jnp.bfloat16)],
        compiler_params=pltpu.CompilerParams(
            dimension_semantics=("arbitrary", "arbitrary"), vmem_limit_bytes=VMEM_LIMIT_BYTES),
        name="out_mlp",
    )(o, y, x, lw["g_attn"], lw["w_out"], lw["g_mlp"], lw["w_up"], lw["w_down"], g_fin)


def _rope_tables(pos):
    inv = ROPE_THETA ** (-jnp.arange(0, QK_ROPE, 2, dtype=jnp.float32) / QK_ROPE)
    ang = pos.astype(jnp.float32)[:, None] * inv[None, :]
    cos, sin = jnp.cos(ang), jnp.sin(ang)
    zeros = jnp.zeros((pos.shape[0], LANES - QK_ROPE), jnp.float32)
    z16 = jnp.zeros_like(sin)
    tab_c = jnp.concatenate([cos, cos, zeros], axis=1)
    tab_m = jnp.concatenate([-sin, z16, zeros], axis=1)
    tab_p = jnp.concatenate([z16, sin, zeros], axis=1)
    return jnp.stack([tab_c, tab_m, tab_p])


def _query_tables(tabk):
    tq = jnp.roll(tabk, QK_NOPE, axis=2)
    tq = tq.at[0, :, :QK_NOPE].set(1.0)
    return tq * (SM_SCALE * LOG2_E)


def _block_diag_gates(w_a, w_x):
    per_half = LRU_BLOCKS // 2
    halves = []
    for s in range(2):
        bd_a = jax.scipy.linalg.block_diag(*[w_a[s * per_half + i] for i in range(per_half)])
        bd_x = jax.scipy.linalg.block_diag(*[w_x[s * per_half + i] for i in range(per_half)])
        halves.append(jnp.concatenate([bd_a, bd_x], axis=1))
    return jnp.stack(halves).astype(jnp.bfloat16)


def _layer_weights(l, norm_mix_g, w_in, q_norm_g, w_uq, kv_norm_g, w_ukv, conv_w, conv_b,
                   w_gate_a, b_gate_a, w_gate_x, b_gate_x, lru_lambda, attn_out_g, lru_out_g,
                   w_out, norm_mlp_g, w_up, w_down):
    bf = jnp.bfloat16
    row = lambda v: v.reshape(1, -1).astype(jnp.float32)
    w_in_p = jnp.concatenate(
        [w_in[l][:, :COL_KR + QK_ROPE], jnp.zeros((D_MODEL, ROPE_PAD - QK_ROPE), w_in.dtype),
         w_in[l][:, COL_KR + QK_ROPE:]], axis=1).astype(bf)
    w_uq_p = jnp.pad(w_uq[l].reshape(Q_LORA, N_HEADS, QK_NOPE + QK_ROPE),
                     ((0, 0), (0, 0), (0, HEAD_PAD - QK_NOPE - QK_ROPE))).reshape(Q_LORA, -1).astype(bf)
    w_ukv_h = w_ukv[l].reshape(KV_LORA, N_HEADS, QK_NOPE + V_DIM)
    w_uk_p = jnp.pad(w_ukv_h[:, :, :QK_NOPE],
                     ((0, 0), (0, 0), (0, HEAD_PAD - QK_NOPE))).reshape(KV_LORA, -1).astype(bf)
    w_uvt = w_ukv_h[:, :, QK_NOPE:].reshape(KV_LORA, ATTN_W).T.astype(bf)
    return {
        "g_mix": row(norm_mix_g[l]), "w_in": w_in_p, "g_q": row(q_norm_g[l]), "w_uq": w_uq_p,
        "g_kv": row(kv_norm_g[l]), "w_uk": w_uk_p, "w_uvt": w_uvt,
        "conv_w": conv_w[l].astype(jnp.float32), "conv_b": row(conv_b[l]),
        "wg": _block_diag_gates(w_gate_a[l], w_gate_x[l]),
        "bg": jnp.stack([b_gate_a[l], b_gate_x[l]]).astype(jnp.float32),
        "lam": row(lru_lambda[l]), "g_lru": row(lru_out_g[l]), "g_attn": row(attn_out_g[l]),
        "w_out": w_out[l].astype(bf), "g_mlp": row(norm_mlp_g[l]),
        "w_up": w_up[l].astype(bf), "w_down": w_down[l].astype(bf),
    }


def _pad_buf(buf):
    return jnp.pad(buf, ((0, 0), (SUBLANES - (CONV_W - 1), 0), (0, 0)))


def _pad_axis(a, axis, size):
    pad = [(0, 0)] * a.ndim
    pad[axis] = (0, size - a.shape[axis])
    return jnp.pad(a, pad)


def kernel(x_prompt, x_sample, cache_ckv, cache_kpe, state_lru_h, state_conv, meta_tokens,
           norm_mix_g, w_in, q_norm_g, w_uq, kv_norm_g, w_ukv, conv_w, conv_b,
           w_gate_a, b_gate_a, w_gate_x, b_gate_x, lru_lambda, attn_out_g, lru_out_g,
           w_out, norm_mlp_g, w_up, w_down, final_norm_g):
    b_p, seq, _ = x_prompt.shape
    b_s, dec_seq, _ = x_sample.shape
    past_len = cache_ckv.shape[2]
    assert (past_len + dec_seq - 1) // CHUNK == past_len // CHUNK
    tm_p = 512
    tq_p = 1024
    tm_o = 1024
    assert seq % tm_p == 0 and seq % tq_p == 0 and (tq_p // 2) % CHUNK == 0

    meta_pos = jnp.arange(-N_META, 0, dtype=jnp.int32)
    prompt_pos = jnp.arange(seq, dtype=jnp.int32)
    sample_pos = past_len + jnp.arange(dec_seq, dtype=jnp.int32)
    tabk_m, tabk_p, tabk_s = _rope_tables(meta_pos), _rope_tables(prompt_pos), _rope_tables(sample_pos)
    tabq_m, tabq_p, tabq_s = _query_tables(tabk_m), _query_tables(tabk_p), _query_tables(tabk_s)
    g_fin = final_norm_g.reshape(1, -1).astype(jnp.float32)

    cache_kpe_pad = _pad_axis(cache_kpe, 3, ROPE_PAD)
    tq_small = LANES

    h_meta = meta_tokens[None].astype(jnp.float32)
    h_p, h_s = x_prompt, x_sample
    zero_h = jnp.zeros((1, 1, LRU_W), jnp.float32)
    zero_buf = jnp.zeros((1, SUBLANES, LRU_W), jnp.float32)
    outs = {name: [] for name in ("ckv_p", "kpe_p", "lru_p", "conv_p", "ckv_s", "kpe_s", "lru_s", "conv_s")}
    for l in range(DEPTH):
        lw = _layer_weights(l, norm_mix_g, w_in, q_norm_g, w_uq, kv_norm_g, w_ukv, conv_w, conv_b,
                            w_gate_a, b_gate_a, w_gate_x, b_gate_x, lru_lambda, attn_out_g,
                            lru_out_g, w_out, norm_mlp_g, w_up, w_down)
        last = l + 1 == DEPTH
        mq, mk, mvt, m_ckv, m_kpe, m_y, m_h, m_buf = _mixer_call(
            h_meta, zero_h, zero_buf, tabq_m, tabk_m, lw, tm=N_META, shared_state=True)
        pq, pk, pvt, p_ckv, p_kpe, p_y, p_h, p_buf = _mixer_call(
            h_p, m_h, _pad_buf(m_buf), tabq_p, tabk_p, lw, tm=tm_p, shared_state=True)
        mk_p, mvt_p = _pad_axis(mk, 2, LANES), _pad_axis(mvt, 2, LANES)
        p_o = _attn_causal_call(pq, mk_p, mvt_p, pk, pvt, tq=tq_p, n_prefix_valid=N_META)
        h_p = _out_mlp_call(p_o.reshape(b_p * seq, ATTN_W), p_y.reshape(b_p * seq, LRU_W),
                            h_p.reshape(b_p * seq, D_MODEL), lw, g_fin, tm=tm_o,
                            final=last).reshape(b_p, seq, D_MODEL)
        sq, sk, svt, s_ckv, s_kpe, s_y, s_h, s_buf = _mixer_call(
            h_s, state_lru_h[l][:, None, :], _pad_buf(state_conv[l]), tabq_s, tabk_s, lw,
            tm=dec_seq, shared_state=False)
        s_o = _attn_sample_call(sq, m_ckv, cache_ckv, s_ckv, m_kpe, cache_kpe_pad, s_kpe, lw,
                                layer=l)[:, :dec_seq]
        h_s = _out_mlp_call(s_o.reshape(b_s * dec_seq, ATTN_W), s_y.reshape(b_s * dec_seq, LRU_W),
                            h_s.reshape(b_s * dec_seq, D_MODEL), lw, g_fin, tm=b_s * dec_seq,
                            final=last).reshape(b_s, dec_seq, D_MODEL)
        if not last:
            m_o = _attn_dense_call(_pad_axis(mq, 2, tq_small), mk_p, mvt_p,
                                   n_prefix_valid=N_META)[:, :N_META]
            h_meta = _out_mlp_call(m_o.reshape(N_META, ATTN_W), m_y.reshape(N_META, LRU_W),
                                   h_meta.reshape(N_META, D_MODEL), lw, g_fin, tm=N_META,
                                   final=False).reshape(1, N_META, D_MODEL)
        outs["ckv_p"].append(p_ckv); outs["kpe_p"].append(p_kpe[:, :, :QK_ROPE])
        outs["lru_p"].append(p_h[:, 0]); outs["conv_p"].append(p_buf)
        outs["ckv_s"].append(s_ckv); outs["kpe_s"].append(s_kpe[:, :, :QK_ROPE])
        outs["lru_s"].append(s_h[:, 0]); outs["conv_s"].append(s_buf)

    return (h_p, h_s, jnp.stack(outs["ckv_p"]), jnp.stack(outs["kpe_p"]), jnp.stack(outs["lru_p"]),
            jnp.stack(outs["conv_p"]), jnp.stack(outs["ckv_s"]), jnp.stack(outs["kpe_s"]),
            jnp.stack(outs["lru_s"]), jnp.stack(outs["conv_s"]))
```

```python
import functools

import jax
import jax.numpy as jnp
from jax import lax
from jax.experimental import pallas as pl
from jax.experimental.pallas import tpu as pltpu

D_MODEL = 1024
DEPTH = 2
CHUNK = 64
N_META = 16
ATTN_W = 512
LRU_W = 512
V_DIM = 64
N_HEADS = 8
QK_NOPE = 64
QK_ROPE = 32
Q_LORA = 768
KV_LORA = 256
LRU_BLOCKS = 8
LRU_BW = LRU_W // LRU_BLOCKS
LRU_C = 8.0
CONV_W = 4
D_FF = 4 * D_MODEL
ROPE_THETA = 10000.0
EPS = 1e-6
SM_SCALE = (QK_NOPE + QK_ROPE) ** -0.5
NEG_INF = -1e30
LOG2_E = 1.4426950408889634

LANES = 128
SUBLANES = 8
HEAD_PAD = LANES
VMEM_LIMIT_BYTES = 56 * 1024 * 1024

ROPE_PAD = LANES
COL_CQ = 0
COL_CKV = Q_LORA
COL_KR = Q_LORA + KV_LORA
COL_XR = COL_KR + ROPE_PAD
COL_GR = COL_XR + LRU_W
IN_W_PAD = COL_GR + LRU_W

HEADS_PER_STEP = 2
ONES_ROWS = 16
V_AUG = V_DIM + ONES_ROWS
F_CHUNK = 1024
Q_SLICE = 256


def _rms(x, g):
    return x * lax.rsqrt(jnp.mean(x * x, axis=-1, keepdims=True) + EPS) * g


def _rope_lanes(x, tab_ref):
    half = QK_ROPE // 2
    return (x * tab_ref[0] + pltpu.roll(x, LANES - half, 1) * tab_ref[1]
            + pltpu.roll(x, half, 1) * tab_ref[2])


def _dot(a, b):
    return jnp.dot(a, b, preferred_element_type=jnp.float32)


def _dot_nt(a, b):
    return lax.dot_general(a, b, (((1,), (1,)), ((), ())), preferred_element_type=jnp.float32)


def _mixer_kernel(x_ref, h0_ref, buf0_ref, tabq_ref, tabk_ref, g_mix_ref, w_in_ref, g_q_ref,
                  w_uq_ref, g_kv_ref, w_uk_ref, w_uvt_ref, conv_w_ref, conv_b_ref, wg_ref, bg_ref,
                  lam_ref, g_lru_ref,
                  q_ref, k_ref, vt_ref, ckv_ref, kpe_ref, y_ref, hlast_ref, buf_ref,
                  tail_scr, hcar_scr, *, tm):
    t = pl.program_id(1)

    @pl.when(t == 0)
    def _():
        tail_scr[...] = buf0_ref[0]
        hcar_scr[...] = h0_ref[0]

    x = x_ref[0]
    xn = _rms(x, g_mix_ref[...]).astype(jnp.bfloat16)
    z = _dot(xn, w_in_ref[...])

    cqn = _rms(z[:, COL_CQ:COL_CQ + Q_LORA], g_q_ref[...]).astype(jnp.bfloat16)
    q = _dot(cqn, w_uq_ref[...])
    for h in range(N_HEADS):
        qh = q[:, h * HEAD_PAD:(h + 1) * HEAD_PAD]
        q_ref[0, h] = _rope_lanes(qh, tabq_ref).astype(jnp.bfloat16)

    ckv = _rms(z[:, COL_CKV:COL_CKV + KV_LORA], g_kv_ref[...])
    ckv_ref[0] = ckv
    ckv_b = ckv.astype(jnp.bfloat16)
    kpe = _rope_lanes(z[:, COL_KR:COL_KR + ROPE_PAD], tabk_ref)
    kpe_ref[0] = kpe
    kpe_shift = pltpu.roll(kpe, QK_NOPE, 1)
    kk = _dot(ckv_b, w_uk_ref[...])
    for h in range(N_HEADS):
        k_ref[0, h] = (kk[:, h * HEAD_PAD:(h + 1) * HEAD_PAD] + kpe_shift).astype(jnp.bfloat16)
    vt_ref[0] = _dot_nt(w_uvt_ref[...], ckv_b).astype(jnp.bfloat16)

    x_r = z[:, COL_XR:COL_XR + LRU_W]
    g_r = z[:, COL_GR:COL_GR + LRU_W]
    n_slabs = tm // SUBLANES
    rows8 = lax.broadcasted_iota(jnp.int32, (SUBLANES, LRU_W), 0)
    slabs = [tail_scr[...]] + [x_r[g * SUBLANES:(g + 1) * SUBLANES, :] for g in range(n_slabs)]
    xc = conv_b_ref[...] + x_r * conv_w_ref[CONV_W - 1:CONV_W, :]
    for back in range(1, CONV_W):
        rolled = [pltpu.roll(s, back, 0) for s in slabs]
        shifted = jnp.concatenate(
            [jnp.where(rows8 < back, rolled[g], rolled[g + 1]) for g in range(n_slabs)], axis=0)
        xc = xc + shifted * conv_w_ref[CONV_W - 1 - back:CONV_W - back, :]
    tail_scr[...] = slabs[-1]
    buf_ref[0] = tail_scr[SUBLANES - (CONV_W - 1):SUBLANES, :]

    half_w = LRU_W // 2
    gates = [_dot(xc[:, s * half_w:(s + 1) * half_w].astype(jnp.bfloat16), wg_ref[s]) for s in range(2)]
    pre_a = jnp.concatenate([gates[0][:, :half_w], gates[1][:, :half_w]], axis=1)
    pre_x = jnp.concatenate([gates[0][:, half_w:], gates[1][:, half_w:]], axis=1)
    r = jax.nn.sigmoid(pre_a + bg_ref[0:1, :])
    gi = jax.nn.sigmoid(pre_x + bg_ref[1:2, :])
    neg_lam = -lam_ref[...]
    softplus = jnp.maximum(neg_lam, 0.0) + jnp.log1p(jnp.exp(-jnp.abs(neg_lam)))
    log_a = (-LRU_C) * r * softplus
    a = jnp.exp(log_a)
    one_m_a2 = -jnp.tanh(log_a) * (a * a + 1.0)
    root = jnp.where(one_m_a2 > 0.0, one_m_a2 * lax.rsqrt(one_m_a2), 0.0)
    b = root * (gi * xc)

    carry = jnp.broadcast_to(hcar_scr[...], (SUBLANES, LRU_W))
    h_slabs = []
    for g in range(n_slabs):
        a_g = a[g * SUBLANES:(g + 1) * SUBLANES, :]
        b_g = b[g * SUBLANES:(g + 1) * SUBLANES, :]
        shift = 1
        while shift < SUBLANES:
            keep = rows8 >= shift
            a_prev = jnp.where(keep, pltpu.roll(a_g, shift, 0), 1.0)
            b_prev = jnp.where(keep, pltpu.roll(b_g, shift, 0), 0.0)
            b_g = a_g * b_prev + b_g
            a_g = a_g * a_prev
            shift *= 2
        h_g = a_g * carry + b_g
        h_slabs.append(h_g)
        carry = jnp.broadcast_to(h_g[SUBLANES - 1:SUBLANES, :], (SUBLANES, LRU_W))
    hh = jnp.concatenate(h_slabs, axis=0)
    h_last = carry[0:1, :]
    hcar_scr[...] = h_last
    hlast_ref[0] = h_last
    y = hh * jax.nn.gelu(g_r)
    y_ref[0] = _rms(y, g_lru_ref[...]).astype(jnp.bfloat16)


def _mixer_call(x, h0, buf0p, tabq, tabk, lw, *, tm, shared_state):
    bsz, seq, _ = x.shape
    nt = seq // tm
    const2 = lambda b, t: (0, 0)
    const3 = lambda b, t: (0, 0, 0)
    state_idx = (lambda b, t: (0, 0, 0)) if shared_state else (lambda b, t: (b, 0, 0))
    in_specs = [
        pl.BlockSpec((1, tm, D_MODEL), lambda b, t: (b, t, 0)),
        pl.BlockSpec((1, 1, LRU_W), state_idx),
        pl.BlockSpec((1, SUBLANES, LRU_W), state_idx),
        pl.BlockSpec((3, tm, LANES), lambda b, t: (0, t, 0)),
        pl.BlockSpec((3, tm, LANES), lambda b, t: (0, t, 0)),
        pl.BlockSpec((1, D_MODEL), const2),
        pl.BlockSpec((D_MODEL, IN_W_PAD), const2),
        pl.BlockSpec((1, Q_LORA), const2),
        pl.BlockSpec((Q_LORA, N_HEADS * HEAD_PAD), const2),
        pl.BlockSpec((1, KV_LORA), const2),
        pl.BlockSpec((KV_LORA, N_HEADS * HEAD_PAD), const2),
        pl.BlockSpec((ATTN_W, KV_LORA), const2),
        pl.BlockSpec((CONV_W, LRU_W), const2),
        pl.BlockSpec((1, LRU_W), const2),
        pl.BlockSpec((2, LRU_W // 2, LRU_W), const3),
        pl.BlockSpec((2, LRU_W), const2),
        pl.BlockSpec((1, LRU_W), const2),
        pl.BlockSpec((1, LRU_W), const2),
    ]
    out_shape = [
        jax.ShapeDtypeStruct((bsz, N_HEADS, seq, HEAD_PAD), jnp.bfloat16),
        jax.ShapeDtypeStruct((bsz, N_HEADS, seq, HEAD_PAD), jnp.bfloat16),
        jax.ShapeDtypeStruct((bsz, ATTN_W, seq), jnp.bfloat16),
        jax.ShapeDtypeStruct((bsz, seq, KV_LORA), jnp.float32),
        jax.ShapeDtypeStruct((bsz, seq, ROPE_PAD), jnp.float32),
        jax.ShapeDtypeStruct((bsz, seq, LRU_W), jnp.bfloat16),
        jax.ShapeDtypeStruct((bsz, 1, LRU_W), jnp.float32),
        jax.ShapeDtypeStruct((bsz, CONV_W - 1, LRU_W), jnp.float32),
    ]
    out_specs = [
        pl.BlockSpec((1, N_HEADS, tm, HEAD_PAD), lambda b, t: (b, 0, t, 0)),
        pl.BlockSpec((1, N_HEADS, tm, HEAD_PAD), lambda b, t: (b, 0, t, 0)),
        pl.BlockSpec((1, ATTN_W, tm), lambda b, t: (b, 0, t)),
        pl.BlockSpec((1, tm, KV_LORA), lambda b, t: (b, t, 0)),
        pl.BlockSpec((1, tm, ROPE_PAD), lambda b, t: (b, t, 0)),
        pl.BlockSpec((1, tm, LRU_W), lambda b, t: (b, t, 0)),
        pl.BlockSpec((1, 1, LRU_W), lambda b, t: (b, 0, 0)),
        pl.BlockSpec((1, CONV_W - 1, LRU_W), lambda b, t: (b, 0, 0)),
    ]
    return pl.pallas_call(
        functools.partial(_mixer_kernel, tm=tm),
        grid=(bsz, nt),
        in_specs=in_specs,
        out_specs=out_specs,
        out_shape=out_shape,
        scratch_shapes=[pltpu.VMEM((SUBLANES, LRU_W), jnp.float32),
                        pltpu.VMEM((1, LRU_W), jnp.float32)],
        compiler_params=pltpu.CompilerParams(
            dimension_semantics=("arbitrary", "arbitrary"), vmem_limit_bytes=VMEM_LIMIT_BYTES),
        name="mixer",
    )(x, h0, buf0p, tabq, tabk, lw["g_mix"], lw["w_in"], lw["g_q"], lw["w_uq"], lw["g_kv"],
      lw["w_uk"], lw["w_uvt"], lw["conv_w"], lw["conv_b"], lw["wg"], lw["bg"], lw["lam"], lw["g_lru"])


def _attn_sample_kernel(q_ref, ckv_m_ref, ckv_c_ref, ckv_s_ref, kpe_m_ref, kpe_c_ref, kpe_s_ref,
                        w_uk_ref, w_uvt_ref, o_ref, *, n_keys_pad):
    ckv_parts = [ckv_m_ref[0], ckv_c_ref[0, 0], ckv_s_ref[0]]
    kpe_parts = [kpe_m_ref[0], kpe_c_ref[0, 0], kpe_s_ref[0]]
    n_valid = sum(p.shape[0] for p in ckv_parts)
    pad = n_keys_pad - n_valid
    ckv = jnp.concatenate(ckv_parts + [jnp.zeros((pad, KV_LORA), jnp.float32)], axis=0)
    kpe = jnp.concatenate(kpe_parts + [jnp.zeros((pad, ROPE_PAD), jnp.float32)], axis=0)
    ckv_b = ckv.astype(jnp.bfloat16)
    kpe_shift = pltpu.roll(kpe, QK_NOPE, 1)
    kk = _dot(ckv_b, w_uk_ref[...])
    vt = _dot_nt(w_uvt_ref[...], ckv_b).astype(jnp.bfloat16)
    n_q = q_ref.shape[2]
    q_pad = jnp.zeros((LANES - n_q, HEAD_PAD), jnp.bfloat16)
    q_tiles = [jnp.concatenate([q_ref[0, h], q_pad], axis=0) for h in range(N_HEADS)]
    k_tiles = [(kk[:, h * HEAD_PAD:(h + 1) * HEAD_PAD] + kpe_shift).astype(jnp.bfloat16)
               for h in range(N_HEADS)]
    vt_tiles = [vt[h * V_DIM:(h + 1) * V_DIM, :] for h in range(N_HEADS)]
    _, accs = _dense_state(q_tiles, k_tiles, vt_tiles, n_valid)
    _store_normalised(accs, o_ref)


def _attn_sample_call(q, ckv_m, ckv_cache, ckv_s, kpe_m, kpe_cache, kpe_s, lw, *, layer):
    bsz, _, dec, _ = q.shape
    past = ckv_cache.shape[2]
    n_meta = ckv_m.shape[1]
    n_keys_pad = -(-(n_meta + past + dec) // LANES) * LANES
    return pl.pallas_call(
        functools.partial(_attn_sample_kernel, n_keys_pad=n_keys_pad),
        grid=(bsz,),
        in_specs=[
            pl.BlockSpec((1, N_HEADS, dec, HEAD_PAD), lambda b: (b, 0, 0, 0)),
            pl.BlockSpec((1, n_meta, KV_LORA), lambda b: (0, 0, 0)),
            pl.BlockSpec((1, 1, past, KV_LORA), lambda b: (layer, b, 0, 0)),
            pl.BlockSpec((1, dec, KV_LORA), lambda b: (b, 0, 0)),
            pl.BlockSpec((1, n_meta, ROPE_PAD), lambda b: (0, 0, 0)),
            pl.BlockSpec((1, 1, past, ROPE_PAD), lambda b: (layer, b, 0, 0)),
            pl.BlockSpec((1, dec, ROPE_PAD), lambda b: (b, 0, 0)),
            pl.BlockSpec((KV_LORA, N_HEADS * HEAD_PAD), lambda b: (0, 0)),
            pl.BlockSpec((ATTN_W, KV_LORA), lambda b: (0, 0)),
        ],
        out_specs=pl.BlockSpec((1, LANES, ATTN_W), lambda b: (b, 0, 0)),
        out_shape=jax.ShapeDtypeStruct((bsz, LANES, ATTN_W), jnp.bfloat16),
        compiler_params=pltpu.CompilerParams(
            dimension_semantics=("arbitrary",), vmem_limit_bytes=VMEM_LIMIT_BYTES),
        name="attn_sample",
    )(q, ckv_m, ckv_cache, ckv_s, kpe_m, kpe_cache, kpe_s, lw["w_uk"], lw["w_uvt"])


def _col_max(st):
    rows, cols = st.shape
    slab = SUBLANES
    if rows % slab == 0 and rows > slab:
        st = jnp.max(st.reshape(rows // slab, slab, cols), axis=0)
    return jnp.max(st, axis=0, keepdims=True)


def _dense_state(q_tiles, k_tiles, vt_tiles, n_valid):
    n_keys = k_tiles[0].shape[0]
    tq = q_tiles[0].shape[0]
    n_used = -(-n_valid // ONES_ROWS) * ONES_ROWS
    ones_p = jnp.ones((ONES_ROWS, n_keys), jnp.bfloat16)
    ms, accs = [], []
    for q_t, k_t, vt_t in zip(q_tiles, k_tiles, vt_tiles):
        st = _dot_nt(k_t[:n_used], q_t)
        if n_valid < n_used:
            key_idx = lax.broadcasted_iota(jnp.int32, (n_used, tq), 0)
            st = jnp.where(key_idx < n_valid, st, NEG_INF)
        m = _col_max(st)
        p = jnp.exp2(st - m).astype(jnp.bfloat16)
        if n_used < n_keys:
            p = jnp.concatenate([p, jnp.zeros((n_keys - n_used, tq), jnp.bfloat16)], axis=0)
        ms.append(m)
        accs.append(_dot(jnp.concatenate([vt_t, ones_p], axis=0), p))
    return ms, accs


def _prefix_state(q_tiles, kp_ref, vtp_ref, n_valid):
    heads = range(len(q_tiles))
    return _dense_state(q_tiles, [kp_ref[0, hh] for hh in heads],
                        [vtp_ref[0, hh * V_DIM:(hh + 1) * V_DIM, :] for hh in heads], n_valid)


def _store_normalised(accs, o_ref):
    outs = [acc[:V_DIM, :] / acc[V_DIM:V_DIM + 1, :] for acc in accs]
    o_ref[0] = jnp.concatenate(outs, axis=0).T.astype(jnp.bfloat16)


def _attn_dense_kernel(q_ref, kp_ref, vtp_ref, o_ref, *, n_valid):
    q_tiles = [q_ref[0, hh] for hh in range(q_ref.shape[1])]
    _, accs = _prefix_state(q_tiles, kp_ref, vtp_ref, n_valid)
    _store_normalised(accs, o_ref)


def _attn_causal_kernel(q_ref, kp_ref, vtp_ref, k_ref, vt_ref, o_ref,
                        sa_scr, sb_scr, ma_scr, mb_scr, m_scr, acc_scr,
                        *, tq, tk, n_prefix_valid):
    qi = pl.program_id(2)
    n_q = pl.num_programs(2)

    def q_rows(hh, q_idx, q0, qn):
        return q_ref[0, hh, pl.ds(pl.multiple_of(q_idx * tq + q0, tk), qn), :]

    m_scr[...] = jnp.full(m_scr.shape, NEG_INF, jnp.float32)
    acc_scr[...] = jnp.zeros(acc_scr.shape, jnp.float32)
    ones_k = jnp.ones((ONES_ROWS, tk), jnp.bfloat16)

    def scores(tile, s_ref, mx_ref, q_idx=qi, q0=0, qn=tq, heads=range(HEADS_PER_STEP)):
        start = pl.multiple_of(tile * tk, tk)
        for hh in heads:
            st = _dot_nt(k_ref[0, hh, pl.ds(start, tk), :], q_rows(hh, q_idx, q0, qn))
            s_ref[hh, :, q0:q0 + qn] = st
            mx_ref[hh, :, q0:q0 + qn] = _col_max(st)

    def softmax_pv(tile, s_ref, mx_ref, mask=None, q0=0, qn=tq, heads=range(HEADS_PER_STEP)):
        start = pl.multiple_of(tile * tk, tk)
        for hh in heads:
            st = s_ref[hh, :, q0:q0 + qn]
            if mask is None:
                m_blk = mx_ref[hh, :, q0:q0 + qn]
            else:
                st = jnp.where(mask, st, NEG_INF)
                m_blk = _col_max(st)
            m_old = m_scr[hh, :, q0:q0 + qn]
            m_new = jnp.maximum(m_old, m_blk)
            p = jnp.exp2(st - m_new).astype(jnp.bfloat16)
            vt_t = vt_ref[0, hh * V_DIM:(hh + 1) * V_DIM, pl.ds(start, tk)]
            vt_aug = jnp.concatenate([vt_t, ones_k], axis=0)
            acc_scr[hh, :, q0:q0 + qn] = (jnp.exp2(m_old - m_new) * acc_scr[hh, :, q0:q0 + qn]
                                          + _dot(vt_aug, p))
            m_scr[hh, :, q0:q0 + qn] = m_new

    @pl.when(qi == 0)
    def _():
        scores(0, sa_scr, ma_scr)

    def body(jj, carry):
        t0 = 2 * jj
        for hh in range(HEADS_PER_STEP):
            for q0 in range(0, tq, Q_SLICE):
                scores(t0 + 1, sb_scr, mb_scr, heads=(hh,), q0=q0, qn=Q_SLICE)
                softmax_pv(t0, sa_scr, ma_scr, heads=(hh,), q0=q0, qn=Q_SLICE)
        for hh in range(HEADS_PER_STEP):
            for q0 in range(0, tq, Q_SLICE):
                scores(t0 + 2, sa_scr, ma_scr, heads=(hh,), q0=q0, qn=Q_SLICE)
                softmax_pv(t0 + 1, sb_scr, mb_scr, heads=(hh,), q0=q0, qn=Q_SLICE)
        return carry

    lax.fori_loop(0, qi, body, 0)

    d1 = 2 * qi
    key_chunk = lax.broadcasted_iota(jnp.int32, (tk, Q_SLICE), 0) // CHUNK
    qry_local = lax.broadcasted_iota(jnp.int32, (tk, Q_SLICE), 1)

    def chunk_mask(q_off):
        return None if q_off >= tk else key_chunk <= (qry_local + q_off) // CHUNK

    for hh in range(HEADS_PER_STEP):
        for q0 in range(0, tq, Q_SLICE):
            if q0 >= tk:
                scores(d1 + 1, sb_scr, mb_scr, heads=(hh,), q0=q0, qn=Q_SLICE)
            softmax_pv(d1, sa_scr, ma_scr, mask=chunk_mask(q0), heads=(hh,), q0=q0, qn=Q_SLICE)
    q_next = jnp.minimum(qi + 1, n_q - 1)
    for hh in range(HEADS_PER_STEP):
        for q0 in range(0, tq, Q_SLICE):
            scores(0, sa_scr, ma_scr, q_idx=q_next, heads=(hh,), q0=q0, qn=Q_SLICE)
            if q0 >= tk:
                softmax_pv(d1 + 1, sb_scr, mb_scr, mask=chunk_mask(q0 - tk), heads=(hh,),
                           q0=q0, qn=Q_SLICE)
    ms_p, accs_p = _prefix_state([q_rows(hh, qi, 0, tq) for hh in range(HEADS_PER_STEP)],
                                 kp_ref, vtp_ref, n_prefix_valid)
    accs = []
    for hh in range(HEADS_PER_STEP):
        m_run = m_scr[hh]
        m_all = jnp.maximum(m_run, ms_p[hh])
        accs.append(jnp.exp2(m_run - m_all) * acc_scr[hh] + jnp.exp2(ms_p[hh] - m_all) * accs_p[hh])
    _store_normalised(accs, o_ref)


def _attn_dense_call(q, kp, vtp, *, n_prefix_valid):
    bsz, _, tq, _ = q.shape
    n_prefix = kp.shape[2]
    return pl.pallas_call(
        functools.partial(_attn_dense_kernel, n_valid=n_prefix_valid),
        grid=(bsz,),
        in_specs=[
            pl.BlockSpec((1, N_HEADS, tq, HEAD_PAD), lambda b: (b, 0, 0, 0)),
            pl.BlockSpec((1, N_HEADS, n_prefix, HEAD_PAD), lambda b: (b, 0, 0, 0)),
            pl.BlockSpec((1, ATTN_W, n_prefix), lambda b: (b, 0, 0)),
        ],
        out_specs=pl.BlockSpec((1, tq, ATTN_W), lambda b: (b, 0, 0)),
        out_shape=jax.ShapeDtypeStruct((bsz, tq, ATTN_W), jnp.bfloat16),
        compiler_params=pltpu.CompilerParams(
            dimension_semantics=("arbitrary",), vmem_limit_bytes=VMEM_LIMIT_BYTES),
        name="attn_dense",
    )(q, kp, vtp)


def _attn_causal_call(q, kp, vtp, k, vt, *, tq, n_prefix_valid):
    bsz, _, seq, _ = q.shape
    n_prefix = kp.shape[2]
    tk = tq // 2
    f32 = jnp.float32
    return pl.pallas_call(
        functools.partial(_attn_causal_kernel, tq=tq, tk=tk, n_prefix_valid=n_prefix_valid),
        grid=(bsz, N_HEADS // HEADS_PER_STEP, seq // tq),
        in_specs=[
            pl.BlockSpec((1, HEADS_PER_STEP, seq, HEAD_PAD), lambda b, hp, i: (b, hp, 0, 0)),
            pl.BlockSpec((1, HEADS_PER_STEP, n_prefix, HEAD_PAD), lambda b, hp, i: (0, hp, 0, 0)),
            pl.BlockSpec((1, HEADS_PER_STEP * V_DIM, n_prefix), lambda b, hp, i: (0, hp, 0)),
            pl.BlockSpec((1, HEADS_PER_STEP, seq, HEAD_PAD), lambda b, hp, i: (b, hp, 0, 0)),
            pl.BlockSpec((1, HEADS_PER_STEP * V_DIM, seq), lambda b, hp, i: (b, hp, 0)),
        ],
        out_specs=pl.BlockSpec((1, tq, HEADS_PER_STEP * V_DIM), lambda b, hp, i: (b, i, hp)),
        out_shape=jax.ShapeDtypeStruct((bsz, seq, ATTN_W), jnp.bfloat16),
        scratch_shapes=[pltpu.VMEM((HEADS_PER_STEP, tk, tq), f32),
                        pltpu.VMEM((HEADS_PER_STEP, tk, tq), f32),
                        pltpu.VMEM((HEADS_PER_STEP, 1, tq), f32),
                        pltpu.VMEM((HEADS_PER_STEP, 1, tq), f32),
                        pltpu.VMEM((HEADS_PER_STEP, 1, tq), f32),
                        pltpu.VMEM((HEADS_PER_STEP, V_AUG, tq), f32)],
        compiler_params=pltpu.CompilerParams(
            dimension_semantics=("arbitrary", "arbitrary", "arbitrary"),
            vmem_limit_bytes=VMEM_LIMIT_BYTES),
        name="attn_causal",
    )(q, kp, vtp, k, vt)


def _out_mlp_kernel(o_ref, y_ref, x_ref, g_attn_ref, w_out_ref, g_mlp_ref, w_up_ref, w_down_ref,
                    g_fin_ref, out_ref, hn_scr, *, final):
    c = pl.program_id(1)

    @pl.when(c == 0)
    def _():
        on = _rms(o_ref[...].astype(jnp.float32), g_attn_ref[...]).astype(jnp.bfloat16)
        h = (x_ref[...] + _dot(on, w_out_ref[0:ATTN_W, :])
             + _dot(y_ref[...], w_out_ref[ATTN_W:ATTN_W + LRU_W, :]))
        hn_scr[...] = _rms(h, g_mlp_ref[...]).astype(jnp.bfloat16)
        out_ref[...] = h

    u = _dot(hn_scr[...], w_up_ref[...])
    u = jnp.square(jnp.maximum(u, 0.0)).astype(jnp.bfloat16)
    out_ref[...] += _dot(u, w_down_ref[...])

    if final:
        @pl.when(c == pl.num_programs(1) - 1)
        def _():
            out_ref[...] = _rms(out_ref[...], g_fin_ref[...])


def _out_mlp_call(o, y, x, lw, g_fin, *, tm, final):
    rows = x.shape[0]
    const = lambda r, c: (0, 0)
    return pl.pallas_call(
        functools.partial(_out_mlp_kernel, final=final),
        grid=(rows // tm, D_FF // F_CHUNK),
        in_specs=[
            pl.BlockSpec((tm, ATTN_W), lambda r, c: (r, 0)),
            pl.BlockSpec((tm, LRU_W), lambda r, c: (r, 0)),
            pl.BlockSpec((tm, D_MODEL), lambda r, c: (r, 0)),
            pl.BlockSpec((1, ATTN_W), const),
            pl.BlockSpec((D_MODEL, D_MODEL), const),
            pl.BlockSpec((1, D_MODEL), const),
            pl.BlockSpec((D_MODEL, F_CHUNK), lambda r, c: (0, c)),
            pl.BlockSpec((F_CHUNK, D_MODEL), lambda r, c: (c, 0)),
            pl.BlockSpec((1, D_MODEL), const),
        ],
        out_specs=pl.BlockSpec((tm, D_MODEL), lambda r, c: (r, 0)),
        out_shape=jax.ShapeDtypeStruct((rows, D_MODEL), jnp.float32),
        scratch_shapes=[pltpu.VMEM((tm, D_MODEL), jnp.bfloat16)],
        compiler_params=pltpu.CompilerParams(
            dimension_semantics=("arbitrary", "arbitrary"), vmem_limit_bytes=VMEM_LIMIT_BYTES),
        name="out_mlp",
    )(o, y, x, lw["g_attn"], lw["w_out"], lw["g_mlp"], lw["w_up"], lw["w_down"], g_fin)


def _rope_tables(pos, rope_lane0, pass_lanes, scale):
    half = QK_ROPE // 2
    lane = jnp.arange(LANES, dtype=jnp.int32)
    rel = lane - rope_lane0
    in_rope = (rel >= 0) & (rel < QK_ROPE)
    freq_idx = jnp.where(in_rope, rel % half, 0).astype(jnp.float32)
    inv = ROPE_THETA ** (-(2.0 * freq_idx) / QK_ROPE)
    ang = pos.astype(jnp.float32)[:, None] * inv[None, :]
    cos, sin = jnp.cos(ang), jnp.sin(ang)
    passthrough = (lane < pass_lanes).astype(jnp.float32)[None, :]
    tab_c = jnp.where(in_rope[None, :], cos, passthrough)
    tab_m = jnp.where((in_rope & (rel < half))[None, :], -sin, 0.0)
    tab_p = jnp.where((in_rope & (rel >= half))[None, :], sin, 0.0)
    return jnp.stack([tab_c, tab_m, tab_p]) * scale


def _key_tables(pos):
    return _rope_tables(pos, 0, 0, 1.0)


def _query_tables(pos):
    return _rope_tables(pos, QK_NOPE, QK_NOPE, SM_SCALE * LOG2_E)


def _block_diag_gates(w_a, w_x):
    per_half = LRU_BLOCKS // 2
    halves = []
    for s in range(2):
        bd_a = jax.scipy.linalg.block_diag(*[w_a[s * per_half + i] for i in range(per_half)])
        bd_x = jax.scipy.linalg.block_diag(*[w_x[s * per_half + i] for i in range(per_half)])
        halves.append(jnp.concatenate([bd_a, bd_x], axis=1))
    return jnp.stack(halves).astype(jnp.bfloat16)


def _layer_weights(l, norm_mix_g, w_in, q_norm_g, w_uq, kv_norm_g, w_ukv, conv_w, conv_b,
                   w_gate_a, b_gate_a, w_gate_x, b_gate_x, lru_lambda, attn_out_g, lru_out_g,
                   w_out, norm_mlp_g, w_up, w_down):
    bf = jnp.bfloat16
    row = lambda v: v.reshape(1, -1).astype(jnp.float32)
    w_in_p = jnp.concatenate(
        [w_in[l][:, :COL_KR + QK_ROPE], jnp.zeros((D_MODEL, ROPE_PAD - QK_ROPE), w_in.dtype),
         w_in[l][:, COL_KR + QK_ROPE:]], axis=1).astype(bf)
    w_uq_p = jnp.pad(w_uq[l].reshape(Q_LORA, N_HEADS, QK_NOPE + QK_ROPE),
                     ((0, 0), (0, 0), (0, HEAD_PAD - QK_NOPE - QK_ROPE))).reshape(Q_LORA, -1).astype(bf)
    w_ukv_h = w_ukv[l].reshape(KV_LORA, N_HEADS, QK_NOPE + V_DIM)
    w_uk_p = jnp.pad(w_ukv_h[:, :, :QK_NOPE],
                     ((0, 0), (0, 0), (0, HEAD_PAD - QK_NOPE))).reshape(KV_LORA, -1).astype(bf)
    w_uvt = w_ukv_h[:, :, QK_NOPE:].reshape(KV_LORA, ATTN_W).T.astype(bf)
    return {
        "g_mix": row(norm_mix_g[l]), "w_in": w_in_p, "g_q": row(q_norm_g[l]), "w_uq": w_uq_p,
        "g_kv": row(kv_norm_g[l]), "w_uk": w_uk_p, "w_uvt": w_uvt,
        "conv_w": conv_w[l].astype(jnp.float32), "conv_b": row(conv_b[l]),
        "wg": _block_diag_gates(w_gate_a[l], w_gate_x[l]),
        "bg": jnp.stack([b_gate_a[l], b_gate_x[l]]).astype(jnp.float32),
        "lam": row(lru_lambda[l]), "g_lru": row(lru_out_g[l]), "g_attn": row(attn_out_g[l]),
        "w_out": w_out[l].astype(bf), "g_mlp": row(norm_mlp_g[l]),
        "w_up": w_up[l].astype(bf), "w_down": w_down[l].astype(bf),
    }


def _pad_buf(buf):
    return jnp.pad(buf, ((0, 0), (SUBLANES - (CONV_W - 1), 0), (0, 0)))


def _pad_axis(a, axis, size):
    pad = [(0, 0)] * a.ndim
    pad[axis] = (0, size - a.shape[axis])
    return jnp.pad(a, pad)


def kernel(x_prompt, x_sample, cache_ckv, cache_kpe, state_lru_h, state_conv, meta_tokens,
           norm_mix_g, w_in, q_norm_g, w_uq, kv_norm_g, w_ukv, conv_w, conv_b,
           w_gate_a, b_gate_a, w_gate_x, b_gate_x, lru_lambda, attn_out_g, lru_out_g,
           w_out, norm_mlp_g, w_up, w_down, final_norm_g):
    b_p, seq, _ = x_prompt.shape
    b_s, dec_seq, _ = x_sample.shape
    past_len = cache_ckv.shape[2]
    assert (past_len + dec_seq - 1) // CHUNK == past_len // CHUNK
    tm_p = 512
    tq_p = 1024
    tm_o = 1024
    assert seq % tm_p == 0 and seq % tq_p == 0 and (tq_p // 2) % CHUNK == 0

    meta_pos = jnp.arange(-N_META, 0, dtype=jnp.int32)
    prompt_pos = jnp.arange(seq, dtype=jnp.int32)
    sample_pos = past_len + jnp.arange(dec_seq, dtype=jnp.int32)
    tabk_m, tabk_p, tabk_s = _key_tables(meta_pos), _key_tables(prompt_pos), _key_tables(sample_pos)
    tabq_m, tabq_p, tabq_s = _query_tables(meta_pos), _query_tables(prompt_pos), _query_tables(sample_pos)
    g_fin = final_norm_g.reshape(1, -1).astype(jnp.float32)

    cache_kpe_pad = _pad_axis(cache_kpe, 3, ROPE_PAD)
    tq_small = LANES

    h_meta = meta_tokens[None].astype(jnp.float32)
    h_p, h_s = x_prompt, x_sample
    zero_h = jnp.zeros((1, 1, LRU_W), jnp.float32)
    zero_buf = jnp.zeros((1, SUBLANES, LRU_W), jnp.float32)
    outs = {name: [] for name in ("ckv_p", "kpe_p", "lru_p", "conv_p", "ckv_s", "kpe_s", "lru_s", "conv_s")}
    for l in range(DEPTH):
        lw = _layer_weights(l, norm_mix_g, w_in, q_norm_g, w_uq, kv_norm_g, w_ukv, conv_w, conv_b,
                            w_gate_a, b_gate_a, w_gate_x, b_gate_x, lru_lambda, attn_out_g,
                            lru_out_g, w_out, norm_mlp_g, w_up, w_down)
        last = l + 1 == DEPTH
        mq, mk, mvt, m_ckv, m_kpe, m_y, m_h, m_buf = _mixer_call(
            h_meta, zero_h, zero_buf, tabq_m, tabk_m, lw, tm=N_META, shared_state=True)
        pq, pk, pvt, p_ckv, p_kpe, p_y, p_h, p_buf = _mixer_call(
            h_p, m_h, _pad_buf(m_buf), tabq_p, tabk_p, lw, tm=tm_p, shared_state=True)
        mk_p, mvt_p = _pad_axis(mk, 2, LANES), _pad_axis(mvt, 2, LANES)
        p_o = _attn_causal_call(pq, mk_p, mvt_p, pk, pvt, tq=tq_p, n_prefix_valid=N_META)
        h_p = _out_mlp_call(p_o.reshape(b_p * seq, ATTN_W), p_y.reshape(b_p * seq, LRU_W),
                            h_p.reshape(b_p * seq, D_MODEL), lw, g_fin, tm=tm_o,
                            final=last).reshape(b_p, seq, D_MODEL)
        sq, sk, svt, s_ckv, s_kpe, s_y, s_h, s_buf = _mixer_call(
            h_s, state_lru_h[l][:, None, :], _pad_buf(state_conv[l]), tabq_s, tabk_s, lw,
            tm=dec_seq, shared_state=False)
        s_o = _attn_sample_call(sq, m_ckv, cache_ckv, s_ckv, m_kpe, cache_kpe_pad, s_kpe, lw,
                                layer=l)[:, :dec_seq]
        h_s = _out_mlp_call(s_o.reshape(b_s * dec_seq, ATTN_W), s_y.reshape(b_s * dec_seq, LRU_W),
                            h_s.reshape(b_s * dec_seq, D_MODEL), lw, g_fin, tm=b_s * dec_seq,
                            final=last).reshape(b_s, dec_seq, D_MODEL)
        if not last:
            m_o = _attn_dense_call(_pad_axis(mq, 2, tq_small), mk_p, mvt_p,
                                   n_prefix_valid=N_META)[:, :N_META]
            h_meta = _out_mlp_call(m_o.reshape(N_META, ATTN_W), m_y.reshape(N_META, LRU_W),
                                   h_meta.reshape(N_META, D_MODEL), lw, g_fin, tm=N_META,
                                   final=False).reshape(1, N_META, D_MODEL)
        outs["ckv_p"].append(p_ckv); outs["kpe_p"].append(p_kpe[:, :, :QK_ROPE])
        outs["lru_p"].append(p_h[:, 0]); outs["conv_p"].append(p_buf)
        outs["ckv_s"].append(s_ckv); outs["kpe_s"].append(s_kpe[:, :, :QK_ROPE])
        outs["lru_s"].append(s_h[:, 0]); outs["conv_s"].append(s_buf)

    return (h_p, h_s, jnp.stack(outs["ckv_p"]), jnp.stack(outs["kpe_p"]), jnp.stack(outs["lru_p"]),
            jnp.stack(outs["conv_p"]), jnp.stack(outs["ckv_s"]), jnp.stack(outs["kpe_s"]),
            jnp.stack(outs["lru_s"]), jnp.stack(outs["conv_s"]))
```

```python
import functools

import jax
import jax.numpy as jnp
from jax import lax
from jax.experimental import pallas as pl
from jax.experimental.pallas import tpu as pltpu

D_MODEL = 1024
DEPTH = 2
CHUNK = 64
N_META = 16
ATTN_W = 512
LRU_W = 512
V_DIM = 64
N_HEADS = 8
QK_NOPE = 64
QK_ROPE = 32
Q_LORA = 768
KV_LORA = 256
LRU_BLOCKS = 8
LRU_BW = LRU_W // LRU_BLOCKS
LRU_C = 8.0
CONV_W = 4
D_FF = 4 * D_MODEL
ROPE_THETA = 10000.0
EPS = 1e-6
SM_SCALE = (QK_NOPE + QK_ROPE) ** -0.5
NEG_INF = -1e30
LOG2_E = 1.4426950408889634

LANES = 128
SUBLANES = 8
HEAD_PAD = LANES
VMEM_LIMIT_BYTES = 56 * 1024 * 1024

ROPE_PAD = LANES
COL_CQ = 0
COL_CKV = Q_LORA
COL_KR = Q_LORA + KV_LORA
COL_XR = COL_KR + ROPE_PAD
COL_GR = COL_XR + LRU_W
IN_W_PAD = COL_GR + LRU_W

HEADS_PER_STEP = 2
ONES_ROWS = 16
V_AUG = V_DIM + ONES_ROWS
F_CHUNK = 1024
Q_SLICE = 256


def _rms(x, g):
    return x * lax.rsqrt(jnp.mean(x * x, axis=-1, keepdims=True) + EPS) * g


def _rope_lanes(x, tab_ref):
    half = QK_ROPE // 2
    return (x * tab_ref[0] + pltpu.roll(x, LANES - half, 1) * tab_ref[1]
            + pltpu.roll(x, half, 1) * tab_ref[2])


def _dot(a, b):
    return jnp.dot(a, b, preferred_element_type=jnp.float32)


def _dot_nt(a, b):
    return lax.dot_general(a, b, (((1,), (1,)), ((), ())), preferred_element_type=jnp.float32)


def _mixer_kernel(x_ref, h0_ref, buf0_ref, tabq_ref, tabk_ref, g_mix_ref, w_in_ref, g_q_ref,
                  w_uq_ref, g_kv_ref, w_uk_ref, w_uvt_ref, conv_w_ref, conv_b_ref, wg_ref, bg_ref,
                  lam_ref, g_lru_ref,
                  q_ref, k_ref, vt_ref, ckv_ref, kpe_ref, y_ref, hlast_ref, buf_ref,
                  tail_scr, hcar_scr, *, tm):
    t = pl.program_id(1)

    @pl.when(t == 0)
    def _():
        tail_scr[...] = buf0_ref[0]
        hcar_scr[...] = h0_ref[0]

    x = x_ref[0]
    xn = _rms(x, g_mix_ref[...]).astype(jnp.bfloat16)
    z = _dot(xn, w_in_ref[...])

    cqn = _rms(z[:, COL_CQ:COL_CQ + Q_LORA], g_q_ref[...]).astype(jnp.bfloat16)
    q = _dot(cqn, w_uq_ref[...])
    for h in range(N_HEADS):
        qh = q[:, h * HEAD_PAD:(h + 1) * HEAD_PAD]
        q_ref[0, h] = _rope_lanes(qh, tabq_ref).astype(jnp.bfloat16)

    ckv = _rms(z[:, COL_CKV:COL_CKV + KV_LORA], g_kv_ref[...])
    ckv_ref[0] = ckv
    ckv_b = ckv.astype(jnp.bfloat16)
    kpe = _rope_lanes(z[:, COL_KR:COL_KR + ROPE_PAD], tabk_ref)
    kpe_ref[0] = kpe
    kpe_shift = pltpu.roll(kpe, QK_NOPE, 1)
    kk = _dot(ckv_b, w_uk_ref[...])
    for h in range(N_HEADS):
        k_ref[0, h] = (kk[:, h * HEAD_PAD:(h + 1) * HEAD_PAD] + kpe_shift).astype(jnp.bfloat16)
    vt_ref[0] = _dot_nt(w_uvt_ref[...], ckv_b).astype(jnp.bfloat16)

    x_r = z[:, COL_XR:COL_XR + LRU_W]
    g_r = z[:, COL_GR:COL_GR + LRU_W]
    n_slabs = tm // SUBLANES
    rows8 = lax.broadcasted_iota(jnp.int32, (SUBLANES, LRU_W), 0)
    slabs = [tail_scr[...]] + [x_r[g * SUBLANES:(g + 1) * SUBLANES, :] for g in range(n_slabs)]
    xc = conv_b_ref[...] + x_r * conv_w_ref[CONV_W - 1:CONV_W, :]
    for back in range(1, CONV_W):
        rolled = [pltpu.roll(s, back, 0) for s in slabs]
        shifted = jnp.concatenate(
            [jnp.where(rows8 < back, rolled[g], rolled[g + 1]) for g in range(n_slabs)], axis=0)
        xc = xc + shifted * conv_w_ref[CONV_W - 1 - back:CONV_W - back, :]
    tail_scr[...] = slabs[-1]
    buf_ref[0] = tail_scr[SUBLANES - (CONV_W - 1):SUBLANES, :]

    half_w = LRU_W // 2
    gates = [_dot(xc[:, s * half_w:(s + 1) * half_w].astype(jnp.bfloat16), wg_ref[s]) for s in range(2)]
    pre_a = jnp.concatenate([gates[0][:, :half_w], gates[1][:, :half_w]], axis=1)
    pre_x = jnp.concatenate([gates[0][:, half_w:], gates[1][:, half_w:]], axis=1)
    r = jax.nn.sigmoid(pre_a + bg_ref[0:1, :])
    gi = jax.nn.sigmoid(pre_x + bg_ref[1:2, :])
    neg_lam = -lam_ref[...]
    softplus = jnp.maximum(neg_lam, 0.0) + jnp.log1p(jnp.exp(-jnp.abs(neg_lam)))
    log_a = (-LRU_C) * r * softplus
    a = jnp.exp(log_a)
    one_m_a2 = -jnp.tanh(log_a) * (a * a + 1.0)
    root = jnp.where(one_m_a2 > 0.0, one_m_a2 * lax.rsqrt(one_m_a2), 0.0)
    b = root * (gi * xc)

    carry = jnp.broadcast_to(hcar_scr[...], (SUBLANES, LRU_W))
    h_slabs = []
    for g in range(n_slabs):
        a_g = a[g * SUBLANES:(g + 1) * SUBLANES, :]
        b_g = b[g * SUBLANES:(g + 1) * SUBLANES, :]
        shift = 1
        while shift < SUBLANES:
            keep = rows8 >= shift
            a_prev = jnp.where(keep, pltpu.roll(a_g, shift, 0), 1.0)
            b_prev = jnp.where(keep, pltpu.roll(b_g, shift, 0), 0.0)
            b_g = a_g * b_prev + b_g
            a_g = a_g * a_prev
            shift *= 2
        h_g = a_g * carry + b_g
        h_slabs.append(h_g)
        carry = jnp.broadcast_to(h_g[SUBLANES - 1:SUBLANES, :], (SUBLANES, LRU_W))
    hh = jnp.concatenate(h_slabs, axis=0)
    h_last = carry[0:1, :]
    hcar_scr[...] = h_last
    hlast_ref[0] = h_last
    y = hh * jax.nn.gelu(g_r)
    y_ref[0] = _rms(y, g_lru_ref[...]).astype(jnp.bfloat16)


def _mixer_call(x, h0, buf0p, tabq, tabk, lw, *, tm, shared_state):
    bsz, seq, _ = x.shape
    nt = seq // tm
    const2 = lambda b, t: (0, 0)
    const3 = lambda b, t: (0, 0, 0)
    state_idx = (lambda b, t: (0, 0, 0)) if shared_state else (lambda b, t: (b, 0, 0))
    in_specs = [
        pl.BlockSpec((1, tm, D_MODEL), lambda b, t: (b, t, 0)),
        pl.BlockSpec((1, 1, LRU_W), state_idx),
        pl.BlockSpec((1, SUBLANES, LRU_W), state_idx),
        pl.BlockSpec((3, tm, LANES), lambda b, t: (0, t, 0)),
        pl.BlockSpec((3, tm, LANES), lambda b, t: (0, t, 0)),
        pl.BlockSpec((1, D_MODEL), const2),
        pl.BlockSpec((D_MODEL, IN_W_PAD), const2),
        pl.BlockSpec((1, Q_LORA), const2),
        pl.BlockSpec((Q_LORA, N_HEADS * HEAD_PAD), const2),
        pl.BlockSpec((1, KV_LORA), const2),
        pl.BlockSpec((KV_LORA, N_HEADS * HEAD_PAD), const2),
        pl.BlockSpec((ATTN_W, KV_LORA), const2),
        pl.BlockSpec((CONV_W, LRU_W), const2),
        pl.BlockSpec((1, LRU_W), const2),
        pl.BlockSpec((2, LRU_W // 2, LRU_W), const3),
        pl.BlockSpec((2, LRU_W), const2),
        pl.BlockSpec((1, LRU_W), const2),
        pl.BlockSpec((1, LRU_W), const2),
    ]
    out_shape = [
        jax.ShapeDtypeStruct((bsz, N_HEADS, seq, HEAD_PAD), jnp.bfloat16),
        jax.ShapeDtypeStruct((bsz, N_HEADS, seq, HEAD_PAD), jnp.bfloat16),
        jax.ShapeDtypeStruct((bsz, ATTN_W, seq), jnp.bfloat16),
        jax.ShapeDtypeStruct((bsz, seq, KV_LORA), jnp.float32),
        jax.ShapeDtypeStruct((bsz, seq, ROPE_PAD), jnp.float32),
        jax.ShapeDtypeStruct((bsz, seq, LRU_W), jnp.bfloat16),
        jax.ShapeDtypeStruct((bsz, 1, LRU_W), jnp.float32),
        jax.ShapeDtypeStruct((bsz, CONV_W - 1, LRU_W), jnp.float32),
    ]
    out_specs = [
        pl.BlockSpec((1, N_HEADS, tm, HEAD_PAD), lambda b, t: (b, 0, t, 0)),
        pl.BlockSpec((1, N_HEADS, tm, HEAD_PAD), lambda b, t: (b, 0, t, 0)),
        pl.BlockSpec((1, ATTN_W, tm), lambda b, t: (b, 0, t)),
        pl.BlockSpec((1, tm, KV_LORA), lambda b, t: (b, t, 0)),
        pl.BlockSpec((1, tm, ROPE_PAD), lambda b, t: (b, t, 0)),
        pl.BlockSpec((1, tm, LRU_W), lambda b, t: (b, t, 0)),
        pl.BlockSpec((1, 1, LRU_W), lambda b, t: (b, 0, 0)),
        pl.BlockSpec((1, CONV_W - 1, LRU_W), lambda b, t: (b, 0, 0)),
    ]
    return pl.pallas_call(
        functools.partial(_mixer_kernel, tm=tm),
        grid=(bsz, nt),
        in_specs=in_specs,
        out_specs=out_specs,
        out_shape=out_shape,
        scratch_shapes=[pltpu.VMEM((SUBLANES, LRU_W), jnp.float32),
                        pltpu.VMEM((1, LRU_W), jnp.float32)],
        compiler_params=pltpu.CompilerParams(
            dimension_semantics=("arbitrary", "arbitrary"), vmem_limit_bytes=VMEM_LIMIT_BYTES),
        name="mixer",
    )(x, h0, buf0p, tabq, tabk, lw["g_mix"], lw["w_in"], lw["g_q"], lw["w_uq"], lw["g_kv"],
      lw["w_uk"], lw["w_uvt"], lw["conv_w"], lw["conv_b"], lw["wg"], lw["bg"], lw["lam"], lw["g_lru"])


def _attn_sample_kernel(q_ref, ckv_m_ref, ckv_c_ref, ckv_s_ref, kpe_m_ref, kpe_c_ref, kpe_s_ref,
                        w_uk_ref, w_uvt_ref, o_ref, *, n_keys_pad):
    ckv_parts = [ckv_m_ref[0], ckv_c_ref[0, 0], ckv_s_ref[0]]
    kpe_parts = [kpe_m_ref[0], kpe_c_ref[0, 0], kpe_s_ref[0]]
    n_valid = sum(p.shape[0] for p in ckv_parts)
    pad = n_keys_pad - n_valid
    ckv = jnp.concatenate(ckv_parts + [jnp.zeros((pad, KV_LORA), jnp.float32)], axis=0)
    kpe = jnp.concatenate(kpe_parts + [jnp.zeros((pad, ROPE_PAD), jnp.float32)], axis=0)
    ckv_b = ckv.astype(jnp.bfloat16)
    kpe_shift = pltpu.roll(kpe, QK_NOPE, 1)
    kk = _dot(ckv_b, w_uk_ref[...])
    vt = _dot_nt(w_uvt_ref[...], ckv_b).astype(jnp.bfloat16)
    n_q = q_ref.shape[2]
    q_pad = jnp.zeros((LANES - n_q, HEAD_PAD), jnp.bfloat16)
    q_tiles = [jnp.concatenate([q_ref[0, h], q_pad], axis=0) for h in range(N_HEADS)]
    k_tiles = [(kk[:, h * HEAD_PAD:(h + 1) * HEAD_PAD] + kpe_shift).astype(jnp.bfloat16)
               for h in range(N_HEADS)]
    vt_tiles = [vt[h * V_DIM:(h + 1) * V_DIM, :] for h in range(N_HEADS)]
    _, accs = _dense_state(q_tiles, k_tiles, vt_tiles, n_valid)
    _store_normalised(accs, o_ref)


def _attn_sample_call(q, ckv_m, ckv_cache, ckv_s, kpe_m, kpe_cache, kpe_s, lw, *, layer):
    bsz, _, dec, _ = q.shape
    past = ckv_cache.shape[2]
    n_meta = ckv_m.shape[1]
    n_keys_pad = -(-(n_meta + past + dec) // LANES) * LANES
    return pl.pallas_call(
        functools.partial(_attn_sample_kernel, n_keys_pad=n_keys_pad),
        grid=(bsz,),
        in_specs=[
            pl.BlockSpec((1, N_HEADS, dec, HEAD_PAD), lambda b: (b, 0, 0, 0)),
            pl.BlockSpec((1, n_meta, KV_LORA), lambda b: (0, 0, 0)),
            pl.BlockSpec((1, 1, past, KV_LORA), lambda b: (layer, b, 0, 0)),
            pl.BlockSpec((1, dec, KV_LORA), lambda b: (b, 0, 0)),
            pl.BlockSpec((1, n_meta, ROPE_PAD), lambda b: (0, 0, 0)),
            pl.BlockSpec((1, 1, past, ROPE_PAD), lambda b: (layer, b, 0, 0)),
            pl.BlockSpec((1, dec, ROPE_PAD), lambda b: (b, 0, 0)),
            pl.BlockSpec((KV_LORA, N_HEADS * HEAD_PAD), lambda b: (0, 0)),
            pl.BlockSpec((ATTN_W, KV_LORA), lambda b: (0, 0)),
        ],
        out_specs=pl.BlockSpec((1, LANES, ATTN_W), lambda b: (b, 0, 0)),
        out_shape=jax.ShapeDtypeStruct((bsz, LANES, ATTN_W), jnp.bfloat16),
        compiler_params=pltpu.CompilerParams(
            dimension_semantics=("arbitrary",), vmem_limit_bytes=VMEM_LIMIT_BYTES),
        name="attn_sample",
    )(q, ckv_m, ckv_cache, ckv_s, kpe_m, kpe_cache, kpe_s, lw["w_uk"], lw["w_uvt"])


def _col_max(st):
    rows, cols = st.shape
    slab = SUBLANES
    if rows % slab == 0 and rows > slab:
        st = jnp.max(st.reshape(rows // slab, slab, cols), axis=0)
    return jnp.max(st, axis=0, keepdims=True)


def _dense_state(q_tiles, k_tiles, vt_tiles, n_valid):
    n_keys = k_tiles[0].shape[0]
    tq = q_tiles[0].shape[0]
    n_used = -(-n_valid // ONES_ROWS) * ONES_ROWS
    ones_p = jnp.ones((ONES_ROWS, n_keys), jnp.bfloat16)
    ms, accs = [], []
    for q_t, k_t, vt_t in zip(q_tiles, k_tiles, vt_tiles):
        st = _dot_nt(k_t[:n_used], q_t)
        if n_valid < n_used:
            key_idx = lax.broadcasted_iota(jnp.int32, (n_used, tq), 0)
            st = jnp.where(key_idx < n_valid, st, NEG_INF)
        m = _col_max(st)
        p = jnp.exp2(st - m).astype(jnp.bfloat16)
        if n_used < n_keys:
            p = jnp.concatenate([p, jnp.zeros((n_keys - n_used, tq), jnp.bfloat16)], axis=0)
        ms.append(m)
        accs.append(_dot(jnp.concatenate([vt_t, ones_p], axis=0), p))
    return ms, accs


def _prefix_state(q_tiles, kp_ref, vtp_ref, n_valid):
    heads = range(len(q_tiles))
    return _dense_state(q_tiles, [kp_ref[0, hh] for hh in heads],
                        [vtp_ref[0, hh * V_DIM:(hh + 1) * V_DIM, :] for hh in heads], n_valid)


def _store_normalised(accs, o_ref):
    outs = [acc[:V_DIM, :] / acc[V_DIM:V_DIM + 1, :] for acc in accs]
    o_ref[0] = jnp.concatenate(outs, axis=0).T.astype(jnp.bfloat16)


def _attn_dense_kernel(q_ref, kp_ref, vtp_ref, o_ref, *, n_valid):
    q_tiles = [q_ref[0, hh] for hh in range(q_ref.shape[1])]
    _, accs = _prefix_state(q_tiles, kp_ref, vtp_ref, n_valid)
    _store_normalised(accs, o_ref)


def _attn_causal_kernel(q_ref, kp_ref, vtp_ref, k_ref, vt_ref, o_ref,
                        sa_scr, sb_scr, ma_scr, mb_scr, m_scr, acc_scr,
                        *, tq, tk, n_prefix_valid):
    qi = pl.program_id(2)
    n_q = pl.num_programs(2)

    def q_rows(hh, q_idx, q0, qn):
        return q_ref[0, hh, pl.ds(pl.multiple_of(q_idx * tq + q0, tk), qn), :]

    m_scr[...] = jnp.full(m_scr.shape, NEG_INF, jnp.float32)
    acc_scr[...] = jnp.zeros(acc_scr.shape, jnp.float32)
    ones_k = jnp.ones((ONES_ROWS, tk), jnp.bfloat16)

    def scores(tile, s_ref, mx_ref, q_idx=qi, q0=0, qn=Q_SLICE, kn=tk, heads=range(HEADS_PER_STEP)):
        start = pl.multiple_of(tile * tk, tk)
        for hh in heads:
            st = _dot_nt(k_ref[0, hh, pl.ds(start, kn), :], q_rows(hh, q_idx, q0, qn))
            s_ref[hh, q0 // Q_SLICE, 0:kn, :] = st
            mx_ref[hh, :, q0:q0 + qn] = _col_max(st)

    def softmax_pv(tile, s_ref, mx_ref, mask=None, q0=0, qn=Q_SLICE, kn=tk, heads=range(HEADS_PER_STEP)):
        start = pl.multiple_of(tile * tk, tk)
        for hh in heads:
            st = s_ref[hh, q0 // Q_SLICE, 0:kn, :]
            if mask is None:
                m_blk = mx_ref[hh, :, q0:q0 + qn]
            else:
                st = jnp.where(mask, st, NEG_INF)
                m_blk = _col_max(st)
            m_old = m_scr[hh, :, q0:q0 + qn]
            m_new = jnp.maximum(m_old, m_blk)
            p = jnp.exp2(st - m_new).astype(jnp.bfloat16)
            vt_t = vt_ref[0, hh * V_DIM:(hh + 1) * V_DIM, pl.ds(start, kn)]
            vt_aug = jnp.concatenate([vt_t, ones_k[:, :kn]], axis=0)
            acc_scr[hh, :, q0:q0 + qn] = (jnp.exp2(m_old - m_new) * acc_scr[hh, :, q0:q0 + qn]
                                          + _dot(vt_aug, p))
            m_scr[hh, :, q0:q0 + qn] = m_new

    @pl.when(qi == 0)
    def _():
        for q0 in range(0, tq, Q_SLICE):
            scores(0, sa_scr, ma_scr, q0=q0)

    def body(jj, carry):
        t0 = 2 * jj
        for hh in range(HEADS_PER_STEP):
            for q0 in range(0, tq, Q_SLICE):
                scores(t0 + 1, sb_scr, mb_scr, heads=(hh,), q0=q0, qn=Q_SLICE)
                softmax_pv(t0, sa_scr, ma_scr, heads=(hh,), q0=q0, qn=Q_SLICE)
        for hh in range(HEADS_PER_STEP):
            for q0 in range(0, tq, Q_SLICE):
                scores(t0 + 2, sa_scr, ma_scr, heads=(hh,), q0=q0, qn=Q_SLICE)
                softmax_pv(t0 + 1, sb_scr, mb_scr, heads=(hh,), q0=q0, qn=Q_SLICE)
        return carry

    lax.fori_loop(0, qi, body, 0)

    d1 = 2 * qi
    key_chunk = lax.broadcasted_iota(jnp.int32, (tk, Q_SLICE), 0) // CHUNK
    qry_local = lax.broadcasted_iota(jnp.int32, (tk, Q_SLICE), 1)

    def visible_keys(q_off):
        return min(tk, q_off + Q_SLICE)

    def chunk_mask(q_off):
        if q_off >= tk:
            return None
        return (key_chunk <= (qry_local + q_off) // CHUNK)[:visible_keys(q_off)]

    for hh in range(HEADS_PER_STEP):
        for q0 in range(0, tq, Q_SLICE):
            if q0 >= tk:
                scores(d1 + 1, sb_scr, mb_scr, heads=(hh,), q0=q0, qn=Q_SLICE, kn=visible_keys(q0 - tk))
            softmax_pv(d1, sa_scr, ma_scr, mask=chunk_mask(q0), heads=(hh,), q0=q0, qn=Q_SLICE,
                       kn=visible_keys(q0))
    q_next = jnp.minimum(qi + 1, n_q - 1)
    for hh in range(HEADS_PER_STEP):
        for q0 in range(0, tq, Q_SLICE):
            scores(0, sa_scr, ma_scr, q_idx=q_next, heads=(hh,), q0=q0, qn=Q_SLICE)
            if q0 >= tk:
                softmax_pv(d1 + 1, sb_scr, mb_scr, mask=chunk_mask(q0 - tk), heads=(hh,),
                           q0=q0, qn=Q_SLICE, kn=visible_keys(q0 - tk))
    ms_p, accs_p = _prefix_state([q_rows(hh, qi, 0, tq) for hh in range(HEADS_PER_STEP)],
                                 kp_ref, vtp_ref, n_prefix_valid)
    accs = []
    for hh in range(HEADS_PER_STEP):
        m_run = m_scr[hh]
        m_all = jnp.maximum(m_run, ms_p[hh])
        accs.append(jnp.exp2(m_run - m_all) * acc_scr[hh] + jnp.exp2(ms_p[hh] - m_all) * accs_p[hh])
    _store_normalised(accs, o_ref)


def _attn_dense_call(q, kp, vtp, *, n_prefix_valid):
    bsz, _, tq, _ = q.shape
    n_prefix = kp.shape[2]
    return pl.pallas_call(
        functools.partial(_attn_dense_kernel, n_valid=n_prefix_valid),
        grid=(bsz,),
        in_specs=[
            pl.BlockSpec((1, N_HEADS, tq, HEAD_PAD), lambda b: (b, 0, 0, 0)),
            pl.BlockSpec((1, N_HEADS, n_prefix, HEAD_PAD), lambda b: (b, 0, 0, 0)),
            pl.BlockSpec((1, ATTN_W, n_prefix), lambda b: (b, 0, 0)),
        ],
        out_specs=pl.BlockSpec((1, tq, ATTN_W), lambda b: (b, 0, 0)),
        out_shape=jax.ShapeDtypeStruct((bsz, tq, ATTN_W), jnp.bfloat16),
        compiler_params=pltpu.CompilerParams(
            dimension_semantics=("arbitrary",), vmem_limit_bytes=VMEM_LIMIT_BYTES),
        name="attn_dense",
    )(q, kp, vtp)


def _attn_causal_call(q, kp, vtp, k, vt, *, tq, n_prefix_valid):
    bsz, _, seq, _ = q.shape
    n_prefix = kp.shape[2]
    tk = tq // 2
    f32 = jnp.float32
    return pl.pallas_call(
        functools.partial(_attn_causal_kernel, tq=tq, tk=tk, n_prefix_valid=n_prefix_valid),
        grid=(bsz, N_HEADS // HEADS_PER_STEP, seq // tq),
        in_specs=[
            pl.BlockSpec((1, HEADS_PER_STEP, seq, HEAD_PAD), lambda b, hp, i: (b, hp, 0, 0)),
            pl.BlockSpec((1, HEADS_PER_STEP, n_prefix, HEAD_PAD), lambda b, hp, i: (0, hp, 0, 0)),
            pl.BlockSpec((1, HEADS_PER_STEP * V_DIM, n_prefix), lambda b, hp, i: (0, hp, 0)),
            pl.BlockSpec((1, HEADS_PER_STEP, seq, HEAD_PAD), lambda b, hp, i: (b, hp, 0, 0)),
            pl.BlockSpec((1, HEADS_PER_STEP * V_DIM, seq), lambda b, hp, i: (b, hp, 0)),
        ],
        out_specs=pl.BlockSpec((1, tq, HEADS_PER_STEP * V_DIM), lambda b, hp, i: (b, i, hp)),
        out_shape=jax.ShapeDtypeStruct((bsz, seq, ATTN_W), jnp.bfloat16),
        scratch_shapes=[pltpu.VMEM((HEADS_PER_STEP, tq // Q_SLICE, tk, Q_SLICE), f32),
                        pltpu.VMEM((HEADS_PER_STEP, tq // Q_SLICE, tk, Q_SLICE), f32),
                        pltpu.VMEM((HEADS_PER_STEP, 1, tq), f32),
                        pltpu.VMEM((HEADS_PER_STEP, 1, tq), f32),
                        pltpu.VMEM((HEADS_PER_STEP, 1, tq), f32),
                        pltpu.VMEM((HEADS_PER_STEP, V_AUG, tq), f32)],
        compiler_params=pltpu.CompilerParams(
            dimension_semantics=("arbitrary", "arbitrary", "arbitrary"),
            vmem_limit_bytes=VMEM_LIMIT_BYTES),
        name="attn_causal",
    )(q, kp, vtp, k, vt)


def _out_mlp_kernel(o_ref, y_ref, x_ref, g_attn_ref, w_out_ref, g_mlp_ref, w_up_ref, w_down_ref,
                    g_fin_ref, out_ref, hn_scr, *, final):
    c = pl.program_id(1)

    @pl.when(c == 0)
    def _():
        on = _rms(o_ref[...].astype(jnp.float32), g_attn_ref[...]).astype(jnp.bfloat16)
        h = (x_ref[...] + _dot(on, w_out_ref[0:ATTN_W, :])
             + _dot(y_ref[...], w_out_ref[ATTN_W:ATTN_W + LRU_W, :]))
        hn_scr[...] = _rms(h, g_mlp_ref[...]).astype(jnp.bfloat16)
        out_ref[...] = h

    u = _dot(hn_scr[...], w_up_ref[...])
    u = jnp.square(jnp.maximum(u, 0.0)).astype(jnp.bfloat16)
    out_ref[...] += _dot(u, w_down_ref[...])

    if final:
        @pl.when(c == pl.num_programs(1) - 1)
        def _():
            out_ref[...] = _rms(out_ref[...], g_fin_ref[...])


def _out_mlp_call(o, y, x, lw, big, g_fin, *, layer, tm, final):
    rows = x.shape[0]
    const = lambda r, c: (0, 0)
    return pl.pallas_call(
        functools.partial(_out_mlp_kernel, final=final),
        grid=(rows // tm, D_FF // F_CHUNK),
        in_specs=[
            pl.BlockSpec((tm, ATTN_W), lambda r, c: (r, 0)),
            pl.BlockSpec((tm, LRU_W), lambda r, c: (r, 0)),
            pl.BlockSpec((tm, D_MODEL), lambda r, c: (r, 0)),
            pl.BlockSpec((1, ATTN_W), const),
            pl.BlockSpec((None, D_MODEL, D_MODEL), lambda r, c: (layer, 0, 0)),
            pl.BlockSpec((1, D_MODEL), const),
            pl.BlockSpec((None, D_MODEL, F_CHUNK), lambda r, c: (layer, 0, c)),
            pl.BlockSpec((None, F_CHUNK, D_MODEL), lambda r, c: (layer, c, 0)),
            pl.BlockSpec((1, D_MODEL), const),
        ],
        out_specs=pl.BlockSpec((tm, D_MODEL), lambda r, c: (r, 0)),
        out_shape=jax.ShapeDtypeStruct((rows, D_MODEL), jnp.float32),
        scratch_shapes=[pltpu.VMEM((tm, D_MODEL), jnp.bfloat16)],
        compiler_params=pltpu.CompilerParams(
            dimension_semantics=("arbitrary", "arbitrary"), vmem_limit_bytes=VMEM_LIMIT_BYTES),
        name="out_mlp",
    )(o, y, x, lw["g_attn"], big["w_out"], lw["g_mlp"], big["w_up"], big["w_down"], g_fin)


def _rope_tables(pos, rope_lane0, pass_lanes, scale):
    half = QK_ROPE // 2
    lane = jnp.arange(LANES, dtype=jnp.int32)
    rel = lane - rope_lane0
    in_rope = (rel >= 0) & (rel < QK_ROPE)
    freq_idx = jnp.where(in_rope, rel % half, 0).astype(jnp.float32)
    inv = ROPE_THETA ** (-(2.0 * freq_idx) / QK_ROPE)
    ang = pos.astype(jnp.float32)[:, None] * inv[None, :]
    cos, sin = jnp.cos(ang), jnp.sin(ang)
    passthrough = (lane < pass_lanes).astype(jnp.float32)[None, :]
    tab_c = jnp.where(in_rope[None, :], cos, passthrough)
    tab_m = jnp.where((in_rope & (rel < half))[None, :], -sin, 0.0)
    tab_p = jnp.where((in_rope & (rel >= half))[None, :], sin, 0.0)
    return jnp.stack([tab_c, tab_m, tab_p]) * scale


def _key_tables(pos):
    return _rope_tables(pos, 0, 0, 1.0)


def _query_tables(pos):
    return _rope_tables(pos, QK_NOPE, QK_NOPE, SM_SCALE * LOG2_E)


def _block_diag_gates(w_a, w_x):
    per_half = LRU_BLOCKS // 2
    halves = []
    for s in range(2):
        bd_a = jax.scipy.linalg.block_diag(*[w_a[s * per_half + i] for i in range(per_half)])
        bd_x = jax.scipy.linalg.block_diag(*[w_x[s * per_half + i] for i in range(per_half)])
        halves.append(jnp.concatenate([bd_a, bd_x], axis=1))
    return jnp.stack(halves).astype(jnp.bfloat16)


def _layer_weights(l, norm_mix_g, w_in, q_norm_g, w_uq, kv_norm_g, w_ukv, conv_w, conv_b,
                   w_gate_a, b_gate_a, w_gate_x, b_gate_x, lru_lambda, attn_out_g, lru_out_g,
                   norm_mlp_g):
    bf = jnp.bfloat16
    row = lambda v: v.reshape(1, -1).astype(jnp.float32)
    w_in_p = jnp.concatenate(
        [w_in[l][:, :COL_KR + QK_ROPE], jnp.zeros((D_MODEL, ROPE_PAD - QK_ROPE), w_in.dtype),
         w_in[l][:, COL_KR + QK_ROPE:]], axis=1).astype(bf)
    w_uq_p = jnp.pad(w_uq[l].reshape(Q_LORA, N_HEADS, QK_NOPE + QK_ROPE),
                     ((0, 0), (0, 0), (0, HEAD_PAD - QK_NOPE - QK_ROPE))).reshape(Q_LORA, -1).astype(bf)
    w_ukv_h = w_ukv[l].reshape(KV_LORA, N_HEADS, QK_NOPE + V_DIM)
    w_uk_p = jnp.pad(w_ukv_h[:, :, :QK_NOPE],
                     ((0, 0), (0, 0), (0, HEAD_PAD - QK_NOPE))).reshape(KV_LORA, -1).astype(bf)
    w_uvt = w_ukv_h[:, :, QK_NOPE:].reshape(KV_LORA, ATTN_W).T.astype(bf)
    return {
        "g_mix": row(norm_mix_g[l]), "w_in": w_in_p, "g_q": row(q_norm_g[l]), "w_uq": w_uq_p,
        "g_kv": row(kv_norm_g[l]), "w_uk": w_uk_p, "w_uvt": w_uvt,
        "conv_w": conv_w[l].astype(jnp.float32), "conv_b": row(conv_b[l]),
        "wg": _block_diag_gates(w_gate_a[l], w_gate_x[l]),
        "bg": jnp.stack([b_gate_a[l], b_gate_x[l]]).astype(jnp.float32),
        "lam": row(lru_lambda[l]), "g_lru": row(lru_out_g[l]), "g_attn": row(attn_out_g[l]),
        "g_mlp": row(norm_mlp_g[l]),
    }


def _pad_buf(buf):
    return jnp.pad(buf, ((0, 0), (SUBLANES - (CONV_W - 1), 0), (0, 0)))


def _pad_axis(a, axis, size):
    pad = [(0, 0)] * a.ndim
    pad[axis] = (0, size - a.shape[axis])
    return jnp.pad(a, pad)


def kernel(x_prompt, x_sample, cache_ckv, cache_kpe, state_lru_h, state_conv, meta_tokens,
           norm_mix_g, w_in, q_norm_g, w_uq, kv_norm_g, w_ukv, conv_w, conv_b,
           w_gate_a, b_gate_a, w_gate_x, b_gate_x, lru_lambda, attn_out_g, lru_out_g,
           w_out, norm_mlp_g, w_up, w_down, final_norm_g):
    b_p, seq, _ = x_prompt.shape
    b_s, dec_seq, _ = x_sample.shape
    past_len = cache_ckv.shape[2]
    assert (past_len + dec_seq - 1) // CHUNK == past_len // CHUNK
    tm_p = 512
    tq_p = 1024
    tm_o = 1024
    assert seq % tm_p == 0 and seq % tq_p == 0 and (tq_p // 2) % CHUNK == 0

    meta_pos = jnp.arange(-N_META, 0, dtype=jnp.int32)
    prompt_pos = jnp.arange(seq, dtype=jnp.int32)
    sample_pos = past_len + jnp.arange(dec_seq, dtype=jnp.int32)
    tabk_m, tabk_p, tabk_s = _key_tables(meta_pos), _key_tables(prompt_pos), _key_tables(sample_pos)
    tabq_m, tabq_p, tabq_s = _query_tables(meta_pos), _query_tables(prompt_pos), _query_tables(sample_pos)
    g_fin = final_norm_g.reshape(1, -1).astype(jnp.float32)

    cache_kpe_pad = _pad_axis(cache_kpe, 3, ROPE_PAD)
    big = {"w_out": w_out.astype(jnp.bfloat16), "w_up": w_up.astype(jnp.bfloat16),
           "w_down": w_down.astype(jnp.bfloat16)}
    tq_small = LANES

    h_meta = meta_tokens[None].astype(jnp.float32)
    h_p, h_s = x_prompt, x_sample
    zero_h = jnp.zeros((1, 1, LRU_W), jnp.float32)
    zero_buf = jnp.zeros((1, SUBLANES, LRU_W), jnp.float32)
    outs = {name: [] for name in ("ckv_p", "kpe_p", "lru_p", "conv_p", "ckv_s", "kpe_s", "lru_s", "conv_s")}
    for l in range(DEPTH):
        lw = _layer_weights(l, norm_mix_g, w_in, q_norm_g, w_uq, kv_norm_g, w_ukv, conv_w, conv_b,
                            w_gate_a, b_gate_a, w_gate_x, b_gate_x, lru_lambda, attn_out_g,
                            lru_out_g, norm_mlp_g)
        last = l + 1 == DEPTH
        mq, mk, mvt, m_ckv, m_kpe, m_y, m_h, m_buf = _mixer_call(
            h_meta, zero_h, zero_buf, tabq_m, tabk_m, lw, tm=N_META, shared_state=True)
        pq, pk, pvt, p_ckv, p_kpe, p_y, p_h, p_buf = _mixer_call(
            h_p, m_h, _pad_buf(m_buf), tabq_p, tabk_p, lw, tm=tm_p, shared_state=True)
        mk_p, mvt_p = _pad_axis(mk, 2, LANES), _pad_axis(mvt, 2, LANES)
        p_o = _attn_causal_call(pq, mk_p, mvt_p, pk, pvt, tq=tq_p, n_prefix_valid=N_META)
        h_p = _out_mlp_call(p_o.reshape(b_p * seq, ATTN_W), p_y.reshape(b_p * seq, LRU_W),
                            h_p.reshape(b_p * seq, D_MODEL), lw, big, g_fin, layer=l, tm=tm_o,
                            final=last).reshape(b_p, seq, D_MODEL)
        sq, sk, svt, s_ckv, s_kpe, s_y, s_h, s_buf = _mixer_call(
            h_s, state_lru_h[l][:, None, :], _pad_buf(state_conv[l]), tabq_s, tabk_s, lw,
            tm=dec_seq, shared_state=False)
        s_o = _attn_sample_call(sq, m_ckv, cache_ckv, s_ckv, m_kpe, cache_kpe_pad, s_kpe, lw,
                                layer=l)[:, :dec_seq]
        h_s = _out_mlp_call(s_o.reshape(b_s * dec_seq, ATTN_W), s_y.reshape(b_s * dec_seq, LRU_W),
                            h_s.reshape(b_s * dec_seq, D_MODEL), lw, big, g_fin, layer=l,
                            tm=b_s * dec_seq, final=last).reshape(b_s, dec_seq, D_MODEL)
        if not last:
            m_o = _attn_dense_call(_pad_axis(mq, 2, tq_small), mk_p, mvt_p,
                                   n_prefix_valid=N_META)[:, :N_META]
            h_meta = _out_mlp_call(m_o.reshape(N_META, ATTN_W), m_y.reshape(N_META, LRU_W),
                                   h_meta.reshape(N_META, D_MODEL), lw, big, g_fin, layer=l,
                                   tm=N_META, final=False).reshape(1, N_META, D_MODEL)
        outs["ckv_p"].append(p_ckv); outs["kpe_p"].append(p_kpe[:, :, :QK_ROPE])
        outs["lru_p"].append(p_h[:, 0]); outs["conv_p"].append(p_buf)
        outs["ckv_s"].append(s_ckv); outs["kpe_s"].append(s_kpe[:, :, :QK_ROPE])
        outs["lru_s"].append(s_h[:, 0]); outs["conv_s"].append(s_buf)

    return (h_p, h_s, jnp.stack(outs["ckv_p"]), jnp.stack(outs["kpe_p"]), jnp.stack(outs["lru_p"]),
            jnp.stack(outs["conv_p"]), jnp.stack(outs["ckv_s"]), jnp.stack(outs["kpe_s"]),
            jnp.stack(outs["lru_s"]), jnp.stack(outs["conv_s"]))
```

```python
import functools

import jax
import jax.numpy as jnp
from jax import lax
from jax.experimental import pallas as pl
from jax.experimental.pallas import tpu as pltpu

D_MODEL = 1024
DEPTH = 2
CHUNK = 64
N_META = 16
ATTN_W = 512
LRU_W = 512
V_DIM = 64
N_HEADS = 8
QK_NOPE = 64
QK_ROPE = 32
Q_LORA = 768
KV_LORA = 256
LRU_BLOCKS = 8
LRU_BW = LRU_W // LRU_BLOCKS
LRU_C = 8.0
CONV_W = 4
D_FF = 4 * D_MODEL
ROPE_THETA = 10000.0
EPS = 1e-6
SM_SCALE = (QK_NOPE + QK_ROPE) ** -0.5
NEG_INF = -1e30
LOG2_E = 1.4426950408889634

LANES = 128
SUBLANES = 8
HEAD_PAD = LANES
VMEM_LIMIT_BYTES = 56 * 1024 * 1024

ROPE_PAD = LANES
COL_CQ = 0
COL_CKV = Q_LORA
COL_KR = Q_LORA + KV_LORA
COL_XR = COL_KR + ROPE_PAD
COL_GR = COL_XR + LRU_W
IN_W_PAD = COL_GR + LRU_W

HEADS_PER_STEP = 2
ONES_ROWS = 16
V_AUG = V_DIM + ONES_ROWS
F_CHUNK = 1024
Q_SLICE = 256


def _rms(x, g):
    return x * lax.rsqrt(jnp.mean(x * x, axis=-1, keepdims=True) + EPS) * g


def _rope_lanes(x, tab_ref):
    half = QK_ROPE // 2
    return (x * tab_ref[0] + pltpu.roll(x, LANES - half, 1) * tab_ref[1]
            + pltpu.roll(x, half, 1) * tab_ref[2])


def _dot(a, b):
    return jnp.dot(a, b, preferred_element_type=jnp.float32)


def _dot_nt(a, b):
    return lax.dot_general(a, b, (((1,), (1,)), ((), ())), preferred_element_type=jnp.float32)


def _mixer_kernel(x_ref, h0_ref, buf0_ref, tabq_ref, tabk_ref, g_mix_ref, w_in_ref, g_q_ref,
                  w_uq_ref, g_kv_ref, w_uk_ref, w_uvt_ref, conv_w_ref, conv_b_ref, wg_ref, bg_ref,
                  lam_ref, g_lru_ref,
                  q_ref, k_ref, vt_ref, ckv_ref, kpe_ref, y_ref, hlast_ref, buf_ref,
                  tail_scr, hcar_scr, *, tm):
    t = pl.program_id(1)

    @pl.when(t == 0)
    def _():
        tail_scr[...] = buf0_ref[0]
        hcar_scr[...] = h0_ref[0]

    x = x_ref[0]
    xn = _rms(x, g_mix_ref[...]).astype(jnp.bfloat16)
    z = _dot(xn, w_in_ref[...])

    cqn = _rms(z[:, COL_CQ:COL_CQ + Q_LORA], g_q_ref[...]).astype(jnp.bfloat16)
    q = _dot(cqn, w_uq_ref[...])
    for h in range(N_HEADS):
        qh = q[:, h * HEAD_PAD:(h + 1) * HEAD_PAD]
        q_ref[0, h] = _rope_lanes(qh, tabq_ref).astype(jnp.bfloat16)

    ckv = _rms(z[:, COL_CKV:COL_CKV + KV_LORA], g_kv_ref[...])
    ckv_ref[0] = ckv
    ckv_b = ckv.astype(jnp.bfloat16)
    kpe = _rope_lanes(z[:, COL_KR:COL_KR + ROPE_PAD], tabk_ref)
    kpe_ref[0] = kpe
    kpe_shift = pltpu.roll(kpe, QK_NOPE, 1)
    kk = _dot(ckv_b, w_uk_ref[...])
    for h in range(N_HEADS):
        k_ref[0, h] = (kk[:, h * HEAD_PAD:(h + 1) * HEAD_PAD] + kpe_shift).astype(jnp.bfloat16)
    vt_ref[0] = _dot_nt(w_uvt_ref[...], ckv_b).astype(jnp.bfloat16)

    x_r = z[:, COL_XR:COL_XR + LRU_W]
    g_r = z[:, COL_GR:COL_GR + LRU_W]
    n_slabs = tm // SUBLANES
    rows8 = lax.broadcasted_iota(jnp.int32, (SUBLANES, LRU_W), 0)
    slabs = [tail_scr[...]] + [x_r[g * SUBLANES:(g + 1) * SUBLANES, :] for g in range(n_slabs)]
    xc = conv_b_ref[...] + x_r * conv_w_ref[CONV_W - 1:CONV_W, :]
    for back in range(1, CONV_W):
        rolled = [pltpu.roll(s, back, 0) for s in slabs]
        shifted = jnp.concatenate(
            [jnp.where(rows8 < back, rolled[g], rolled[g + 1]) for g in range(n_slabs)], axis=0)
        xc = xc + shifted * conv_w_ref[CONV_W - 1 - back:CONV_W - back, :]
    tail_scr[...] = slabs[-1]
    buf_ref[0] = tail_scr[SUBLANES - (CONV_W - 1):SUBLANES, :]

    half_w = LRU_W // 2
    gates = [_dot(xc[:, s * half_w:(s + 1) * half_w].astype(jnp.bfloat16), wg_ref[s]) for s in range(2)]
    pre_a = jnp.concatenate([gates[0][:, :half_w], gates[1][:, :half_w]], axis=1)
    pre_x = jnp.concatenate([gates[0][:, half_w:], gates[1][:, half_w:]], axis=1)
    r = jax.nn.sigmoid(pre_a + bg_ref[0:1, :])
    gi = jax.nn.sigmoid(pre_x + bg_ref[1:2, :])
    neg_lam = -lam_ref[...]
    softplus = jnp.maximum(neg_lam, 0.0) + jnp.log1p(jnp.exp(-jnp.abs(neg_lam)))
    log_a = (-LRU_C) * r * softplus
    a = jnp.exp(log_a)
    one_m_a2 = -jnp.tanh(log_a) * (a * a + 1.0)
    root = jnp.where(one_m_a2 > 0.0, one_m_a2 * lax.rsqrt(one_m_a2), 0.0)
    b = root * (gi * xc)

    carry = jnp.broadcast_to(hcar_scr[...], (SUBLANES, LRU_W))
    h_slabs = []
    for g in range(n_slabs):
        a_g = a[g * SUBLANES:(g + 1) * SUBLANES, :]
        b_g = b[g * SUBLANES:(g + 1) * SUBLANES, :]
        shift = 1
        while shift < SUBLANES:
            keep = rows8 >= shift
            a_prev = jnp.where(keep, pltpu.roll(a_g, shift, 0), 1.0)
            b_prev = jnp.where(keep, pltpu.roll(b_g, shift, 0), 0.0)
            b_g = a_g * b_prev + b_g
            a_g = a_g * a_prev
            shift *= 2
        h_g = a_g * carry + b_g
        h_slabs.append(h_g)
        carry = jnp.broadcast_to(h_g[SUBLANES - 1:SUBLANES, :], (SUBLANES, LRU_W))
    hh = jnp.concatenate(h_slabs, axis=0)
    h_last = carry[0:1, :]
    hcar_scr[...] = h_last
    hlast_ref[0] = h_last
    y = hh * jax.nn.gelu(g_r)
    y_ref[0] = _rms(y, g_lru_ref[...]).astype(jnp.bfloat16)


def _mixer_call(x, h0, buf0p, tabq, tabk, lw, *, tm, shared_state):
    bsz, seq, _ = x.shape
    nt = seq // tm
    const2 = lambda b, t: (0, 0)
    const3 = lambda b, t: (0, 0, 0)
    state_idx = (lambda b, t: (0, 0, 0)) if shared_state else (lambda b, t: (b, 0, 0))
    in_specs = [
        pl.BlockSpec((1, tm, D_MODEL), lambda b, t: (b, t, 0)),
        pl.BlockSpec((1, 1, LRU_W), state_idx),
        pl.BlockSpec((1, SUBLANES, LRU_W), state_idx),
        pl.BlockSpec((3, tm, LANES), lambda b, t: (0, t, 0)),
        pl.BlockSpec((3, tm, LANES), lambda b, t: (0, t, 0)),
        pl.BlockSpec((1, D_MODEL), const2),
        pl.BlockSpec((D_MODEL, IN_W_PAD), const2),
        pl.BlockSpec((1, Q_LORA), const2),
        pl.BlockSpec((Q_LORA, N_HEADS * HEAD_PAD), const2),
        pl.BlockSpec((1, KV_LORA), const2),
        pl.BlockSpec((KV_LORA, N_HEADS * HEAD_PAD), const2),
        pl.BlockSpec((ATTN_W, KV_LORA), const2),
        pl.BlockSpec((CONV_W, LRU_W), const2),
        pl.BlockSpec((1, LRU_W), const2),
        pl.BlockSpec((2, LRU_W // 2, LRU_W), const3),
        pl.BlockSpec((2, LRU_W), const2),
        pl.BlockSpec((1, LRU_W), const2),
        pl.BlockSpec((1, LRU_W), const2),
    ]
    out_shape = [
        jax.ShapeDtypeStruct((bsz, N_HEADS, seq, HEAD_PAD), jnp.bfloat16),
        jax.ShapeDtypeStruct((bsz, N_HEADS, seq, HEAD_PAD), jnp.bfloat16),
        jax.ShapeDtypeStruct((bsz, ATTN_W, seq), jnp.bfloat16),
        jax.ShapeDtypeStruct((bsz, seq, KV_LORA), jnp.float32),
        jax.ShapeDtypeStruct((bsz, seq, ROPE_PAD), jnp.float32),
        jax.ShapeDtypeStruct((bsz, seq, LRU_W), jnp.bfloat16),
        jax.ShapeDtypeStruct((bsz, 1, LRU_W), jnp.float32),
        jax.ShapeDtypeStruct((bsz, CONV_W - 1, LRU_W), jnp.float32),
    ]
    out_specs = [
        pl.BlockSpec((1, N_HEADS, tm, HEAD_PAD), lambda b, t: (b, 0, t, 0)),
        pl.BlockSpec((1, N_HEADS, tm, HEAD_PAD), lambda b, t: (b, 0, t, 0)),
        pl.BlockSpec((1, ATTN_W, tm), lambda b, t: (b, 0, t)),
        pl.BlockSpec((1, tm, KV_LORA), lambda b, t: (b, t, 0)),
        pl.BlockSpec((1, tm, ROPE_PAD), lambda b, t: (b, t, 0)),
        pl.BlockSpec((1, tm, LRU_W), lambda b, t: (b, t, 0)),
        pl.BlockSpec((1, 1, LRU_W), lambda b, t: (b, 0, 0)),
        pl.BlockSpec((1, CONV_W - 1, LRU_W), lambda b, t: (b, 0, 0)),
    ]
    return pl.pallas_call(
        functools.partial(_mixer_kernel, tm=tm),
        grid=(bsz, nt),
        in_specs=in_specs,
        out_specs=out_specs,
        out_shape=out_shape,
        scratch_shapes=[pltpu.VMEM((SUBLANES, LRU_W), jnp.float32),
                        pltpu.VMEM((1, LRU_W), jnp.float32)],
        compiler_params=pltpu.CompilerParams(
            dimension_semantics=("arbitrary", "arbitrary"), vmem_limit_bytes=VMEM_LIMIT_BYTES),
        name="mixer",
    )(x, h0, buf0p, tabq, tabk, lw["g_mix"], lw["w_in"], lw["g_q"], lw["w_uq"], lw["g_kv"],
      lw["w_uk"], lw["w_uvt"], lw["conv_w"], lw["conv_b"], lw["wg"], lw["bg"], lw["lam"], lw["g_lru"])


def _attn_sample_kernel(q_ref, ckv_m_ref, ckv_c_ref, ckv_s_ref, kpe_m_ref, kpe_c_ref, kpe_s_ref,
                        w_uk_ref, w_uvt_ref, o_ref, *, n_keys_pad):
    ckv_parts = [ckv_m_ref[0], ckv_c_ref[0, 0], ckv_s_ref[0]]
    kpe_parts = [kpe_m_ref[0], kpe_c_ref[0, 0], kpe_s_ref[0]]
    n_valid = sum(p.shape[0] for p in ckv_parts)
    pad = n_keys_pad - n_valid
    ckv = jnp.concatenate(ckv_parts + [jnp.zeros((pad, KV_LORA), jnp.float32)], axis=0)
    kpe = jnp.concatenate(kpe_parts + [jnp.zeros((pad, ROPE_PAD), jnp.float32)], axis=0)
    ckv_b = ckv.astype(jnp.bfloat16)
    kpe_shift = pltpu.roll(kpe, QK_NOPE, 1)
    kk = _dot(ckv_b, w_uk_ref[...])
    vt = _dot_nt(w_uvt_ref[...], ckv_b).astype(jnp.bfloat16)
    n_q = q_ref.shape[2]
    q_pad = jnp.zeros((LANES - n_q, HEAD_PAD), jnp.bfloat16)
    q_tiles = [jnp.concatenate([q_ref[0, h], q_pad], axis=0) for h in range(N_HEADS)]
    zero_q = jnp.zeros((LANES, HEAD_PAD), jnp.bfloat16)
    ones_k = jnp.ones((ONES_ROWS, n_keys_pad), jnp.bfloat16)
    kpe_pair = jnp.concatenate([kpe_shift, kpe_shift], axis=1)
    key_idx = lax.broadcasted_iota(jnp.int32, (n_keys_pad, 2 * LANES), 0)
    accs = []
    for pair in range(N_HEADS // 2):
        h_a, h_b = 2 * pair, 2 * pair + 1
        k_pair = (kk[:, h_a * HEAD_PAD:(h_b + 1) * HEAD_PAD] + kpe_pair).astype(jnp.bfloat16)
        q_blockdiag = jnp.concatenate([jnp.concatenate([q_tiles[h_a], zero_q], axis=1),
                                       jnp.concatenate([zero_q, q_tiles[h_b]], axis=1)], axis=0)
        st = _dot_nt(k_pair, q_blockdiag)
        st = jnp.where(key_idx < n_valid, st, NEG_INF)
        p = jnp.exp2(st - _col_max(st)).astype(jnp.bfloat16)
        vt_pair = jnp.concatenate([vt[h_a * V_DIM:(h_a + 1) * V_DIM, :], ones_k,
                                   vt[h_b * V_DIM:(h_b + 1) * V_DIM, :], ones_k], axis=0)
        acc = _dot(vt_pair, p)
        accs.append(acc[:V_AUG, :LANES])
        accs.append(acc[V_AUG:, LANES:])
    _store_normalised(accs, o_ref)


def _attn_sample_call(q, ckv_m, ckv_cache, ckv_s, kpe_m, kpe_cache, kpe_s, lw, *, layer):
    bsz, _, dec, _ = q.shape
    past = ckv_cache.shape[2]
    n_meta = ckv_m.shape[1]
    n_keys_pad = -(-(n_meta + past + dec) // LANES) * LANES
    return pl.pallas_call(
        functools.partial(_attn_sample_kernel, n_keys_pad=n_keys_pad),
        grid=(bsz,),
        in_specs=[
            pl.BlockSpec((1, N_HEADS, dec, HEAD_PAD), lambda b: (b, 0, 0, 0)),
            pl.BlockSpec((1, n_meta, KV_LORA), lambda b: (0, 0, 0)),
            pl.BlockSpec((1, 1, past, KV_LORA), lambda b: (layer, b, 0, 0)),
            pl.BlockSpec((1, dec, KV_LORA), lambda b: (b, 0, 0)),
            pl.BlockSpec((1, n_meta, ROPE_PAD), lambda b: (0, 0, 0)),
            pl.BlockSpec((1, 1, past, ROPE_PAD), lambda b: (layer, b, 0, 0)),
            pl.BlockSpec((1, dec, ROPE_PAD), lambda b: (b, 0, 0)),
            pl.BlockSpec((KV_LORA, N_HEADS * HEAD_PAD), lambda b: (0, 0)),
            pl.BlockSpec((ATTN_W, KV_LORA), lambda b: (0, 0)),
        ],
        out_specs=pl.BlockSpec((1, LANES, ATTN_W), lambda b: (b, 0, 0)),
        out_shape=jax.ShapeDtypeStruct((bsz, LANES, ATTN_W), jnp.bfloat16),
        compiler_params=pltpu.CompilerParams(
            dimension_semantics=("arbitrary",), vmem_limit_bytes=VMEM_LIMIT_BYTES),
        name="attn_sample",
    )(q, ckv_m, ckv_cache, ckv_s, kpe_m, kpe_cache, kpe_s, lw["w_uk"], lw["w_uvt"])


def _col_max(st):
    rows, cols = st.shape
    slab = SUBLANES
    if rows % slab == 0 and rows > slab:
        st = jnp.max(st.reshape(rows // slab, slab, cols), axis=0)
    return jnp.max(st, axis=0, keepdims=True)


def _dense_state(q_tiles, k_tiles, vt_tiles, n_valid):
    n_keys = k_tiles[0].shape[0]
    tq = q_tiles[0].shape[0]
    n_used = -(-n_valid // ONES_ROWS) * ONES_ROWS
    ones_p = jnp.ones((ONES_ROWS, n_keys), jnp.bfloat16)
    ms, accs = [], []
    for q_t, k_t, vt_t in zip(q_tiles, k_tiles, vt_tiles):
        st = _dot_nt(k_t[:n_used], q_t)
        if n_valid < n_used:
            key_idx = lax.broadcasted_iota(jnp.int32, (n_used, tq), 0)
            st = jnp.where(key_idx < n_valid, st, NEG_INF)
        m = _col_max(st)
        p = jnp.exp2(st - m).astype(jnp.bfloat16)
        if n_used < n_keys:
            p = jnp.concatenate([p, jnp.zeros((n_keys - n_used, tq), jnp.bfloat16)], axis=0)
        ms.append(m)
        accs.append(_dot(jnp.concatenate([vt_t, ones_p], axis=0), p))
    return ms, accs


def _prefix_state(q_tiles, kp_ref, vtp_ref, n_valid):
    heads = range(len(q_tiles))
    return _dense_state(q_tiles, [kp_ref[0, hh] for hh in heads],
                        [vtp_ref[0, hh * V_DIM:(hh + 1) * V_DIM, :] for hh in heads], n_valid)


def _store_normalised(accs, o_ref):
    outs = [acc[:V_DIM, :] / acc[V_DIM:V_DIM + 1, :] for acc in accs]
    o_ref[0] = jnp.concatenate(outs, axis=0).T.astype(jnp.bfloat16)


def _attn_dense_kernel(q_ref, kp_ref, vtp_ref, o_ref, *, n_valid):
    q_tiles = [q_ref[0, hh] for hh in range(q_ref.shape[1])]
    _, accs = _prefix_state(q_tiles, kp_ref, vtp_ref, n_valid)
    _store_normalised(accs, o_ref)


def _attn_causal_kernel(q_ref, kp_ref, vtp_ref, k_ref, vt_ref, o_ref,
                        sa_scr, sb_scr, ma_scr, mb_scr, m_scr, acc_scr,
                        *, tq, tk, n_prefix_valid):
    qi = pl.program_id(2)
    n_q = pl.num_programs(2)

    def q_rows(hh, q_idx, q0, qn):
        return q_ref[0, hh, pl.ds(pl.multiple_of(q_idx * tq + q0, tk), qn), :]

    m_scr[...] = jnp.full(m_scr.shape, NEG_INF, jnp.float32)
    acc_scr[...] = jnp.zeros(acc_scr.shape, jnp.float32)
    ones_k = jnp.ones((ONES_ROWS, tk), jnp.bfloat16)

    def scores(tile, s_ref, mx_ref, q_idx=qi, q0=0, qn=Q_SLICE, kn=tk, heads=range(HEADS_PER_STEP)):
        start = pl.multiple_of(tile * tk, tk)
        for hh in heads:
            st = _dot_nt(k_ref[0, hh, pl.ds(start, kn), :], q_rows(hh, q_idx, q0, qn))
            s_ref[hh, q0 // Q_SLICE, 0:kn, :] = st
            mx_ref[hh, :, q0:q0 + qn] = _col_max(st)

    def softmax_pv(tile, s_ref, mx_ref, mask=None, q0=0, qn=Q_SLICE, kn=tk, heads=range(HEADS_PER_STEP)):
        start = pl.multiple_of(tile * tk, tk)
        for hh in heads:
            st = s_ref[hh, q0 // Q_SLICE, 0:kn, :]
            if mask is None:
                m_blk = mx_ref[hh, :, q0:q0 + qn]
            else:
                st = jnp.where(mask, st, NEG_INF)
                m_blk = _col_max(st)
            m_old = m_scr[hh, :, q0:q0 + qn]
            m_new = jnp.maximum(m_old, m_blk)
            p = jnp.exp2(st - m_new).astype(jnp.bfloat16)
            vt_t = vt_ref[0, hh * V_DIM:(hh + 1) * V_DIM, pl.ds(start, kn)]
            vt_aug = jnp.concatenate([vt_t, ones_k[:, :kn]], axis=0)
            acc_scr[hh, :, q0:q0 + qn] = (jnp.exp2(m_old - m_new) * acc_scr[hh, :, q0:q0 + qn]
                                          + _dot(vt_aug, p))
            m_scr[hh, :, q0:q0 + qn] = m_new

    @pl.when(qi == 0)
    def _():
        for q0 in range(0, tq, Q_SLICE):
            scores(0, sa_scr, ma_scr, q0=q0)

    def tile_pair(t0):
        for hh in range(HEADS_PER_STEP):
            for q0 in range(0, tq, Q_SLICE):
                scores(t0 + 1, sb_scr, mb_scr, heads=(hh,), q0=q0, qn=Q_SLICE)
                softmax_pv(t0, sa_scr, ma_scr, heads=(hh,), q0=q0, qn=Q_SLICE)
        for hh in range(HEADS_PER_STEP):
            for q0 in range(0, tq, Q_SLICE):
                scores(t0 + 2, sa_scr, ma_scr, heads=(hh,), q0=q0, qn=Q_SLICE)
                softmax_pv(t0 + 1, sb_scr, mb_scr, heads=(hh,), q0=q0, qn=Q_SLICE)

    def body(jj, carry):
        tile_pair(4 * jj)
        tile_pair(4 * jj + 2)
        return carry

    lax.fori_loop(0, qi // 2, body, 0)

    @pl.when(qi % 2 == 1)
    def _():
        tile_pair(2 * (qi - 1))

    d1 = 2 * qi
    key_chunk = lax.broadcasted_iota(jnp.int32, (tk, Q_SLICE), 0) // CHUNK
    qry_local = lax.broadcasted_iota(jnp.int32, (tk, Q_SLICE), 1)

    def visible_keys(q_off):
        return min(tk, q_off + Q_SLICE)

    def chunk_mask(q_off):
        if q_off >= tk:
            return None
        return (key_chunk <= (qry_local + q_off) // CHUNK)[:visible_keys(q_off)]

    for hh in range(HEADS_PER_STEP):
        for q0 in range(0, tq, Q_SLICE):
            if q0 >= tk:
                scores(d1 + 1, sb_scr, mb_scr, heads=(hh,), q0=q0, qn=Q_SLICE, kn=visible_keys(q0 - tk))
            softmax_pv(d1, sa_scr, ma_scr, mask=chunk_mask(q0), heads=(hh,), q0=q0, qn=Q_SLICE,
                       kn=visible_keys(q0))
    q_next = jnp.minimum(qi + 1, n_q - 1)
    for hh in range(HEADS_PER_STEP):
        for q0 in range(0, tq, Q_SLICE):
            scores(0, sa_scr, ma_scr, q_idx=q_next, heads=(hh,), q0=q0, qn=Q_SLICE)
            if q0 >= tk:
                softmax_pv(d1 + 1, sb_scr, mb_scr, mask=chunk_mask(q0 - tk), heads=(hh,),
                           q0=q0, qn=Q_SLICE, kn=visible_keys(q0 - tk))
    ms_p, accs_p = _prefix_state([q_rows(hh, qi, 0, tq) for hh in range(HEADS_PER_STEP)],
                                 kp_ref, vtp_ref, n_prefix_valid)
    accs = []
    for hh in range(HEADS_PER_STEP):
        m_run = m_scr[hh]
        m_all = jnp.maximum(m_run, ms_p[hh])
        accs.append(jnp.exp2(m_run - m_all) * acc_scr[hh] + jnp.exp2(ms_p[hh] - m_all) * accs_p[hh])
    _store_normalised(accs, o_ref)


def _attn_dense_call(q, kp, vtp, *, n_prefix_valid):
    bsz, _, tq, _ = q.shape
    n_prefix = kp.shape[2]
    return pl.pallas_call(
        functools.partial(_attn_dense_kernel, n_valid=n_prefix_valid),
        grid=(bsz,),
        in_specs=[
            pl.BlockSpec((1, N_HEADS, tq, HEAD_PAD), lambda b: (b, 0, 0, 0)),
            pl.BlockSpec((1, N_HEADS, n_prefix, HEAD_PAD), lambda b: (b, 0, 0, 0)),
            pl.BlockSpec((1, ATTN_W, n_prefix), lambda b: (b, 0, 0)),
        ],
        out_specs=pl.BlockSpec((1, tq, ATTN_W), lambda b: (b, 0, 0)),
        out_shape=jax.ShapeDtypeStruct((bsz, tq, ATTN_W), jnp.bfloat16),
        compiler_params=pltpu.CompilerParams(
            dimension_semantics=("arbitrary",), vmem_limit_bytes=VMEM_LIMIT_BYTES),
        name="attn_dense",
    )(q, kp, vtp)


def _attn_causal_call(q, kp, vtp, k, vt, *, tq, n_prefix_valid):
    bsz, _, seq, _ = q.shape
    n_prefix = kp.shape[2]
    tk = tq // 2
    f32 = jnp.float32
    return pl.pallas_call(
        functools.partial(_attn_causal_kernel, tq=tq, tk=tk, n_prefix_valid=n_prefix_valid),
        grid=(bsz, N_HEADS // HEADS_PER_STEP, seq // tq),
        in_specs=[
            pl.BlockSpec((1, HEADS_PER_STEP, seq, HEAD_PAD), lambda b, hp, i: (b, hp, 0, 0)),
            pl.BlockSpec((1, HEADS_PER_STEP, n_prefix, HEAD_PAD), lambda b, hp, i: (0, hp, 0, 0)),
            pl.BlockSpec((1, HEADS_PER_STEP * V_DIM, n_prefix), lambda b, hp, i: (0, hp, 0)),
            pl.BlockSpec((1, HEADS_PER_STEP, seq, HEAD_PAD), lambda b, hp, i: (b, hp, 0, 0)),
            pl.BlockSpec((1, HEADS_PER_STEP * V_DIM, seq), lambda b, hp, i: (b, hp, 0)),
        ],
        out_specs=pl.BlockSpec((1, tq, HEADS_PER_STEP * V_DIM), lambda b, hp, i: (b, i, hp)),
        out_shape=jax.ShapeDtypeStruct((bsz, seq, ATTN_W), jnp.bfloat16),
        scratch_shapes=[pltpu.VMEM((HEADS_PER_STEP, tq // Q_SLICE, tk, Q_SLICE), f32),
                        pltpu.VMEM((HEADS_PER_STEP, tq // Q_SLICE, tk, Q_SLICE), f32),
                        pltpu.VMEM((HEADS_PER_STEP, 1, tq), f32),
                        pltpu.VMEM((HEADS_PER_STEP, 1, tq), f32),
                        pltpu.VMEM((HEADS_PER_STEP, 1, tq), f32),
                        pltpu.VMEM((HEADS_PER_STEP, V_AUG, tq), f32)],
        compiler_params=pltpu.CompilerParams(
            dimension_semantics=("arbitrary", "arbitrary", "arbitrary"),
            vmem_limit_bytes=VMEM_LIMIT_BYTES),
        name="attn_causal",
    )(q, kp, vtp, k, vt)


def _out_mlp_kernel(o_ref, y_ref, x_ref, g_attn_ref, w_out_ref, g_mlp_ref, w_up_ref, w_down_ref,
                    g_fin_ref, out_ref, hn_scr, *, final):
    c = pl.program_id(1)

    @pl.when(c == 0)
    def _():
        on = _rms(o_ref[...].astype(jnp.float32), g_attn_ref[...]).astype(jnp.bfloat16)
        h = (x_ref[...] + _dot(on, w_out_ref[0:ATTN_W, :])
             + _dot(y_ref[...], w_out_ref[ATTN_W:ATTN_W + LRU_W, :]))
        hn_scr[...] = _rms(h, g_mlp_ref[...]).astype(jnp.bfloat16)
        out_ref[...] = h

    u = _dot(hn_scr[...], w_up_ref[...])
    u = jnp.square(jnp.maximum(u, 0.0)).astype(jnp.bfloat16)
    out_ref[...] += _dot(u, w_down_ref[...])

    if final:
        @pl.when(c == pl.num_programs(1) - 1)
        def _():
            out_ref[...] = _rms(out_ref[...], g_fin_ref[...])


def _out_mlp_call(o, y, x, lw, big, g_fin, *, layer, tm, final):
    rows = x.shape[0]
    const = lambda r, c: (0, 0)
    return pl.pallas_call(
        functools.partial(_out_mlp_kernel, final=final),
        grid=(rows // tm, D_FF // F_CHUNK),
        in_specs=[
            pl.BlockSpec((tm, ATTN_W), lambda r, c: (r, 0)),
            pl.BlockSpec((tm, LRU_W), lambda r, c: (r, 0)),
            pl.BlockSpec((tm, D_MODEL), lambda r, c: (r, 0)),
            pl.BlockSpec((1, ATTN_W), const),
            pl.BlockSpec((None, D_MODEL, D_MODEL), lambda r, c: (layer, 0, 0)),
            pl.BlockSpec((1, D_MODEL), const),
            pl.BlockSpec((None, D_MODEL, F_CHUNK), lambda r, c: (layer, 0, c)),
            pl.BlockSpec((None, F_CHUNK, D_MODEL), lambda r, c: (layer, c, 0)),
            pl.BlockSpec((1, D_MODEL), const),
        ],
        out_specs=pl.BlockSpec((tm, D_MODEL), lambda r, c: (r, 0)),
        out_shape=jax.ShapeDtypeStruct((rows, D_MODEL), jnp.float32),
        scratch_shapes=[pltpu.VMEM((tm, D_MODEL), jnp.bfloat16)],
        compiler_params=pltpu.CompilerParams(
            dimension_semantics=("arbitrary", "arbitrary"), vmem_limit_bytes=VMEM_LIMIT_BYTES),
        name="out_mlp",
    )(o, y, x, lw["g_attn"], big["w_out"], lw["g_mlp"], big["w_up"], big["w_down"], g_fin)


def _rope_tables(pos, rope_lane0, pass_lanes, scale):
    half = QK_ROPE // 2
    lane = jnp.arange(LANES, dtype=jnp.int32)
    rel = lane - rope_lane0
    in_rope = (rel >= 0) & (rel < QK_ROPE)
    freq_idx = jnp.where(in_rope, rel % half, 0).astype(jnp.float32)
    inv = ROPE_THETA ** (-(2.0 * freq_idx) / QK_ROPE)
    ang = pos.astype(jnp.float32)[:, None] * inv[None, :]
    cos, sin = jnp.cos(ang), jnp.sin(ang)
    passthrough = (lane < pass_lanes).astype(jnp.float32)[None, :]
    tab_c = jnp.where(in_rope[None, :], cos, passthrough)
    tab_m = jnp.where((in_rope & (rel < half))[None, :], -sin, 0.0)
    tab_p = jnp.where((in_rope & (rel >= half))[None, :], sin, 0.0)
    return jnp.stack([tab_c, tab_m, tab_p]) * scale


def _key_tables(pos):
    return _rope_tables(pos, 0, 0, 1.0)


def _query_tables(pos):
    return _rope_tables(pos, QK_NOPE, QK_NOPE, SM_SCALE * LOG2_E)


def _block_diag_gates(w_a, w_x):
    per_half = LRU_BLOCKS // 2
    halves = []
    for s in range(2):
        bd_a = jax.scipy.linalg.block_diag(*[w_a[s * per_half + i] for i in range(per_half)])
        bd_x = jax.scipy.linalg.block_diag(*[w_x[s * per_half + i] for i in range(per_half)])
        halves.append(jnp.concatenate([bd_a, bd_x], axis=1))
    return jnp.stack(halves).astype(jnp.bfloat16)


def _layer_weights(l, norm_mix_g, w_in, q_norm_g, w_uq, kv_norm_g, w_ukv, conv_w, conv_b,
                   w_gate_a, b_gate_a, w_gate_x, b_gate_x, lru_lambda, attn_out_g, lru_out_g,
                   norm_mlp_g):
    bf = jnp.bfloat16
    row = lambda v: v.reshape(1, -1).astype(jnp.float32)
    w_in_p = jnp.concatenate(
        [w_in[l][:, :COL_KR + QK_ROPE], jnp.zeros((D_MODEL, ROPE_PAD - QK_ROPE), w_in.dtype),
         w_in[l][:, COL_KR + QK_ROPE:]], axis=1).astype(bf)
    w_uq_p = jnp.pad(w_uq[l].reshape(Q_LORA, N_HEADS, QK_NOPE + QK_ROPE),
                     ((0, 0), (0, 0), (0, HEAD_PAD - QK_NOPE - QK_ROPE))).reshape(Q_LORA, -1).astype(bf)
    w_ukv_h = w_ukv[l].reshape(KV_LORA, N_HEADS, QK_NOPE + V_DIM)
    w_uk_p = jnp.pad(w_ukv_h[:, :, :QK_NOPE],
                     ((0, 0), (0, 0), (0, HEAD_PAD - QK_NOPE))).reshape(KV_LORA, -1).astype(bf)
    w_uvt = w_ukv_h[:, :, QK_NOPE:].reshape(KV_LORA, ATTN_W).T.astype(bf)
    return {
        "g_mix": row(norm_mix_g[l]), "w_in": w_in_p, "g_q": row(q_norm_g[l]), "w_uq": w_uq_p,
        "g_kv": row(kv_norm_g[l]), "w_uk": w_uk_p, "w_uvt": w_uvt,
        "conv_w": conv_w[l].astype(jnp.float32), "conv_b": row(conv_b[l]),
        "wg": _block_diag_gates(w_gate_a[l], w_gate_x[l]),
        "bg": jnp.stack([b_gate_a[l], b_gate_x[l]]).astype(jnp.float32),
        "lam": row(lru_lambda[l]), "g_lru": row(lru_out_g[l]), "g_attn": row(attn_out_g[l]),
        "g_mlp": row(norm_mlp_g[l]),
    }


def _pad_buf(buf):
    return jnp.pad(buf, ((0, 0), (SUBLANES - (CONV_W - 1), 0), (0, 0)))


def _pad_axis(a, axis, size):
    pad = [(0, 0)] * a.ndim
    pad[axis] = (0, size - a.shape[axis])
    return jnp.pad(a, pad)


def kernel(x_prompt, x_sample, cache_ckv, cache_kpe, state_lru_h, state_conv, meta_tokens,
           norm_mix_g, w_in, q_norm_g, w_uq, kv_norm_g, w_ukv, conv_w, conv_b,
           w_gate_a, b_gate_a, w_gate_x, b_gate_x, lru_lambda, attn_out_g, lru_out_g,
           w_out, norm_mlp_g, w_up, w_down, final_norm_g):
    b_p, seq, _ = x_prompt.shape
    b_s, dec_seq, _ = x_sample.shape
    past_len = cache_ckv.shape[2]
    assert (past_len + dec_seq - 1) // CHUNK == past_len // CHUNK
    tm_p = 512
    tq_p = 1024
    tm_o = 1024
    assert seq % tm_p == 0 and seq % tq_p == 0 and (tq_p // 2) % CHUNK == 0

    meta_pos = jnp.arange(-N_META, 0, dtype=jnp.int32)
    prompt_pos = jnp.arange(seq, dtype=jnp.int32)
    sample_pos = past_len + jnp.arange(dec_seq, dtype=jnp.int32)
    tabk_m, tabk_p, tabk_s = _key_tables(meta_pos), _key_tables(prompt_pos), _key_tables(sample_pos)
    tabq_m, tabq_p, tabq_s = _query_tables(meta_pos), _query_tables(prompt_pos), _query_tables(sample_pos)
    g_fin = final_norm_g.reshape(1, -1).astype(jnp.float32)

    cache_kpe_pad = _pad_axis(cache_kpe, 3, ROPE_PAD)
    big = {"w_out": w_out.astype(jnp.bfloat16), "w_up": w_up.astype(jnp.bfloat16),
           "w_down": w_down.astype(jnp.bfloat16)}
    tq_small = LANES

    h_meta = meta_tokens[None].astype(jnp.float32)
    h_p, h_s = x_prompt, x_sample
    zero_h = jnp.zeros((1, 1, LRU_W), jnp.float32)
    zero_buf = jnp.zeros((1, SUBLANES, LRU_W), jnp.float32)
    outs = {name: [] for name in ("ckv_p", "kpe_p", "lru_p", "conv_p", "ckv_s", "kpe_s", "lru_s", "conv_s")}
    for l in range(DEPTH):
        lw = _layer_weights(l, norm_mix_g, w_in, q_norm_g, w_uq, kv_norm_g, w_ukv, conv_w, conv_b,
                            w_gate_a, b_gate_a, w_gate_x, b_gate_x, lru_lambda, attn_out_g,
                            lru_out_g, norm_mlp_g)
        last = l + 1 == DEPTH
        mq, mk, mvt, m_ckv, m_kpe, m_y, m_h, m_buf = _mixer_call(
            h_meta, zero_h, zero_buf, tabq_m, tabk_m, lw, tm=N_META, shared_state=True)
        pq, pk, pvt, p_ckv, p_kpe, p_y, p_h, p_buf = _mixer_call(
            h_p, m_h, _pad_buf(m_buf), tabq_p, tabk_p, lw, tm=tm_p, shared_state=True)
        mk_p, mvt_p = _pad_axis(mk, 2, LANES), _pad_axis(mvt, 2, LANES)
        p_o = _attn_causal_call(pq, mk_p, mvt_p, pk, pvt, tq=tq_p, n_prefix_valid=N_META)
        h_p = _out_mlp_call(p_o.reshape(b_p * seq, ATTN_W), p_y.reshape(b_p * seq, LRU_W),
                            h_p.reshape(b_p * seq, D_MODEL), lw, big, g_fin, layer=l, tm=tm_o,
                            final=last).reshape(b_p, seq, D_MODEL)
        sq, sk, svt, s_ckv, s_kpe, s_y, s_h, s_buf = _mixer_call(
            h_s, state_lru_h[l][:, None, :], _pad_buf(state_conv[l]), tabq_s, tabk_s, lw,
            tm=dec_seq, shared_state=False)
        s_o = _attn_sample_call(sq, m_ckv, cache_ckv, s_ckv, m_kpe, cache_kpe_pad, s_kpe, lw,
                                layer=l)[:, :dec_seq]
        h_s = _out_mlp_call(s_o.reshape(b_s * dec_seq, ATTN_W), s_y.reshape(b_s * dec_seq, LRU_W),
                            h_s.reshape(b_s * dec_seq, D_MODEL), lw, big, g_fin, layer=l,
                            tm=b_s * dec_seq, final=last).reshape(b_s, dec_seq, D_MODEL)
        if not last:
            m_o = _attn_dense_call(_pad_axis(mq, 2, tq_small), mk_p, mvt_p,
                                   n_prefix_valid=N_META)[:, :N_META]
            h_meta = _out_mlp_call(m_o.reshape(N_META, ATTN_W), m_y.reshape(N_META, LRU_W),
                                   h_meta.reshape(N_META, D_MODEL), lw, big, g_fin, layer=l,
                                   tm=N_META, final=False).reshape(1, N_META, D_MODEL)
        outs["ckv_p"].append(p_ckv); outs["kpe_p"].append(p_kpe[:, :, :QK_ROPE])
        outs["lru_p"].append(p_h[:, 0]); outs["conv_p"].append(p_buf)
        outs["ckv_s"].append(s_ckv); outs["kpe_s"].append(s_kpe[:, :, :QK_ROPE])
        outs["lru_s"].append(s_h[:, 0]); outs["conv_s"].append(s_buf)

    return (h_p, h_s, jnp.stack(outs["ckv_p"]), jnp.stack(outs["kpe_p"]), jnp.stack(outs["lru_p"]),
            jnp.stack(outs["conv_p"]), jnp.stack(outs["ckv_s"]), jnp.stack(outs["kpe_s"]),
            jnp.stack(outs["lru_s"]), jnp.stack(outs["conv_s"]))
```

```python
import functools

import jax
import jax.numpy as jnp
from jax import lax
from jax.experimental import pallas as pl
from jax.experimental.pallas import tpu as pltpu

D_MODEL = 1024
DEPTH = 2
CHUNK = 64
N_META = 16
ATTN_W = 512
LRU_W = 512
V_DIM = 64
N_HEADS = 8
QK_NOPE = 64
QK_ROPE = 32
Q_LORA = 768
KV_LORA = 256
LRU_BLOCKS = 8
LRU_BW = LRU_W // LRU_BLOCKS
LRU_C = 8.0
CONV_W = 4
D_FF = 4 * D_MODEL
ROPE_THETA = 10000.0
EPS = 1e-6
SM_SCALE = (QK_NOPE + QK_ROPE) ** -0.5
NEG_INF = -1e30
LOG2_E = 1.4426950408889634

LANES = 128
SUBLANES = 8
HEAD_PAD = LANES
VMEM_LIMIT_BYTES = 56 * 1024 * 1024

ROPE_PAD = LANES
COL_CQ = 0
COL_CKV = Q_LORA
COL_KR = Q_LORA + KV_LORA
COL_XR = COL_KR + ROPE_PAD
COL_GR = COL_XR + LRU_W
IN_W_PAD = COL_GR + LRU_W

HEADS_PER_STEP = 2
ONES_ROWS = 16
V_AUG = V_DIM + ONES_ROWS
F_CHUNK = 1024
Q_SLICE = 256
PROMPT_MIXER_ROWS = 512
PROMPT_QUERY_ROWS = 1024
PROMPT_MLP_ROWS = 1024


def _rms(x, g):
    return x * lax.rsqrt(jnp.mean(x * x, axis=-1, keepdims=True) + EPS) * g


def _rope_lanes(x, tab_ref):
    half = QK_ROPE // 2
    return (x * tab_ref[0] + pltpu.roll(x, LANES - half, 1) * tab_ref[1]
            + pltpu.roll(x, half, 1) * tab_ref[2])


def _dot(a, b):
    return jnp.dot(a, b, preferred_element_type=jnp.float32)


def _dot_nt(a, b):
    return lax.dot_general(a, b, (((1,), (1,)), ((), ())), preferred_element_type=jnp.float32)


def _mixer_kernel(x_ref, h0_ref, buf0_ref, tabq_ref, tabk_ref, g_mix_ref, w_in_ref, g_q_ref,
                  w_uq_ref, g_kv_ref, w_uk_ref, w_uvt_ref, conv_w_ref, conv_b_ref, wg_ref, bg_ref,
                  lam_ref, g_lru_ref,
                  q_ref, k_ref, vt_ref, ckv_ref, kpe_ref, y_ref, hlast_ref, buf_ref,
                  tail_scr, hcar_scr, *, tm):
    t = pl.program_id(1)

    @pl.when(t == 0)
    def _():
        tail_scr[...] = buf0_ref[0]
        hcar_scr[...] = h0_ref[0]

    x = x_ref[0]
    xn = _rms(x, g_mix_ref[...]).astype(jnp.bfloat16)
    z = _dot(xn, w_in_ref[...])

    cqn = _rms(z[:, COL_CQ:COL_CQ + Q_LORA], g_q_ref[...]).astype(jnp.bfloat16)
    q = _dot(cqn, w_uq_ref[...])
    for h in range(N_HEADS):
        qh = q[:, h * HEAD_PAD:(h + 1) * HEAD_PAD]
        q_ref[0, h] = _rope_lanes(qh, tabq_ref).astype(jnp.bfloat16)

    ckv = _rms(z[:, COL_CKV:COL_CKV + KV_LORA], g_kv_ref[...])
    ckv_ref[0] = ckv
    ckv_b = ckv.astype(jnp.bfloat16)
    kpe = _rope_lanes(z[:, COL_KR:COL_KR + ROPE_PAD], tabk_ref)
    kpe_ref[0] = kpe
    kpe_shift = pltpu.roll(kpe, QK_NOPE, 1)
    kk = _dot(ckv_b, w_uk_ref[...])
    for h in range(N_HEADS):
        k_ref[0, h] = (kk[:, h * HEAD_PAD:(h + 1) * HEAD_PAD] + kpe_shift).astype(jnp.bfloat16)
    vt_ref[0] = _dot_nt(w_uvt_ref[...], ckv_b).astype(jnp.bfloat16)

    x_r = z[:, COL_XR:COL_XR + LRU_W]
    g_r = z[:, COL_GR:COL_GR + LRU_W]
    n_slabs = tm // SUBLANES
    rows8 = lax.broadcasted_iota(jnp.int32, (SUBLANES, LRU_W), 0)
    slabs = [tail_scr[...]] + [x_r[g * SUBLANES:(g + 1) * SUBLANES, :] for g in range(n_slabs)]
    xc = conv_b_ref[...] + x_r * conv_w_ref[CONV_W - 1:CONV_W, :]
    for back in range(1, CONV_W):
        rolled = [pltpu.roll(s, back, 0) for s in slabs]
        shifted = jnp.concatenate(
            [jnp.where(rows8 < back, rolled[g], rolled[g + 1]) for g in range(n_slabs)], axis=0)
        xc = xc + shifted * conv_w_ref[CONV_W - 1 - back:CONV_W - back, :]
    tail_scr[...] = slabs[-1]
    buf_ref[0] = tail_scr[SUBLANES - (CONV_W - 1):SUBLANES, :]

    half_w = LRU_W // 2
    gates = [_dot(xc[:, s * half_w:(s + 1) * half_w].astype(jnp.bfloat16), wg_ref[s]) for s in range(2)]
    pre_a = jnp.concatenate([gates[0][:, :half_w], gates[1][:, :half_w]], axis=1)
    pre_x = jnp.concatenate([gates[0][:, half_w:], gates[1][:, half_w:]], axis=1)
    r = jax.nn.sigmoid(pre_a + bg_ref[0:1, :])
    gi = jax.nn.sigmoid(pre_x + bg_ref[1:2, :])
    neg_lam = -lam_ref[...]
    softplus = jnp.maximum(neg_lam, 0.0) + jnp.log1p(jnp.exp(-jnp.abs(neg_lam)))
    log_a = (-LRU_C) * r * softplus
    a = jnp.exp(log_a)
    one_m_a2 = -jnp.tanh(log_a) * (a * a + 1.0)
    root = jnp.where(one_m_a2 > 0.0, one_m_a2 * lax.rsqrt(one_m_a2), 0.0)
    b = root * (gi * xc)

    carry = jnp.broadcast_to(hcar_scr[...], (SUBLANES, LRU_W))
    h_slabs = []
    for g in range(n_slabs):
        a_g = a[g * SUBLANES:(g + 1) * SUBLANES, :]
        b_g = b[g * SUBLANES:(g + 1) * SUBLANES, :]
        shift = 1
        while shift < SUBLANES:
            keep = rows8 >= shift
            a_prev = jnp.where(keep, pltpu.roll(a_g, shift, 0), 1.0)
            b_prev = jnp.where(keep, pltpu.roll(b_g, shift, 0), 0.0)
            b_g = a_g * b_prev + b_g
            a_g = a_g * a_prev
            shift *= 2
        h_g = a_g * carry + b_g
        h_slabs.append(h_g)
        carry = jnp.broadcast_to(h_g[SUBLANES - 1:SUBLANES, :], (SUBLANES, LRU_W))
    hh = jnp.concatenate(h_slabs, axis=0)
    h_last = carry[0:1, :]
    hcar_scr[...] = h_last
    hlast_ref[0] = h_last
    y = hh * jax.nn.gelu(g_r)
    y_ref[0] = _rms(y, g_lru_ref[...]).astype(jnp.bfloat16)


def _mixer_call(x, h0, buf0p, tabq, tabk, lw, *, tm, shared_state):
    bsz, seq, _ = x.shape
    nt = seq // tm
    const2 = lambda b, t: (0, 0)
    const3 = lambda b, t: (0, 0, 0)
    state_idx = (lambda b, t: (0, 0, 0)) if shared_state else (lambda b, t: (b, 0, 0))
    in_specs = [
        pl.BlockSpec((1, tm, D_MODEL), lambda b, t: (b, t, 0)),
        pl.BlockSpec((1, 1, LRU_W), state_idx),
        pl.BlockSpec((1, SUBLANES, LRU_W), state_idx),
        pl.BlockSpec((3, tm, LANES), lambda b, t: (0, t, 0)),
        pl.BlockSpec((3, tm, LANES), lambda b, t: (0, t, 0)),
        pl.BlockSpec((1, D_MODEL), const2),
        pl.BlockSpec((D_MODEL, IN_W_PAD), const2),
        pl.BlockSpec((1, Q_LORA), const2),
        pl.BlockSpec((Q_LORA, N_HEADS * HEAD_PAD), const2),
        pl.BlockSpec((1, KV_LORA), const2),
        pl.BlockSpec((KV_LORA, N_HEADS * HEAD_PAD), const2),
        pl.BlockSpec((ATTN_W, KV_LORA), const2),
        pl.BlockSpec((CONV_W, LRU_W), const2),
        pl.BlockSpec((1, LRU_W), const2),
        pl.BlockSpec((2, LRU_W // 2, LRU_W), const3),
        pl.BlockSpec((2, LRU_W), const2),
        pl.BlockSpec((1, LRU_W), const2),
        pl.BlockSpec((1, LRU_W), const2),
    ]
    out_shape = [
        jax.ShapeDtypeStruct((bsz, N_HEADS, seq, HEAD_PAD), jnp.bfloat16),
        jax.ShapeDtypeStruct((bsz, N_HEADS, seq, HEAD_PAD), jnp.bfloat16),
        jax.ShapeDtypeStruct((bsz, ATTN_W, seq), jnp.bfloat16),
        jax.ShapeDtypeStruct((bsz, seq, KV_LORA), jnp.float32),
        jax.ShapeDtypeStruct((bsz, seq, ROPE_PAD), jnp.float32),
        jax.ShapeDtypeStruct((bsz, seq, LRU_W), jnp.bfloat16),
        jax.ShapeDtypeStruct((bsz, 1, LRU_W), jnp.float32),
        jax.ShapeDtypeStruct((bsz, CONV_W - 1, LRU_W), jnp.float32),
    ]
    out_specs = [
        pl.BlockSpec((1, N_HEADS, tm, HEAD_PAD), lambda b, t: (b, 0, t, 0)),
        pl.BlockSpec((1, N_HEADS, tm, HEAD_PAD), lambda b, t: (b, 0, t, 0)),
        pl.BlockSpec((1, ATTN_W, tm), lambda b, t: (b, 0, t)),
        pl.BlockSpec((1, tm, KV_LORA), lambda b, t: (b, t, 0)),
        pl.BlockSpec((1, tm, ROPE_PAD), lambda b, t: (b, t, 0)),
        pl.BlockSpec((1, tm, LRU_W), lambda b, t: (b, t, 0)),
        pl.BlockSpec((1, 1, LRU_W), lambda b, t: (b, 0, 0)),
        pl.BlockSpec((1, CONV_W - 1, LRU_W), lambda b, t: (b, 0, 0)),
    ]
    return pl.pallas_call(
        functools.partial(_mixer_kernel, tm=tm),
        grid=(bsz, nt),
        in_specs=in_specs,
        out_specs=out_specs,
        out_shape=out_shape,
        scratch_shapes=[pltpu.VMEM((SUBLANES, LRU_W), jnp.float32),
                        pltpu.VMEM((1, LRU_W), jnp.float32)],
        compiler_params=pltpu.CompilerParams(
            dimension_semantics=("arbitrary", "arbitrary"), vmem_limit_bytes=VMEM_LIMIT_BYTES),
        name="mixer",
    )(x, h0, buf0p, tabq, tabk, lw["g_mix"], lw["w_in"], lw["g_q"], lw["w_uq"], lw["g_kv"],
      lw["w_uk"], lw["w_uvt"], lw["conv_w"], lw["conv_b"], lw["wg"], lw["bg"], lw["lam"], lw["g_lru"])


def _attn_sample_kernel(q_ref, ckv_m_ref, ckv_c_ref, ckv_s_ref, kpe_m_ref, kpe_c_ref, kpe_s_ref,
                        w_uk_ref, w_uvt_ref, o_ref, *, n_keys_pad):
    ckv_parts = [ckv_m_ref[0], ckv_c_ref[0, 0], ckv_s_ref[0]]
    kpe_parts = [kpe_m_ref[0], kpe_c_ref[0, 0], kpe_s_ref[0]]
    n_valid = sum(p.shape[0] for p in ckv_parts)
    pad = n_keys_pad - n_valid
    ckv = jnp.concatenate(ckv_parts + [jnp.zeros((pad, KV_LORA), jnp.float32)], axis=0)
    kpe = jnp.concatenate(kpe_parts + [jnp.zeros((pad, ROPE_PAD), jnp.float32)], axis=0)
    ckv_b = ckv.astype(jnp.bfloat16)
    kpe_shift = pltpu.roll(kpe, QK_NOPE, 1)
    kk = _dot(ckv_b, w_uk_ref[...])
    vt = _dot_nt(w_uvt_ref[...], ckv_b).astype(jnp.bfloat16)
    n_q = q_ref.shape[2]
    q_pad = jnp.zeros((LANES - n_q, HEAD_PAD), jnp.bfloat16)
    q_tiles = [jnp.concatenate([q_ref[0, h], q_pad], axis=0) for h in range(N_HEADS)]
    zero_q = jnp.zeros((LANES, HEAD_PAD), jnp.bfloat16)
    ones_k = jnp.ones((ONES_ROWS, n_keys_pad), jnp.bfloat16)
    kpe_pair = jnp.concatenate([kpe_shift, kpe_shift], axis=1)
    key_idx = lax.broadcasted_iota(jnp.int32, (n_keys_pad, 2 * LANES), 0)
    accs = []
    for pair in range(N_HEADS // 2):
        h_a, h_b = 2 * pair, 2 * pair + 1
        k_pair = (kk[:, h_a * HEAD_PAD:(h_b + 1) * HEAD_PAD] + kpe_pair).astype(jnp.bfloat16)
        q_blockdiag = jnp.concatenate([jnp.concatenate([q_tiles[h_a], zero_q], axis=1),
                                       jnp.concatenate([zero_q, q_tiles[h_b]], axis=1)], axis=0)
        st = _dot_nt(k_pair, q_blockdiag)
        st = jnp.where(key_idx < n_valid, st, NEG_INF)
        p = jnp.exp2(st - _col_max(st)).astype(jnp.bfloat16)
        vt_pair = jnp.concatenate([vt[h_a * V_DIM:(h_a + 1) * V_DIM, :], ones_k,
                                   vt[h_b * V_DIM:(h_b + 1) * V_DIM, :], ones_k], axis=0)
        acc = _dot(vt_pair, p)
        accs.append(acc[:V_AUG, :LANES])
        accs.append(acc[V_AUG:, LANES:])
    _store_normalised(accs, o_ref)


def _attn_sample_call(q, ckv_m, ckv_cache, ckv_s, kpe_m, kpe_cache, kpe_s, lw, *, layer):
    bsz, _, dec, _ = q.shape
    past = ckv_cache.shape[2]
    n_meta = ckv_m.shape[1]
    n_keys_pad = -(-(n_meta + past + dec) // LANES) * LANES
    return pl.pallas_call(
        functools.partial(_attn_sample_kernel, n_keys_pad=n_keys_pad),
        grid=(bsz,),
        in_specs=[
            pl.BlockSpec((1, N_HEADS, dec, HEAD_PAD), lambda b: (b, 0, 0, 0)),
            pl.BlockSpec((1, n_meta, KV_LORA), lambda b: (0, 0, 0)),
            pl.BlockSpec((1, 1, past, KV_LORA), lambda b: (layer, b, 0, 0)),
            pl.BlockSpec((1, dec, KV_LORA), lambda b: (b, 0, 0)),
            pl.BlockSpec((1, n_meta, ROPE_PAD), lambda b: (0, 0, 0)),
            pl.BlockSpec((1, 1, past, ROPE_PAD), lambda b: (layer, b, 0, 0)),
            pl.BlockSpec((1, dec, ROPE_PAD), lambda b: (b, 0, 0)),
            pl.BlockSpec((KV_LORA, N_HEADS * HEAD_PAD), lambda b: (0, 0)),
            pl.BlockSpec((ATTN_W, KV_LORA), lambda b: (0, 0)),
        ],
        out_specs=pl.BlockSpec((1, LANES, ATTN_W), lambda b: (b, 0, 0)),
        out_shape=jax.ShapeDtypeStruct((bsz, LANES, ATTN_W), jnp.bfloat16),
        compiler_params=pltpu.CompilerParams(
            dimension_semantics=("arbitrary",), vmem_limit_bytes=VMEM_LIMIT_BYTES),
        name="attn_sample",
    )(q, ckv_m, ckv_cache, ckv_s, kpe_m, kpe_cache, kpe_s, lw["w_uk"], lw["w_uvt"])


def _col_max(st):
    rows, cols = st.shape
    slab = SUBLANES
    if rows % slab == 0 and rows > slab:
        st = jnp.max(st.reshape(rows // slab, slab, cols), axis=0)
    return jnp.max(st, axis=0, keepdims=True)


def _dense_state(q_tiles, k_tiles, vt_tiles, n_valid):
    n_keys = k_tiles[0].shape[0]
    tq = q_tiles[0].shape[0]
    n_used = -(-n_valid // ONES_ROWS) * ONES_ROWS
    ones_p = jnp.ones((ONES_ROWS, n_keys), jnp.bfloat16)
    ms, accs = [], []
    for q_t, k_t, vt_t in zip(q_tiles, k_tiles, vt_tiles):
        st = _dot_nt(k_t[:n_used], q_t)
        if n_valid < n_used:
            key_idx = lax.broadcasted_iota(jnp.int32, (n_used, tq), 0)
            st = jnp.where(key_idx < n_valid, st, NEG_INF)
        m = _col_max(st)
        p = jnp.exp2(st - m).astype(jnp.bfloat16)
        if n_used < n_keys:
            p = jnp.concatenate([p, jnp.zeros((n_keys - n_used, tq), jnp.bfloat16)], axis=0)
        ms.append(m)
        accs.append(_dot(jnp.concatenate([vt_t, ones_p], axis=0), p))
    return ms, accs


def _prefix_state(q_tiles, kp_ref, vtp_ref, n_valid):
    heads = range(len(q_tiles))
    return _dense_state(q_tiles, [kp_ref[0, hh] for hh in heads],
                        [vtp_ref[0, hh * V_DIM:(hh + 1) * V_DIM, :] for hh in heads], n_valid)


def _store_normalised(accs, o_ref):
    outs = [acc[:V_DIM, :] / acc[V_DIM:V_DIM + 1, :] for acc in accs]
    o_ref[0] = jnp.concatenate(outs, axis=0).T.astype(jnp.bfloat16)


def _attn_dense_kernel(q_ref, kp_ref, vtp_ref, o_ref, *, n_valid):
    q_tiles = [q_ref[0, hh] for hh in range(q_ref.shape[1])]
    _, accs = _prefix_state(q_tiles, kp_ref, vtp_ref, n_valid)
    _store_normalised(accs, o_ref)


def _attn_causal_kernel(q_ref, kp_ref, vtp_ref, k_ref, vt_ref, o_ref,
                        sa_scr, sb_scr, ma_scr, mb_scr, m_scr, acc_scr,
                        *, tq, tk, n_prefix_valid):
    qi = pl.program_id(2)
    n_q = pl.num_programs(2)

    def q_rows(hh, q_idx, q0, qn):
        return q_ref[0, hh, pl.ds(pl.multiple_of(q_idx * tq + q0, tk), qn), :]

    m_scr[...] = jnp.full(m_scr.shape, NEG_INF, jnp.float32)
    acc_scr[...] = jnp.zeros(acc_scr.shape, jnp.float32)
    ones_k = jnp.ones((ONES_ROWS, tk), jnp.bfloat16)

    def scores(tile, s_ref, mx_ref, q_idx=qi, q0=0, qn=Q_SLICE, kn=tk, heads=range(HEADS_PER_STEP)):
        start = pl.multiple_of(tile * tk, tk)
        for hh in heads:
            st = _dot_nt(k_ref[0, hh, pl.ds(start, kn), :], q_rows(hh, q_idx, q0, qn))
            s_ref[hh, q0 // Q_SLICE, 0:kn, :] = st
            mx_ref[hh, :, q0:q0 + qn] = _col_max(st)

    def softmax_pv(tile, s_ref, mx_ref, mask=None, q0=0, qn=Q_SLICE, kn=tk, heads=range(HEADS_PER_STEP)):
        start = pl.multiple_of(tile * tk, tk)
        for hh in heads:
            st = s_ref[hh, q0 // Q_SLICE, 0:kn, :]
            if mask is None:
                m_blk = mx_ref[hh, :, q0:q0 + qn]
            else:
                st = jnp.where(mask, st, NEG_INF)
                m_blk = _col_max(st)
            m_old = m_scr[hh, :, q0:q0 + qn]
            m_new = jnp.maximum(m_old, m_blk)
            p = jnp.exp2(st - m_new).astype(jnp.bfloat16)
            vt_t = vt_ref[0, hh * V_DIM:(hh + 1) * V_DIM, pl.ds(start, kn)]
            vt_aug = jnp.concatenate([vt_t, ones_k[:, :kn]], axis=0)
            acc_scr[hh, :, q0:q0 + qn] = (jnp.exp2(m_old - m_new) * acc_scr[hh, :, q0:q0 + qn]
                                          + _dot(vt_aug, p))
            m_scr[hh, :, q0:q0 + qn] = m_new

    @pl.when(qi == 0)
    def _():
        for q0 in range(0, tq, Q_SLICE):
            scores(0, sa_scr, ma_scr, q0=q0)

    def tile_pair(t0):
        for hh in range(HEADS_PER_STEP):
            for q0 in range(0, tq, Q_SLICE):
                scores(t0 + 1, sb_scr, mb_scr, heads=(hh,), q0=q0, qn=Q_SLICE)
                softmax_pv(t0, sa_scr, ma_scr, heads=(hh,), q0=q0, qn=Q_SLICE)
        for hh in range(HEADS_PER_STEP):
            for q0 in range(0, tq, Q_SLICE):
                scores(t0 + 2, sa_scr, ma_scr, heads=(hh,), q0=q0, qn=Q_SLICE)
                softmax_pv(t0 + 1, sb_scr, mb_scr, heads=(hh,), q0=q0, qn=Q_SLICE)

    def body(jj, carry):
        tile_pair(4 * jj)
        tile_pair(4 * jj + 2)
        return carry

    lax.fori_loop(0, qi // 2, body, 0)

    @pl.when(qi % 2 == 1)
    def _():
        tile_pair(2 * (qi - 1))

    d1 = 2 * qi
    key_chunk = lax.broadcasted_iota(jnp.int32, (tk, Q_SLICE), 0) // CHUNK
    qry_local = lax.broadcasted_iota(jnp.int32, (tk, Q_SLICE), 1)

    def visible_keys(q_off):
        return min(tk, q_off + Q_SLICE)

    def chunk_mask(q_off):
        if q_off >= tk:
            return None
        return (key_chunk <= (qry_local + q_off) // CHUNK)[:visible_keys(q_off)]

    for hh in range(HEADS_PER_STEP):
        for q0 in range(0, tq, Q_SLICE):
            if q0 >= tk:
                scores(d1 + 1, sb_scr, mb_scr, heads=(hh,), q0=q0, qn=Q_SLICE, kn=visible_keys(q0 - tk))
            softmax_pv(d1, sa_scr, ma_scr, mask=chunk_mask(q0), heads=(hh,), q0=q0, qn=Q_SLICE,
                       kn=visible_keys(q0))
    q_next = jnp.minimum(qi + 1, n_q - 1)
    for hh in range(HEADS_PER_STEP):
        for q0 in range(0, tq, Q_SLICE):
            scores(0, sa_scr, ma_scr, q_idx=q_next, heads=(hh,), q0=q0, qn=Q_SLICE)
            if q0 >= tk:
                softmax_pv(d1 + 1, sb_scr, mb_scr, mask=chunk_mask(q0 - tk), heads=(hh,),
                           q0=q0, qn=Q_SLICE, kn=visible_keys(q0 - tk))
    ms_p, accs_p = _prefix_state([q_rows(hh, qi, 0, tq) for hh in range(HEADS_PER_STEP)],
                                 kp_ref, vtp_ref, n_prefix_valid)
    accs = []
    for hh in range(HEADS_PER_STEP):
        m_run = m_scr[hh]
        m_all = jnp.maximum(m_run, ms_p[hh])
        accs.append(jnp.exp2(m_run - m_all) * acc_scr[hh] + jnp.exp2(ms_p[hh] - m_all) * accs_p[hh])
    _store_normalised(accs, o_ref)


def _attn_dense_call(q, kp, vtp, *, n_prefix_valid):
    bsz, _, tq, _ = q.shape
    n_prefix = kp.shape[2]
    return pl.pallas_call(
        functools.partial(_attn_dense_kernel, n_valid=n_prefix_valid),
        grid=(bsz,),
        in_specs=[
            pl.BlockSpec((1, N_HEADS, tq, HEAD_PAD), lambda b: (b, 0, 0, 0)),
            pl.BlockSpec((1, N_HEADS, n_prefix, HEAD_PAD), lambda b: (b, 0, 0, 0)),
            pl.BlockSpec((1, ATTN_W, n_prefix), lambda b: (b, 0, 0)),
        ],
        out_specs=pl.BlockSpec((1, tq, ATTN_W), lambda b: (b, 0, 0)),
        out_shape=jax.ShapeDtypeStruct((bsz, tq, ATTN_W), jnp.bfloat16),
        compiler_params=pltpu.CompilerParams(
            dimension_semantics=("arbitrary",), vmem_limit_bytes=VMEM_LIMIT_BYTES),
        name="attn_dense",
    )(q, kp, vtp)


def _attn_causal_call(q, kp, vtp, k, vt, *, tq, n_prefix_valid):
    bsz, _, seq, _ = q.shape
    n_prefix = kp.shape[2]
    tk = tq // 2
    f32 = jnp.float32
    return pl.pallas_call(
        functools.partial(_attn_causal_kernel, tq=tq, tk=tk, n_prefix_valid=n_prefix_valid),
        grid=(bsz, N_HEADS // HEADS_PER_STEP, seq // tq),
        in_specs=[
            pl.BlockSpec((1, HEADS_PER_STEP, seq, HEAD_PAD), lambda b, hp, i: (b, hp, 0, 0)),
            pl.BlockSpec((1, HEADS_PER_STEP, n_prefix, HEAD_PAD), lambda b, hp, i: (0, hp, 0, 0)),
            pl.BlockSpec((1, HEADS_PER_STEP * V_DIM, n_prefix), lambda b, hp, i: (0, hp, 0)),
            pl.BlockSpec((1, HEADS_PER_STEP, seq, HEAD_PAD), lambda b, hp, i: (b, hp, 0, 0)),
            pl.BlockSpec((1, HEADS_PER_STEP * V_DIM, seq), lambda b, hp, i: (b, hp, 0)),
        ],
        out_specs=pl.BlockSpec((1, tq, HEADS_PER_STEP * V_DIM), lambda b, hp, i: (b, i, hp)),
        out_shape=jax.ShapeDtypeStruct((bsz, seq, ATTN_W), jnp.bfloat16),
        scratch_shapes=[pltpu.VMEM((HEADS_PER_STEP, tq // Q_SLICE, tk, Q_SLICE), f32),
                        pltpu.VMEM((HEADS_PER_STEP, tq // Q_SLICE, tk, Q_SLICE), f32),
                        pltpu.VMEM((HEADS_PER_STEP, 1, tq), f32),
                        pltpu.VMEM((HEADS_PER_STEP, 1, tq), f32),
                        pltpu.VMEM((HEADS_PER_STEP, 1, tq), f32),
                        pltpu.VMEM((HEADS_PER_STEP, V_AUG, tq), f32)],
        compiler_params=pltpu.CompilerParams(
            dimension_semantics=("arbitrary", "arbitrary", "arbitrary"),
            vmem_limit_bytes=VMEM_LIMIT_BYTES),
        name="attn_causal",
    )(q, kp, vtp, k, vt)


def _out_mlp_kernel(o_ref, y_ref, x_ref, g_attn_ref, w_out_ref, g_mlp_ref, w_up_ref, w_down_ref,
                    g_fin_ref, out_ref, hn_scr, *, final):
    c = pl.program_id(1)

    @pl.when(c == 0)
    def _():
        on = _rms(o_ref[...].astype(jnp.float32), g_attn_ref[...]).astype(jnp.bfloat16)
        h = (x_ref[...] + _dot(on, w_out_ref[0:ATTN_W, :])
             + _dot(y_ref[...], w_out_ref[ATTN_W:ATTN_W + LRU_W, :]))
        hn_scr[...] = _rms(h, g_mlp_ref[...]).astype(jnp.bfloat16)
        out_ref[...] = h

    u = _dot(hn_scr[...], w_up_ref[...])
    u = jnp.square(jnp.maximum(u, 0.0)).astype(jnp.bfloat16)
    out_ref[...] += _dot(u, w_down_ref[...])

    if final:
        @pl.when(c == pl.num_programs(1) - 1)
        def _():
            out_ref[...] = _rms(out_ref[...], g_fin_ref[...])


def _out_mlp_call(o, y, x, lw, big, g_fin, *, layer, tm, final):
    rows = x.shape[0]
    const = lambda r, c: (0, 0)
    return pl.pallas_call(
        functools.partial(_out_mlp_kernel, final=final),
        grid=(rows // tm, D_FF // F_CHUNK),
        in_specs=[
            pl.BlockSpec((tm, ATTN_W), lambda r, c: (r, 0)),
            pl.BlockSpec((tm, LRU_W), lambda r, c: (r, 0)),
            pl.BlockSpec((tm, D_MODEL), lambda r, c: (r, 0)),
            pl.BlockSpec((1, ATTN_W), const),
            pl.BlockSpec((None, D_MODEL, D_MODEL), lambda r, c: (layer, 0, 0)),
            pl.BlockSpec((1, D_MODEL), const),
            pl.BlockSpec((None, D_MODEL, F_CHUNK), lambda r, c: (layer, 0, c)),
            pl.BlockSpec((None, F_CHUNK, D_MODEL), lambda r, c: (layer, c, 0)),
            pl.BlockSpec((1, D_MODEL), const),
        ],
        out_specs=pl.BlockSpec((tm, D_MODEL), lambda r, c: (r, 0)),
        out_shape=jax.ShapeDtypeStruct((rows, D_MODEL), jnp.float32),
        scratch_shapes=[pltpu.VMEM((tm, D_MODEL), jnp.bfloat16)],
        compiler_params=pltpu.CompilerParams(
            dimension_semantics=("arbitrary", "arbitrary"), vmem_limit_bytes=VMEM_LIMIT_BYTES),
        name="out_mlp",
    )(o, y, x, lw["g_attn"], big["w_out"], lw["g_mlp"], big["w_up"], big["w_down"], g_fin)


def _rope_tables(pos, rope_lane0, pass_lanes, scale):
    half = QK_ROPE // 2
    n = pos.shape[0]
    inv = ROPE_THETA ** (-jnp.arange(0, QK_ROPE, 2, dtype=jnp.float32) / QK_ROPE)
    ang = pos.astype(jnp.float32)[:, None] * inv[None, :]
    cos, sin = jnp.cos(ang) * scale, jnp.sin(ang) * scale
    fill = lambda width, value=0.0: jnp.full((n, width), value, jnp.float32)
    tail = LANES - rope_lane0 - QK_ROPE
    tab_c = jnp.concatenate([fill(pass_lanes, scale), fill(rope_lane0 - pass_lanes), cos, cos, fill(tail)],
                            axis=1)
    tab_m = jnp.concatenate([fill(rope_lane0), -sin, fill(half + tail)], axis=1)
    tab_p = jnp.concatenate([fill(rope_lane0 + half), sin, fill(tail)], axis=1)
    return jnp.stack([tab_c, tab_m, tab_p])


def _key_tables(pos):
    return _rope_tables(pos, 0, 0, 1.0)


def _query_tables(pos):
    return _rope_tables(pos, QK_NOPE, QK_NOPE, SM_SCALE * LOG2_E)


def _block_diag_gates(w_a, w_x):
    per_half = LRU_BLOCKS // 2
    halves = []
    for s in range(2):
        bd_a = jax.scipy.linalg.block_diag(*[w_a[s * per_half + i] for i in range(per_half)])
        bd_x = jax.scipy.linalg.block_diag(*[w_x[s * per_half + i] for i in range(per_half)])
        halves.append(jnp.concatenate([bd_a, bd_x], axis=1))
    return jnp.stack(halves).astype(jnp.bfloat16)


def _layer_weights(l, norm_mix_g, w_in, q_norm_g, w_uq, kv_norm_g, w_ukv, conv_w, conv_b,
                   w_gate_a, b_gate_a, w_gate_x, b_gate_x, lru_lambda, attn_out_g, lru_out_g,
                   norm_mlp_g):
    bf = jnp.bfloat16
    row = lambda v: v.reshape(1, -1).astype(jnp.float32)
    split = COL_KR + QK_ROPE
    w_in_p = (jnp.pad(w_in[l][:, :split], ((0, 0), (0, IN_W_PAD - split)))
              + jnp.pad(w_in[l][:, split:], ((0, 0), (split + ROPE_PAD - QK_ROPE, 0)))).astype(bf)
    w_uq_p = jnp.pad(w_uq[l].reshape(Q_LORA, N_HEADS, QK_NOPE + QK_ROPE),
                     ((0, 0), (0, 0), (0, HEAD_PAD - QK_NOPE - QK_ROPE))).reshape(Q_LORA, -1).astype(bf)
    w_ukv_h = w_ukv[l].reshape(KV_LORA, N_HEADS, QK_NOPE + V_DIM)
    w_uk_p = jnp.pad(w_ukv_h[:, :, :QK_NOPE],
                     ((0, 0), (0, 0), (0, HEAD_PAD - QK_NOPE))).reshape(KV_LORA, -1).astype(bf)
    w_uvt = w_ukv_h[:, :, QK_NOPE:].reshape(KV_LORA, ATTN_W).T.astype(bf)
    return {
        "g_mix": row(norm_mix_g[l]), "w_in": w_in_p, "g_q": row(q_norm_g[l]), "w_uq": w_uq_p,
        "g_kv": row(kv_norm_g[l]), "w_uk": w_uk_p, "w_uvt": w_uvt,
        "conv_w": conv_w[l].astype(jnp.float32), "conv_b": row(conv_b[l]),
        "wg": _block_diag_gates(w_gate_a[l], w_gate_x[l]),
        "bg": jnp.stack([b_gate_a[l], b_gate_x[l]]).astype(jnp.float32),
        "lam": row(lru_lambda[l]), "g_lru": row(lru_out_g[l]), "g_attn": row(attn_out_g[l]),
        "g_mlp": row(norm_mlp_g[l]),
    }


def _pad_buf(buf):
    return jnp.pad(buf, ((0, 0), (SUBLANES - (CONV_W - 1), 0), (0, 0)))


def _pad_axis(a, axis, size):
    pad = [(0, 0)] * a.ndim
    pad[axis] = (0, size - a.shape[axis])
    return jnp.pad(a, pad)


def kernel(x_prompt, x_sample, cache_ckv, cache_kpe, state_lru_h, state_conv, meta_tokens,
           norm_mix_g, w_in, q_norm_g, w_uq, kv_norm_g, w_ukv, conv_w, conv_b,
           w_gate_a, b_gate_a, w_gate_x, b_gate_x, lru_lambda, attn_out_g, lru_out_g,
           w_out, norm_mlp_g, w_up, w_down, final_norm_g):
    b_p, seq, _ = x_prompt.shape
    b_s, dec_seq, _ = x_sample.shape
    past_len = cache_ckv.shape[2]
    assert (past_len + dec_seq - 1) // CHUNK == past_len // CHUNK
    tm_p, tq_p, tm_o = PROMPT_MIXER_ROWS, PROMPT_QUERY_ROWS, PROMPT_MLP_ROWS
    assert seq % tm_p == 0 and seq % tq_p == 0 and (b_p * seq) % tm_o == 0
    assert (tq_p // 2) % CHUNK == 0 and tq_p % Q_SLICE == 0

    meta_pos = jnp.arange(-N_META, 0, dtype=jnp.int32)
    prompt_pos = jnp.arange(seq, dtype=jnp.int32)
    sample_pos = past_len + jnp.arange(dec_seq, dtype=jnp.int32)
    tabk_m, tabk_p, tabk_s = _key_tables(meta_pos), _key_tables(prompt_pos), _key_tables(sample_pos)
    tabq_m, tabq_p, tabq_s = _query_tables(meta_pos), _query_tables(prompt_pos), _query_tables(sample_pos)
    g_fin = final_norm_g.reshape(1, -1).astype(jnp.float32)

    cache_kpe_pad = _pad_axis(cache_kpe, 3, ROPE_PAD)
    big = {"w_out": w_out.astype(jnp.bfloat16), "w_up": w_up.astype(jnp.bfloat16),
           "w_down": w_down.astype(jnp.bfloat16)}
    tq_small = LANES

    h_meta = meta_tokens[None].astype(jnp.float32)
    h_p, h_s = x_prompt, x_sample
    zero_h = jnp.zeros((1, 1, LRU_W), jnp.float32)
    zero_buf = jnp.zeros((1, SUBLANES, LRU_W), jnp.float32)
    outs = {name: [] for name in ("ckv_p", "kpe_p", "lru_p", "conv_p", "ckv_s", "kpe_s", "lru_s", "conv_s")}
    for l in range(DEPTH):
        lw = _layer_weights(l, norm_mix_g, w_in, q_norm_g, w_uq, kv_norm_g, w_ukv, conv_w, conv_b,
                            w_gate_a, b_gate_a, w_gate_x, b_gate_x, lru_lambda, attn_out_g,
                            lru_out_g, norm_mlp_g)
        last = l + 1 == DEPTH
        mq, mk, mvt, m_ckv, m_kpe, m_y, m_h, m_buf = _mixer_call(
            h_meta, zero_h, zero_buf, tabq_m, tabk_m, lw, tm=N_META, shared_state=True)
        pq, pk, pvt, p_ckv, p_kpe, p_y, p_h, p_buf = _mixer_call(
            h_p, m_h, _pad_buf(m_buf), tabq_p, tabk_p, lw, tm=tm_p, shared_state=True)
        mk_p, mvt_p = _pad_axis(mk, 2, LANES), _pad_axis(mvt, 2, LANES)
        p_o = _attn_causal_call(pq, mk_p, mvt_p, pk, pvt, tq=tq_p, n_prefix_valid=N_META)
        h_p = _out_mlp_call(p_o.reshape(b_p * seq, ATTN_W), p_y.reshape(b_p * seq, LRU_W),
                            h_p.reshape(b_p * seq, D_MODEL), lw, big, g_fin, layer=l, tm=tm_o,
                            final=last).reshape(b_p, seq, D_MODEL)
        sq, sk, svt, s_ckv, s_kpe, s_y, s_h, s_buf = _mixer_call(
            h_s, state_lru_h[l][:, None, :], _pad_buf(state_conv[l]), tabq_s, tabk_s, lw,
            tm=dec_seq, shared_state=False)
        s_o = _attn_sample_call(sq, m_ckv, cache_ckv, s_ckv, m_kpe, cache_kpe_pad, s_kpe, lw,
                                layer=l)[:, :dec_seq]
        h_s = _out_mlp_call(s_o.reshape(b_s * dec_seq, ATTN_W), s_y.reshape(b_s * dec_seq, LRU_W),
                            h_s.reshape(b_s * dec_seq, D_MODEL), lw, big, g_fin, layer=l,
                            tm=b_s * dec_seq, final=last).reshape(b_s, dec_seq, D_MODEL)
        if not last:
            m_o = _attn_dense_call(_pad_axis(mq, 2, tq_small), mk_p, mvt_p,
                                   n_prefix_valid=N_META)[:, :N_META]
            h_meta = _out_mlp_call(m_o.reshape(N_META, ATTN_W), m_y.reshape(N_META, LRU_W),
                                   h_meta.reshape(N_META, D_MODEL), lw, big, g_fin, layer=l,
                                   tm=N_META, final=False).reshape(1, N_META, D_MODEL)
        outs["ckv_p"].append(p_ckv); outs["kpe_p"].append(p_kpe[:, :, :QK_ROPE])
        outs["lru_p"].append(p_h[:, 0]); outs["conv_p"].append(p_buf)
        outs["ckv_s"].append(s_ckv); outs["kpe_s"].append(s_kpe[:, :, :QK_ROPE])
        outs["lru_s"].append(s_h[:, 0]); outs["conv_s"].append(s_buf)

    return (h_p, h_s, jnp.stack(outs["ckv_p"]), jnp.stack(outs["kpe_p"]), jnp.stack(outs["lru_p"]),
            jnp.stack(outs["conv_p"]), jnp.stack(outs["ckv_s"]), jnp.stack(outs["kpe_s"]),
            jnp.stack(outs["lru_s"]), jnp.stack(outs["conv_s"]))
```

```python
import functools

import jax
import jax.numpy as jnp
from jax import lax
from jax.experimental import pallas as pl
from jax.experimental.pallas import tpu as pltpu

D_MODEL = 1024
DEPTH = 2
CHUNK = 64
N_META = 16
ATTN_W = 512
LRU_W = 512
V_DIM = 64
N_HEADS = 8
QK_NOPE = 64
QK_ROPE = 32
Q_LORA = 768
KV_LORA = 256
LRU_BLOCKS = 8
LRU_BW = LRU_W // LRU_BLOCKS
LRU_C = 8.0
CONV_W = 4
D_FF = 4 * D_MODEL
ROPE_THETA = 10000.0
EPS = 1e-6
SM_SCALE = (QK_NOPE + QK_ROPE) ** -0.5
NEG_INF = -1e30
LOG2_E = 1.4426950408889634

LANES = 128
SUBLANES = 8
HEAD_PAD = LANES
VMEM_LIMIT_BYTES = 56 * 1024 * 1024

ROPE_PAD = LANES
COL_CQ = 0
COL_CKV = Q_LORA
COL_KR = Q_LORA + KV_LORA
COL_XR = COL_KR + ROPE_PAD
COL_GR = COL_XR + LRU_W
IN_W_PAD = COL_GR + LRU_W

HEADS_PER_STEP = 2
ONES_ROWS = 16
V_AUG = V_DIM + ONES_ROWS
F_CHUNK = 1024
Q_SLICE = 256
PROMPT_MIXER_ROWS = 512
PROMPT_QUERY_ROWS = 1024
PROMPT_MLP_ROWS = 1024


def _rms(x, g):
    return x * lax.rsqrt(jnp.mean(x * x, axis=-1, keepdims=True) + EPS) * g


def _rope_lanes(x, tab_ref, axis=1):
    half = QK_ROPE // 2
    return (x * tab_ref[0] + pltpu.roll(x, LANES - half, axis) * tab_ref[1]
            + pltpu.roll(x, half, axis) * tab_ref[2])


def _dot(a, b):
    return jnp.dot(a, b, preferred_element_type=jnp.float32)


def _dot_nt(a, b):
    return lax.dot_general(a, b, (((1,), (1,)), ((), ())), preferred_element_type=jnp.float32)


def _mixer_kernel(x_ref, h0_ref, buf0_ref, tabq_ref, tabk_ref, g_mix_ref, w_in_ref, g_q_ref,
                  w_uq_ref, g_kv_ref, w_uk_ref, w_uvt_ref, conv_w_ref, conv_b_ref, wg_ref, bg_ref,
                  lam_ref, g_lru_ref,
                  q_ref, k_ref, vt_ref, ckv_ref, kpe_ref, y_ref, hlast_ref, buf_ref,
                  tail_scr, hcar_scr, *, tm, q_transposed):
    t = pl.program_id(1)

    @pl.when(t == 0)
    def _():
        tail_scr[...] = buf0_ref[0]
        hcar_scr[...] = h0_ref[0]

    x = x_ref[0]
    xn = _rms(x, g_mix_ref[...]).astype(jnp.bfloat16)
    z = _dot(xn, w_in_ref[...])

    cqn = _rms(z[:, COL_CQ:COL_CQ + Q_LORA], g_q_ref[...]).astype(jnp.bfloat16)
    if q_transposed:
        q_t = _dot_nt(w_uq_ref[...], cqn)
        for h in range(N_HEADS):
            qh = q_t[h * HEAD_PAD:(h + 1) * HEAD_PAD, :]
            q_ref[0, h] = _rope_lanes(qh, tabq_ref, axis=0).astype(jnp.bfloat16)
    else:
        q = _dot(cqn, w_uq_ref[...])
        for h in range(N_HEADS):
            qh = q[:, h * HEAD_PAD:(h + 1) * HEAD_PAD]
            q_ref[0, h] = _rope_lanes(qh, tabq_ref).astype(jnp.bfloat16)

    ckv = _rms(z[:, COL_CKV:COL_CKV + KV_LORA], g_kv_ref[...])
    ckv_ref[0] = ckv
    ckv_b = ckv.astype(jnp.bfloat16)
    kpe = _rope_lanes(z[:, COL_KR:COL_KR + ROPE_PAD], tabk_ref)
    kpe_ref[0] = kpe
    kpe_shift = pltpu.roll(kpe, QK_NOPE, 1)
    kk = _dot(ckv_b, w_uk_ref[...])
    for h in range(N_HEADS):
        k_ref[0, h] = (kk[:, h * HEAD_PAD:(h + 1) * HEAD_PAD] + kpe_shift).astype(jnp.bfloat16)
    vt_ref[0] = _dot_nt(w_uvt_ref[...], ckv_b).astype(jnp.bfloat16)

    x_r = z[:, COL_XR:COL_XR + LRU_W]
    g_r = z[:, COL_GR:COL_GR + LRU_W]
    n_slabs = tm // SUBLANES
    rows8 = lax.broadcasted_iota(jnp.int32, (SUBLANES, LRU_W), 0)
    slabs = [tail_scr[...]] + [x_r[g * SUBLANES:(g + 1) * SUBLANES, :] for g in range(n_slabs)]
    xc = conv_b_ref[...] + x_r * conv_w_ref[CONV_W - 1:CONV_W, :]
    for back in range(1, CONV_W):
        rolled = [pltpu.roll(s, back, 0) for s in slabs]
        shifted = jnp.concatenate(
            [jnp.where(rows8 < back, rolled[g], rolled[g + 1]) for g in range(n_slabs)], axis=0)
        xc = xc + shifted * conv_w_ref[CONV_W - 1 - back:CONV_W - back, :]
    tail_scr[...] = slabs[-1]
    buf_ref[0] = tail_scr[SUBLANES - (CONV_W - 1):SUBLANES, :]

    half_w = LRU_W // 2
    gates = [_dot(xc[:, s * half_w:(s + 1) * half_w].astype(jnp.bfloat16), wg_ref[s]) for s in range(2)]
    pre_a = jnp.concatenate([gates[0][:, :half_w], gates[1][:, :half_w]], axis=1)
    pre_x = jnp.concatenate([gates[0][:, half_w:], gates[1][:, half_w:]], axis=1)
    r = jax.nn.sigmoid(pre_a + bg_ref[0:1, :])
    gi = jax.nn.sigmoid(pre_x + bg_ref[1:2, :])
    neg_lam = -lam_ref[...]
    softplus = jnp.maximum(neg_lam, 0.0) + jnp.log1p(jnp.exp(-jnp.abs(neg_lam)))
    log_a = (-LRU_C) * r * softplus
    a = jnp.exp(log_a)
    one_m_a2 = -jnp.tanh(log_a) * (a * a + 1.0)
    root = jnp.where(one_m_a2 > 0.0, one_m_a2 * lax.rsqrt(one_m_a2), 0.0)
    b = root * (gi * xc)

    carry = jnp.broadcast_to(hcar_scr[...], (SUBLANES, LRU_W))
    h_slabs = []
    for g in range(n_slabs):
        a_g = a[g * SUBLANES:(g + 1) * SUBLANES, :]
        b_g = b[g * SUBLANES:(g + 1) * SUBLANES, :]
        shift = 1
        while shift < SUBLANES:
            keep = rows8 >= shift
            a_prev = jnp.where(keep, pltpu.roll(a_g, shift, 0), 1.0)
            b_prev = jnp.where(keep, pltpu.roll(b_g, shift, 0), 0.0)
            b_g = a_g * b_prev + b_g
            a_g = a_g * a_prev
            shift *= 2
        h_g = a_g * carry + b_g
        h_slabs.append(h_g)
        carry = jnp.broadcast_to(h_g[SUBLANES - 1:SUBLANES, :], (SUBLANES, LRU_W))
    hh = jnp.concatenate(h_slabs, axis=0)
    h_last = carry[0:1, :]
    hcar_scr[...] = h_last
    hlast_ref[0] = h_last
    y = hh * jax.nn.gelu(g_r)
    y_ref[0] = _rms(y, g_lru_ref[...]).astype(jnp.bfloat16)


def _mixer_call(x, h0, buf0p, tabq, tabk, lw, *, tm, shared_state, q_transposed=False):
    bsz, seq, _ = x.shape
    nt = seq // tm
    const2 = lambda b, t: (0, 0)
    const3 = lambda b, t: (0, 0, 0)
    state_idx = (lambda b, t: (0, 0, 0)) if shared_state else (lambda b, t: (b, 0, 0))
    in_specs = [
        pl.BlockSpec((1, tm, D_MODEL), lambda b, t: (b, t, 0)),
        pl.BlockSpec((1, 1, LRU_W), state_idx),
        pl.BlockSpec((1, SUBLANES, LRU_W), state_idx),
        (pl.BlockSpec((3, LANES, tm), lambda b, t: (0, 0, t)) if q_transposed
         else pl.BlockSpec((3, tm, LANES), lambda b, t: (0, t, 0))),
        pl.BlockSpec((3, tm, LANES), lambda b, t: (0, t, 0)),
        pl.BlockSpec((1, D_MODEL), const2),
        pl.BlockSpec((D_MODEL, IN_W_PAD), const2),
        pl.BlockSpec((1, Q_LORA), const2),
        pl.BlockSpec((N_HEADS * HEAD_PAD, Q_LORA) if q_transposed else (Q_LORA, N_HEADS * HEAD_PAD), const2),
        pl.BlockSpec((1, KV_LORA), const2),
        pl.BlockSpec((KV_LORA, N_HEADS * HEAD_PAD), const2),
        pl.BlockSpec((ATTN_W, KV_LORA), const2),
        pl.BlockSpec((CONV_W, LRU_W), const2),
        pl.BlockSpec((1, LRU_W), const2),
        pl.BlockSpec((2, LRU_W // 2, LRU_W), const3),
        pl.BlockSpec((2, LRU_W), const2),
        pl.BlockSpec((1, LRU_W), const2),
        pl.BlockSpec((1, LRU_W), const2),
    ]
    out_shape = [
        jax.ShapeDtypeStruct((bsz, N_HEADS, HEAD_PAD, seq) if q_transposed
                             else (bsz, N_HEADS, seq, HEAD_PAD), jnp.bfloat16),
        jax.ShapeDtypeStruct((bsz, N_HEADS, seq, HEAD_PAD), jnp.bfloat16),
        jax.ShapeDtypeStruct((bsz, ATTN_W, seq), jnp.bfloat16),
        jax.ShapeDtypeStruct((bsz, seq, KV_LORA), jnp.float32),
        jax.ShapeDtypeStruct((bsz, seq, ROPE_PAD), jnp.float32),
        jax.ShapeDtypeStruct((bsz, seq, LRU_W), jnp.bfloat16),
        jax.ShapeDtypeStruct((bsz, 1, LRU_W), jnp.float32),
        jax.ShapeDtypeStruct((bsz, CONV_W - 1, LRU_W), jnp.float32),
    ]
    out_specs = [
        (pl.BlockSpec((1, N_HEADS, HEAD_PAD, tm), lambda b, t: (b, 0, 0, t)) if q_transposed
         else pl.BlockSpec((1, N_HEADS, tm, HEAD_PAD), lambda b, t: (b, 0, t, 0))),
        pl.BlockSpec((1, N_HEADS, tm, HEAD_PAD), lambda b, t: (b, 0, t, 0)),
        pl.BlockSpec((1, ATTN_W, tm), lambda b, t: (b, 0, t)),
        pl.BlockSpec((1, tm, KV_LORA), lambda b, t: (b, t, 0)),
        pl.BlockSpec((1, tm, ROPE_PAD), lambda b, t: (b, t, 0)),
        pl.BlockSpec((1, tm, LRU_W), lambda b, t: (b, t, 0)),
        pl.BlockSpec((1, 1, LRU_W), lambda b, t: (b, 0, 0)),
        pl.BlockSpec((1, CONV_W - 1, LRU_W), lambda b, t: (b, 0, 0)),
    ]
    return pl.pallas_call(
        functools.partial(_mixer_kernel, tm=tm, q_transposed=q_transposed),
        grid=(bsz, nt),
        in_specs=in_specs,
        out_specs=out_specs,
        out_shape=out_shape,
        scratch_shapes=[pltpu.VMEM((SUBLANES, LRU_W), jnp.float32),
                        pltpu.VMEM((1, LRU_W), jnp.float32)],
        compiler_params=pltpu.CompilerParams(
            dimension_semantics=("arbitrary", "arbitrary"), vmem_limit_bytes=VMEM_LIMIT_BYTES),
        name="mixer",
    )(x, h0, buf0p, tabq, tabk, lw["g_mix"], lw["w_in"], lw["g_q"],
      lw["w_uq_t"] if q_transposed else lw["w_uq"], lw["g_kv"],
      lw["w_uk"], lw["w_uvt"], lw["conv_w"], lw["conv_b"], lw["wg"], lw["bg"], lw["lam"], lw["g_lru"])


def _attn_sample_kernel(q_ref, ckv_m_ref, ckv_c_ref, ckv_s_ref, kpe_m_ref, kpe_c_ref, kpe_s_ref,
                        w_uk_ref, w_uvt_ref, o_ref, *, n_keys_pad):
    ckv_parts = [ckv_m_ref[0], ckv_c_ref[0, 0], ckv_s_ref[0]]
    kpe_parts = [kpe_m_ref[0], kpe_c_ref[0, 0], kpe_s_ref[0]]
    n_valid = sum(p.shape[0] for p in ckv_parts)
    pad = n_keys_pad - n_valid
    ckv = jnp.concatenate(ckv_parts + [jnp.zeros((pad, KV_LORA), jnp.float32)], axis=0)
    kpe = jnp.concatenate(kpe_parts + [jnp.zeros((pad, ROPE_PAD), jnp.float32)], axis=0)
    ckv_b = ckv.astype(jnp.bfloat16)
    kpe_shift = pltpu.roll(kpe, QK_NOPE, 1)
    kk = _dot(ckv_b, w_uk_ref[...])
    vt = _dot_nt(w_uvt_ref[...], ckv_b).astype(jnp.bfloat16)
    n_q = q_ref.shape[2]
    q_pad = jnp.zeros((LANES - n_q, HEAD_PAD), jnp.bfloat16)
    q_tiles = [jnp.concatenate([q_ref[0, h], q_pad], axis=0) for h in range(N_HEADS)]
    zero_q = jnp.zeros((LANES, HEAD_PAD), jnp.bfloat16)
    ones_k = jnp.ones((ONES_ROWS, n_keys_pad), jnp.bfloat16)
    kpe_pair = jnp.concatenate([kpe_shift, kpe_shift], axis=1)
    key_idx = lax.broadcasted_iota(jnp.int32, (n_keys_pad, 2 * LANES), 0)
    accs = []
    for pair in range(N_HEADS // 2):
        h_a, h_b = 2 * pair, 2 * pair + 1
        k_pair = (kk[:, h_a * HEAD_PAD:(h_b + 1) * HEAD_PAD] + kpe_pair).astype(jnp.bfloat16)
        q_blockdiag = jnp.concatenate([jnp.concatenate([q_tiles[h_a], zero_q], axis=1),
                                       jnp.concatenate([zero_q, q_tiles[h_b]], axis=1)], axis=0)
        st = _dot_nt(k_pair, q_blockdiag)
        st = jnp.where(key_idx < n_valid, st, NEG_INF)
        p = jnp.exp2(st - _col_max(st)).astype(jnp.bfloat16)
        vt_pair = jnp.concatenate([vt[h_a * V_DIM:(h_a + 1) * V_DIM, :], ones_k,
                                   vt[h_b * V_DIM:(h_b + 1) * V_DIM, :], ones_k], axis=0)
        acc = _dot(vt_pair, p)
        accs.append(acc[:V_AUG, :LANES])
        accs.append(acc[V_AUG:, LANES:])
    _store_normalised(accs, o_ref)


def _attn_sample_call(q, ckv_m, ckv_cache, ckv_s, kpe_m, kpe_cache, kpe_s, lw, *, layer):
    bsz, _, dec, _ = q.shape
    past = ckv_cache.shape[2]
    n_meta = ckv_m.shape[1]
    n_keys_pad = -(-(n_meta + past + dec) // LANES) * LANES
    return pl.pallas_call(
        functools.partial(_attn_sample_kernel, n_keys_pad=n_keys_pad),
        grid=(bsz,),
        in_specs=[
            pl.BlockSpec((1, N_HEADS, dec, HEAD_PAD), lambda b: (b, 0, 0, 0)),
            pl.BlockSpec((1, n_meta, KV_LORA), lambda b: (0, 0, 0)),
            pl.BlockSpec((1, 1, past, KV_LORA), lambda b: (layer, b, 0, 0)),
            pl.BlockSpec((1, dec, KV_LORA), lambda b: (b, 0, 0)),
            pl.BlockSpec((1, n_meta, ROPE_PAD), lambda b: (0, 0, 0)),
            pl.BlockSpec((1, 1, past, ROPE_PAD), lambda b: (layer, b, 0, 0)),
            pl.BlockSpec((1, dec, ROPE_PAD), lambda b: (b, 0, 0)),
            pl.BlockSpec((KV_LORA, N_HEADS * HEAD_PAD), lambda b: (0, 0)),
            pl.BlockSpec((ATTN_W, KV_LORA), lambda b: (0, 0)),
        ],
        out_specs=pl.BlockSpec((1, LANES, ATTN_W), lambda b: (b, 0, 0)),
        out_shape=jax.ShapeDtypeStruct((bsz, LANES, ATTN_W), jnp.bfloat16),
        compiler_params=pltpu.CompilerParams(
            dimension_semantics=("arbitrary",), vmem_limit_bytes=VMEM_LIMIT_BYTES),
        name="attn_sample",
    )(q, ckv_m, ckv_cache, ckv_s, kpe_m, kpe_cache, kpe_s, lw["w_uk"], lw["w_uvt"])


def _col_max(st):
    rows, cols = st.shape
    slab = SUBLANES
    if rows % slab == 0 and rows > slab:
        st = jnp.max(st.reshape(rows // slab, slab, cols), axis=0)
    return jnp.max(st, axis=0, keepdims=True)


def _dense_state(q_tiles, k_tiles, vt_tiles, n_valid, q_transposed=False):
    n_keys = k_tiles[0].shape[0]
    tq = q_tiles[0].shape[1 if q_transposed else 0]
    n_used = -(-n_valid // ONES_ROWS) * ONES_ROWS
    ones_p = jnp.ones((ONES_ROWS, n_keys), jnp.bfloat16)
    ms, accs = [], []
    for q_t, k_t, vt_t in zip(q_tiles, k_tiles, vt_tiles):
        st = (_dot if q_transposed else _dot_nt)(k_t[:n_used], q_t)
        if n_valid < n_used:
            key_idx = lax.broadcasted_iota(jnp.int32, (n_used, tq), 0)
            st = jnp.where(key_idx < n_valid, st, NEG_INF)
        m = _col_max(st)
        p = jnp.exp2(st - m).astype(jnp.bfloat16)
        if n_used < n_keys:
            p = jnp.concatenate([p, jnp.zeros((n_keys - n_used, tq), jnp.bfloat16)], axis=0)
        ms.append(m)
        accs.append(_dot(jnp.concatenate([vt_t, ones_p], axis=0), p))
    return ms, accs


def _prefix_state(q_tiles, kp_ref, vtp_ref, n_valid, q_transposed=False):
    heads = range(len(q_tiles))
    return _dense_state(q_tiles, [kp_ref[0, hh] for hh in heads],
                        [vtp_ref[0, hh * V_DIM:(hh + 1) * V_DIM, :] for hh in heads], n_valid,
                        q_transposed=q_transposed)


def _store_normalised(accs, o_ref):
    outs = [acc[:V_DIM, :] / acc[V_DIM:V_DIM + 1, :] for acc in accs]
    o_ref[0] = jnp.concatenate(outs, axis=0).T.astype(jnp.bfloat16)


def _attn_dense_kernel(q_ref, kp_ref, vtp_ref, o_ref, *, n_valid):
    q_tiles = [q_ref[0, hh] for hh in range(q_ref.shape[1])]
    _, accs = _prefix_state(q_tiles, kp_ref, vtp_ref, n_valid)
    _store_normalised(accs, o_ref)


def _attn_causal_kernel(q_ref, kp_ref, vtp_ref, k_ref, vt_ref, o_ref,
                        sa_scr, sb_scr, ma_scr, mb_scr, m_scr, acc_scr,
                        *, tq, tk, n_prefix_valid):
    qi = pl.program_id(2)
    n_q = pl.num_programs(2)

    def q_cols(hh, q_idx, q0, qn):
        return q_ref[0, hh, :, pl.ds(pl.multiple_of(q_idx * tq + q0, Q_SLICE), qn)]

    m_scr[...] = jnp.full(m_scr.shape, NEG_INF, jnp.float32)
    acc_scr[...] = jnp.zeros(acc_scr.shape, jnp.float32)
    ones_k = jnp.ones((ONES_ROWS, tk), jnp.bfloat16)

    def scores(tile, s_ref, mx_ref, q_idx=qi, q0=0, qn=Q_SLICE, kn=tk, heads=range(HEADS_PER_STEP)):
        start = pl.multiple_of(tile * tk, tk)
        for hh in heads:
            st = _dot(k_ref[0, hh, pl.ds(start, kn), :], q_cols(hh, q_idx, q0, qn))
            s_ref[hh, q0 // Q_SLICE, 0:kn, :] = st
            mx_ref[hh, :, q0:q0 + qn] = _col_max(st)

    def softmax_pv(tile, s_ref, mx_ref, mask=None, q0=0, qn=Q_SLICE, kn=tk, heads=range(HEADS_PER_STEP)):
        start = pl.multiple_of(tile * tk, tk)
        for hh in heads:
            st = s_ref[hh, q0 // Q_SLICE, 0:kn, :]
            if mask is None:
                m_blk = mx_ref[hh, :, q0:q0 + qn]
            else:
                st = jnp.where(mask, st, NEG_INF)
                m_blk = _col_max(st)
            m_old = m_scr[hh, :, q0:q0 + qn]
            m_new = jnp.maximum(m_old, m_blk)
            p = jnp.exp2(st - m_new).astype(jnp.bfloat16)
            vt_t = vt_ref[0, hh * V_DIM:(hh + 1) * V_DIM, pl.ds(start, kn)]
            vt_aug = jnp.concatenate([vt_t, ones_k[:, :kn]], axis=0)
            acc_scr[hh, :, q0:q0 + qn] = (jnp.exp2(m_old - m_new) * acc_scr[hh, :, q0:q0 + qn]
                                          + _dot(vt_aug, p))
            m_scr[hh, :, q0:q0 + qn] = m_new

    @pl.when(qi == 0)
    def _():
        for q0 in range(0, tq, Q_SLICE):
            scores(0, sa_scr, ma_scr, q0=q0)

    def tile_pair(t0):
        for hh in range(HEADS_PER_STEP):
            for q0 in range(0, tq, Q_SLICE):
                scores(t0 + 1, sb_scr, mb_scr, heads=(hh,), q0=q0, qn=Q_SLICE)
                softmax_pv(t0, sa_scr, ma_scr, heads=(hh,), q0=q0, qn=Q_SLICE)
        for hh in range(HEADS_PER_STEP):
            for q0 in range(0, tq, Q_SLICE):
                scores(t0 + 2, sa_scr, ma_scr, heads=(hh,), q0=q0, qn=Q_SLICE)
                softmax_pv(t0 + 1, sb_scr, mb_scr, heads=(hh,), q0=q0, qn=Q_SLICE)

    def body(jj, carry):
        tile_pair(4 * jj)
        tile_pair(4 * jj + 2)
        return carry

    lax.fori_loop(0, qi // 2, body, 0)

    @pl.when(qi % 2 == 1)
    def _():
        tile_pair(2 * (qi - 1))

    d1 = 2 * qi
    key_chunk = lax.broadcasted_iota(jnp.int32, (tk, Q_SLICE), 0) // CHUNK
    qry_local = lax.broadcasted_iota(jnp.int32, (tk, Q_SLICE), 1)

    def visible_keys(q_off):
        return min(tk, q_off + Q_SLICE)

    def chunk_mask(q_off):
        if q_off >= tk:
            return None
        return (key_chunk <= (qry_local + q_off) // CHUNK)[:visible_keys(q_off)]

    for hh in range(HEADS_PER_STEP):
        for q0 in range(0, tq, Q_SLICE):
            if q0 >= tk:
                scores(d1 + 1, sb_scr, mb_scr, heads=(hh,), q0=q0, qn=Q_SLICE, kn=visible_keys(q0 - tk))
            softmax_pv(d1, sa_scr, ma_scr, mask=chunk_mask(q0), heads=(hh,), q0=q0, qn=Q_SLICE,
                       kn=visible_keys(q0))
    q_next = jnp.minimum(qi + 1, n_q - 1)
    for hh in range(HEADS_PER_STEP):
        for q0 in range(0, tq, Q_SLICE):
            scores(0, sa_scr, ma_scr, q_idx=q_next, heads=(hh,), q0=q0, qn=Q_SLICE)
            if q0 >= tk:
                softmax_pv(d1 + 1, sb_scr, mb_scr, mask=chunk_mask(q0 - tk), heads=(hh,),
                           q0=q0, qn=Q_SLICE, kn=visible_keys(q0 - tk))
    ms_p, accs_p = _prefix_state([q_cols(hh, qi, 0, tq) for hh in range(HEADS_PER_STEP)],
                                 kp_ref, vtp_ref, n_prefix_valid, q_transposed=True)
    accs = []
    for hh in range(HEADS_PER_STEP):
        m_run = m_scr[hh]
        m_all = jnp.maximum(m_run, ms_p[hh])
        accs.append(jnp.exp2(m_run - m_all) * acc_scr[hh] + jnp.exp2(ms_p[hh] - m_all) * accs_p[hh])
    _store_normalised(accs, o_ref)


def _attn_dense_call(q, kp, vtp, *, n_prefix_valid):
    bsz, _, tq, _ = q.shape
    n_prefix = kp.shape[2]
    return pl.pallas_call(
        functools.partial(_attn_dense_kernel, n_valid=n_prefix_valid),
        grid=(bsz,),
        in_specs=[
            pl.BlockSpec((1, N_HEADS, tq, HEAD_PAD), lambda b: (b, 0, 0, 0)),
            pl.BlockSpec((1, N_HEADS, n_prefix, HEAD_PAD), lambda b: (b, 0, 0, 0)),
            pl.BlockSpec((1, ATTN_W, n_prefix), lambda b: (b, 0, 0)),
        ],
        out_specs=pl.BlockSpec((1, tq, ATTN_W), lambda b: (b, 0, 0)),
        out_shape=jax.ShapeDtypeStruct((bsz, tq, ATTN_W), jnp.bfloat16),
        compiler_params=pltpu.CompilerParams(
            dimension_semantics=("arbitrary",), vmem_limit_bytes=VMEM_LIMIT_BYTES),
        name="attn_dense",
    )(q, kp, vtp)


def _attn_causal_call(q, kp, vtp, k, vt, *, tq, n_prefix_valid):
    bsz, _, _, seq = q.shape
    n_prefix = kp.shape[2]
    tk = tq // 2
    f32 = jnp.float32
    return pl.pallas_call(
        functools.partial(_attn_causal_kernel, tq=tq, tk=tk, n_prefix_valid=n_prefix_valid),
        grid=(bsz, N_HEADS // HEADS_PER_STEP, seq // tq),
        in_specs=[
            pl.BlockSpec((1, HEADS_PER_STEP, HEAD_PAD, seq), lambda b, hp, i: (b, hp, 0, 0)),
            pl.BlockSpec((1, HEADS_PER_STEP, n_prefix, HEAD_PAD), lambda b, hp, i: (0, hp, 0, 0)),
            pl.BlockSpec((1, HEADS_PER_STEP * V_DIM, n_prefix), lambda b, hp, i: (0, hp, 0)),
            pl.BlockSpec((1, HEADS_PER_STEP, seq, HEAD_PAD), lambda b, hp, i: (b, hp, 0, 0)),
            pl.BlockSpec((1, HEADS_PER_STEP * V_DIM, seq), lambda b, hp, i: (b, hp, 0)),
        ],
        out_specs=pl.BlockSpec((1, tq, HEADS_PER_STEP * V_DIM), lambda b, hp, i: (b, i, hp)),
        out_shape=jax.ShapeDtypeStruct((bsz, seq, ATTN_W), jnp.bfloat16),
        scratch_shapes=[pltpu.VMEM((HEADS_PER_STEP, tq // Q_SLICE, tk, Q_SLICE), f32),
                        pltpu.VMEM((HEADS_PER_STEP, tq // Q_SLICE, tk, Q_SLICE), f32),
                        pltpu.VMEM((HEADS_PER_STEP, 1, tq), f32),
                        pltpu.VMEM((HEADS_PER_STEP, 1, tq), f32),
                        pltpu.VMEM((HEADS_PER_STEP, 1, tq), f32),
                        pltpu.VMEM((HEADS_PER_STEP, V_AUG, tq), f32)],
        compiler_params=pltpu.CompilerParams(
            dimension_semantics=("arbitrary", "arbitrary", "arbitrary"),
            vmem_limit_bytes=VMEM_LIMIT_BYTES),
        name="attn_causal",
    )(q, kp, vtp, k, vt)


def _out_mlp_kernel(o_ref, y_ref, x_ref, g_attn_ref, w_out_ref, g_mlp_ref, w_up_ref, w_down_ref,
                    g_fin_ref, out_ref, hn_scr, *, final):
    c = pl.program_id(1)

    @pl.when(c == 0)
    def _():
        on = _rms(o_ref[...].astype(jnp.float32), g_attn_ref[...]).astype(jnp.bfloat16)
        h = (x_ref[...] + _dot(on, w_out_ref[0:ATTN_W, :])
             + _dot(y_ref[...], w_out_ref[ATTN_W:ATTN_W + LRU_W, :]))
        hn_scr[...] = _rms(h, g_mlp_ref[...]).astype(jnp.bfloat16)
        out_ref[...] = h

    u = _dot(hn_scr[...], w_up_ref[...])
    u = jnp.square(jnp.maximum(u, 0.0)).astype(jnp.bfloat16)
    out_ref[...] += _dot(u, w_down_ref[...])

    if final:
        @pl.when(c == pl.num_programs(1) - 1)
        def _():
            out_ref[...] = _rms(out_ref[...], g_fin_ref[...])


def _out_mlp_call(o, y, x, lw, big, g_fin, *, layer, tm, final):
    rows = x.shape[0]
    const = lambda r, c: (0, 0)
    return pl.pallas_call(
        functools.partial(_out_mlp_kernel, final=final),
        grid=(rows // tm, D_FF // F_CHUNK),
        in_specs=[
            pl.BlockSpec((tm, ATTN_W), lambda r, c: (r, 0)),
            pl.BlockSpec((tm, LRU_W), lambda r, c: (r, 0)),
            pl.BlockSpec((tm, D_MODEL), lambda r, c: (r, 0)),
            pl.BlockSpec((1, ATTN_W), const),
            pl.BlockSpec((None, D_MODEL, D_MODEL), lambda r, c: (layer, 0, 0)),
            pl.BlockSpec((1, D_MODEL), const),
            pl.BlockSpec((None, D_MODEL, F_CHUNK), lambda r, c: (layer, 0, c)),
            pl.BlockSpec((None, F_CHUNK, D_MODEL), lambda r, c: (layer, c, 0)),
            pl.BlockSpec((1, D_MODEL), const),
        ],
        out_specs=pl.BlockSpec((tm, D_MODEL), lambda r, c: (r, 0)),
        out_shape=jax.ShapeDtypeStruct((rows, D_MODEL), jnp.float32),
        scratch_shapes=[pltpu.VMEM((tm, D_MODEL), jnp.bfloat16)],
        compiler_params=pltpu.CompilerParams(
            dimension_semantics=("arbitrary", "arbitrary"), vmem_limit_bytes=VMEM_LIMIT_BYTES),
        name="out_mlp",
    )(o, y, x, lw["g_attn"], big["w_out"], lw["g_mlp"], big["w_up"], big["w_down"], g_fin)


def _rope_tables(pos, rope_lane0, pass_lanes, scale):
    half = QK_ROPE // 2
    n = pos.shape[0]
    inv = ROPE_THETA ** (-jnp.arange(0, QK_ROPE, 2, dtype=jnp.float32) / QK_ROPE)
    ang = pos.astype(jnp.float32)[:, None] * inv[None, :]
    cos, sin = jnp.cos(ang) * scale, jnp.sin(ang) * scale
    fill = lambda width, value=0.0: jnp.full((n, width), value, jnp.float32)
    tail = LANES - rope_lane0 - QK_ROPE
    tab_c = jnp.concatenate([fill(pass_lanes, scale), fill(rope_lane0 - pass_lanes), cos, cos, fill(tail)],
                            axis=1)
    tab_m = jnp.concatenate([fill(rope_lane0), -sin, fill(half + tail)], axis=1)
    tab_p = jnp.concatenate([fill(rope_lane0 + half), sin, fill(tail)], axis=1)
    return jnp.stack([tab_c, tab_m, tab_p])


def _key_tables(pos):
    return _rope_tables(pos, 0, 0, 1.0)


def _query_tables(pos):
    return _rope_tables(pos, QK_NOPE, QK_NOPE, SM_SCALE * LOG2_E)


def _block_diag_gates(w_a, w_x):
    per_half = LRU_BLOCKS // 2
    halves = []
    for s in range(2):
        bd_a = jax.scipy.linalg.block_diag(*[w_a[s * per_half + i] for i in range(per_half)])
        bd_x = jax.scipy.linalg.block_diag(*[w_x[s * per_half + i] for i in range(per_half)])
        halves.append(jnp.concatenate([bd_a, bd_x], axis=1))
    return jnp.stack(halves).astype(jnp.bfloat16)


def _layer_weights(l, norm_mix_g, w_in, q_norm_g, w_uq, kv_norm_g, w_ukv, conv_w, conv_b,
                   w_gate_a, b_gate_a, w_gate_x, b_gate_x, lru_lambda, attn_out_g, lru_out_g,
                   norm_mlp_g):
    bf = jnp.bfloat16
    row = lambda v: v.reshape(1, -1).astype(jnp.float32)
    split = COL_KR + QK_ROPE
    w_in_p = (jnp.pad(w_in[l][:, :split], ((0, 0), (0, IN_W_PAD - split)))
              + jnp.pad(w_in[l][:, split:], ((0, 0), (split + ROPE_PAD - QK_ROPE, 0)))).astype(bf)
    w_uq_p = jnp.pad(w_uq[l].reshape(Q_LORA, N_HEADS, QK_NOPE + QK_ROPE),
                     ((0, 0), (0, 0), (0, HEAD_PAD - QK_NOPE - QK_ROPE))).reshape(Q_LORA, -1).astype(bf)
    w_ukv_h = w_ukv[l].reshape(KV_LORA, N_HEADS, QK_NOPE + V_DIM)
    w_uk_p = jnp.pad(w_ukv_h[:, :, :QK_NOPE],
                     ((0, 0), (0, 0), (0, HEAD_PAD - QK_NOPE))).reshape(KV_LORA, -1).astype(bf)
    w_uvt = w_ukv_h[:, :, QK_NOPE:].reshape(KV_LORA, ATTN_W).T.astype(bf)
    return {
        "g_mix": row(norm_mix_g[l]), "w_in": w_in_p, "g_q": row(q_norm_g[l]), "w_uq": w_uq_p, "w_uq_t": w_uq_p.T,
        "g_kv": row(kv_norm_g[l]), "w_uk": w_uk_p, "w_uvt": w_uvt,
        "conv_w": conv_w[l].astype(jnp.float32), "conv_b": row(conv_b[l]),
        "wg": _block_diag_gates(w_gate_a[l], w_gate_x[l]),
        "bg": jnp.stack([b_gate_a[l], b_gate_x[l]]).astype(jnp.float32),
        "lam": row(lru_lambda[l]), "g_lru": row(lru_out_g[l]), "g_attn": row(attn_out_g[l]),
        "g_mlp": row(norm_mlp_g[l]),
    }


def _pad_buf(buf):
    return jnp.pad(buf, ((0, 0), (SUBLANES - (CONV_W - 1), 0), (0, 0)))


def _pad_axis(a, axis, size):
    pad = [(0, 0)] * a.ndim
    pad[axis] = (0, size - a.shape[axis])
    return jnp.pad(a, pad)


def kernel(x_prompt, x_sample, cache_ckv, cache_kpe, state_lru_h, state_conv, meta_tokens,
           norm_mix_g, w_in, q_norm_g, w_uq, kv_norm_g, w_ukv, conv_w, conv_b,
           w_gate_a, b_gate_a, w_gate_x, b_gate_x, lru_lambda, attn_out_g, lru_out_g,
           w_out, norm_mlp_g, w_up, w_down, final_norm_g):
    b_p, seq, _ = x_prompt.shape
    b_s, dec_seq, _ = x_sample.shape
    past_len = cache_ckv.shape[2]
    assert (past_len + dec_seq - 1) // CHUNK == past_len // CHUNK
    tm_p, tq_p, tm_o = PROMPT_MIXER_ROWS, PROMPT_QUERY_ROWS, PROMPT_MLP_ROWS
    assert seq % tm_p == 0 and seq % tq_p == 0 and (b_p * seq) % tm_o == 0
    assert (tq_p // 2) % CHUNK == 0 and tq_p % Q_SLICE == 0

    meta_pos = jnp.arange(-N_META, 0, dtype=jnp.int32)
    prompt_pos = jnp.arange(seq, dtype=jnp.int32)
    sample_pos = past_len + jnp.arange(dec_seq, dtype=jnp.int32)
    tabk_m, tabk_p, tabk_s = _key_tables(meta_pos), _key_tables(prompt_pos), _key_tables(sample_pos)
    tabq_m, tabq_p, tabq_s = _query_tables(meta_pos), _query_tables(prompt_pos), _query_tables(sample_pos)
    tabq_p_t = jnp.swapaxes(tabq_p, 1, 2)
    g_fin = final_norm_g.reshape(1, -1).astype(jnp.float32)

    cache_kpe_pad = _pad_axis(cache_kpe, 3, ROPE_PAD)
    big = {"w_out": w_out.astype(jnp.bfloat16), "w_up": w_up.astype(jnp.bfloat16),
           "w_down": w_down.astype(jnp.bfloat16)}
    tq_small = LANES

    h_meta = meta_tokens[None].astype(jnp.float32)
    h_p, h_s = x_prompt, x_sample
    zero_h = jnp.zeros((1, 1, LRU_W), jnp.float32)
    zero_buf = jnp.zeros((1, SUBLANES, LRU_W), jnp.float32)
    outs = {name: [] for name in ("ckv_p", "kpe_p", "lru_p", "conv_p", "ckv_s", "kpe_s", "lru_s", "conv_s")}
    for l in range(DEPTH):
        lw = _layer_weights(l, norm_mix_g, w_in, q_norm_g, w_uq, kv_norm_g, w_ukv, conv_w, conv_b,
                            w_gate_a, b_gate_a, w_gate_x, b_gate_x, lru_lambda, attn_out_g,
                            lru_out_g, norm_mlp_g)
        last = l + 1 == DEPTH
        mq, mk, mvt, m_ckv, m_kpe, m_y, m_h, m_buf = _mixer_call(
            h_meta, zero_h, zero_buf, tabq_m, tabk_m, lw, tm=N_META, shared_state=True)
        pq, pk, pvt, p_ckv, p_kpe, p_y, p_h, p_buf = _mixer_call(
            h_p, m_h, _pad_buf(m_buf), tabq_p_t, tabk_p, lw, tm=tm_p, shared_state=True,
            q_transposed=True)
        mk_p, mvt_p = _pad_axis(mk, 2, LANES), _pad_axis(mvt, 2, LANES)
        p_o = _attn_causal_call(pq, mk_p, mvt_p, pk, pvt, tq=tq_p, n_prefix_valid=N_META)
        h_p = _out_mlp_call(p_o.reshape(b_p * seq, ATTN_W), p_y.reshape(b_p * seq, LRU_W),
                            h_p.reshape(b_p * seq, D_MODEL), lw, big, g_fin, layer=l, tm=tm_o,
                            final=last).reshape(b_p, seq, D_MODEL)
        sq, sk, svt, s_ckv, s_kpe, s_y, s_h, s_buf = _mixer_call(
            h_s, state_lru_h[l][:, None, :], _pad_buf(state_conv[l]), tabq_s, tabk_s, lw,
            tm=dec_seq, shared_state=False)
        s_o = _attn_sample_call(sq, m_ckv, cache_ckv, s_ckv, m_kpe, cache_kpe_pad, s_kpe, lw,
                                layer=l)[:, :dec_seq]
        h_s = _out_mlp_call(s_o.reshape(b_s * dec_seq, ATTN_W), s_y.reshape(b_s * dec_seq, LRU_W),
                            h_s.reshape(b_s * dec_seq, D_MODEL), lw, big, g_fin, layer=l,
                            tm=b_s * dec_seq, final=last).reshape(b_s, dec_seq, D_MODEL)
        if not last:
            m_o = _attn_dense_call(_pad_axis(mq, 2, tq_small), mk_p, mvt_p,
                                   n_prefix_valid=N_META)[:, :N_META]
            h_meta = _out_mlp_call(m_o.reshape(N_META, ATTN_W), m_y.reshape(N_META, LRU_W),
                                   h_meta.reshape(N_META, D_MODEL), lw, big, g_fin, layer=l,
                                   tm=N_META, final=False).reshape(1, N_META, D_MODEL)
        outs["ckv_p"].append(p_ckv); outs["kpe_p"].append(p_kpe[:, :, :QK_ROPE])
        outs["lru_p"].append(p_h[:, 0]); outs["conv_p"].append(p_buf)
        outs["ckv_s"].append(s_ckv); outs["kpe_s"].append(s_kpe[:, :, :QK_ROPE])
        outs["lru_s"].append(s_h[:, 0]); outs["conv_s"].append(s_buf)

    return (h_p, h_s, jnp.stack(outs["ckv_p"]), jnp.stack(outs["kpe_p"]), jnp.stack(outs["lru_p"]),
            jnp.stack(outs["conv_p"]), jnp.stack(outs["ckv_s"]), jnp.stack(outs["kpe_s"]),
            jnp.stack(outs["lru_s"]), jnp.stack(outs["conv_s"]))
```

```python
import functools

import jax
import jax.numpy as jnp
from jax import lax
from jax.experimental import pallas as pl
from jax.experimental.pallas import tpu as pltpu

D_MODEL = 1024
DEPTH = 2
CHUNK = 64
N_META = 16
ATTN_W = 512
LRU_W = 512
V_DIM = 64
N_HEADS = 8
QK_NOPE = 64
QK_ROPE = 32
Q_LORA = 768
KV_LORA = 256
LRU_BLOCKS = 8
LRU_BW = LRU_W // LRU_BLOCKS
LRU_C = 8.0
CONV_W = 4
D_FF = 4 * D_MODEL
ROPE_THETA = 10000.0
EPS = 1e-6
SM_SCALE = (QK_NOPE + QK_ROPE) ** -0.5
NEG_INF = -1e30
LOG2_E = 1.4426950408889634

LANES = 128
SUBLANES = 8
HEAD_PAD = LANES
VMEM_LIMIT_BYTES = 56 * 1024 * 1024

ROPE_PAD = LANES
COL_CQ = 0
COL_CKV = Q_LORA
COL_KR = Q_LORA + KV_LORA
COL_XR = COL_KR + ROPE_PAD
COL_GR = COL_XR + LRU_W
IN_W_PAD = COL_GR + LRU_W

HEADS_PER_STEP = 2
ONES_ROWS = 16
V_AUG = V_DIM + ONES_ROWS
F_CHUNK = 1024
Q_SLICE = 256
PROMPT_MIXER_ROWS = 512
PROMPT_QUERY_ROWS = 1024
PROMPT_MLP_ROWS = 1024


def _rms(x, g):
    return x * lax.rsqrt(jnp.mean(x * x, axis=-1, keepdims=True) + EPS) * g


def _rope_lanes(x, tab_ref, axis=1):
    half = QK_ROPE // 2
    return (x * tab_ref[0] + pltpu.roll(x, LANES - half, axis) * tab_ref[1]
            + pltpu.roll(x, half, axis) * tab_ref[2])


def _dot(a, b):
    return jnp.dot(a, b, preferred_element_type=jnp.float32)


def _dot_nt(a, b):
    return lax.dot_general(a, b, (((1,), (1,)), ((), ())), preferred_element_type=jnp.float32)


def _mixer_kernel(x_ref, h0_ref, buf0_ref, tabq_ref, tabk_ref, g_mix_ref, w_in_ref, g_q_ref,
                  w_uq_ref, g_kv_ref, w_uk_ref, w_uvt_ref, conv_w_ref, conv_b_ref, wg_ref, bg_ref,
                  lam_ref, g_lru_ref,
                  q_ref, k_ref, vt_ref, ckv_ref, kpe_ref, y_ref, hlast_ref, buf_ref,
                  tail_scr, hcar_scr, *, tm, q_transposed):
    t = pl.program_id(1)

    @pl.when(t == 0)
    def _():
        tail_scr[...] = buf0_ref[0]
        hcar_scr[...] = h0_ref[0]

    x = x_ref[0]
    xn = _rms(x, g_mix_ref[...]).astype(jnp.bfloat16)
    z = _dot(xn, w_in_ref[...])

    cqn = _rms(z[:, COL_CQ:COL_CQ + Q_LORA], g_q_ref[...]).astype(jnp.bfloat16)
    if q_transposed:
        q_t = _dot_nt(w_uq_ref[...], cqn)
        for h in range(N_HEADS):
            qh = q_t[h * HEAD_PAD:(h + 1) * HEAD_PAD, :]
            q_ref[0, h] = _rope_lanes(qh, tabq_ref, axis=0).astype(jnp.bfloat16)
    else:
        q = _dot(cqn, w_uq_ref[...])
        for h in range(N_HEADS):
            qh = q[:, h * HEAD_PAD:(h + 1) * HEAD_PAD]
            q_ref[0, h] = _rope_lanes(qh, tabq_ref).astype(jnp.bfloat16)

    ckv = _rms(z[:, COL_CKV:COL_CKV + KV_LORA], g_kv_ref[...])
    ckv_ref[0] = ckv
    ckv_b = ckv.astype(jnp.bfloat16)
    kpe = _rope_lanes(z[:, COL_KR:COL_KR + ROPE_PAD], tabk_ref)
    kpe_ref[0] = kpe
    kpe_shift = pltpu.roll(kpe, QK_NOPE, 1)
    kk = _dot(ckv_b, w_uk_ref[...])
    for h in range(N_HEADS):
        k_ref[0, h] = (kk[:, h * HEAD_PAD:(h + 1) * HEAD_PAD] + kpe_shift).astype(jnp.bfloat16)
    vt_ref[0] = _dot_nt(w_uvt_ref[...], ckv_b).astype(jnp.bfloat16)

    x_r = z[:, COL_XR:COL_XR + LRU_W]
    g_r = z[:, COL_GR:COL_GR + LRU_W]
    n_slabs = tm // SUBLANES
    rows8 = lax.broadcasted_iota(jnp.int32, (SUBLANES, LRU_W), 0)
    slabs = [tail_scr[...]] + [x_r[g * SUBLANES:(g + 1) * SUBLANES, :] for g in range(n_slabs)]
    xc = conv_b_ref[...] + x_r * conv_w_ref[CONV_W - 1:CONV_W, :]
    for back in range(1, CONV_W):
        rolled = [pltpu.roll(s, back, 0) for s in slabs]
        shifted = jnp.concatenate(
            [jnp.where(rows8 < back, rolled[g], rolled[g + 1]) for g in range(n_slabs)], axis=0)
        xc = xc + shifted * conv_w_ref[CONV_W - 1 - back:CONV_W - back, :]
    tail_scr[...] = slabs[-1]
    buf_ref[0] = tail_scr[SUBLANES - (CONV_W - 1):SUBLANES, :]

    half_w = LRU_W // 2
    gates = [_dot(xc[:, s * half_w:(s + 1) * half_w].astype(jnp.bfloat16), wg_ref[s]) for s in range(2)]
    pre_a = jnp.concatenate([gates[0][:, :half_w], gates[1][:, :half_w]], axis=1)
    pre_x = jnp.concatenate([gates[0][:, half_w:], gates[1][:, half_w:]], axis=1)
    r = jax.nn.sigmoid(pre_a + bg_ref[0:1, :])
    gi = jax.nn.sigmoid(pre_x + bg_ref[1:2, :])
    neg_lam = -lam_ref[...]
    softplus = jnp.maximum(neg_lam, 0.0) + jnp.log1p(jnp.exp(-jnp.abs(neg_lam)))
    log_a = (-LRU_C) * r * softplus
    a = jnp.exp(log_a)
    one_m_a2 = -jnp.tanh(log_a) * (a * a + 1.0)
    root = jnp.where(one_m_a2 > 0.0, one_m_a2 * lax.rsqrt(one_m_a2), 0.0)
    b = root * (gi * xc)

    carry = jnp.broadcast_to(hcar_scr[...], (SUBLANES, LRU_W))
    h_slabs = []
    for g in range(n_slabs):
        a_g = a[g * SUBLANES:(g + 1) * SUBLANES, :]
        b_g = b[g * SUBLANES:(g + 1) * SUBLANES, :]
        shift = 1
        while shift < SUBLANES:
            keep = rows8 >= shift
            a_prev = jnp.where(keep, pltpu.roll(a_g, shift, 0), 1.0)
            b_prev = jnp.where(keep, pltpu.roll(b_g, shift, 0), 0.0)
            b_g = a_g * b_prev + b_g
            a_g = a_g * a_prev
            shift *= 2
        h_g = a_g * carry + b_g
        h_slabs.append(h_g)
        carry = jnp.broadcast_to(h_g[SUBLANES - 1:SUBLANES, :], (SUBLANES, LRU_W))
    hh = jnp.concatenate(h_slabs, axis=0)
    h_last = carry[0:1, :]
    hcar_scr[...] = h_last
    hlast_ref[0] = h_last
    y = hh * jax.nn.gelu(g_r)
    y_ref[0] = _rms(y, g_lru_ref[...]).astype(jnp.bfloat16)


def _mixer_call(x, h0, buf0p, tabq, tabk, lw, *, tm, shared_state, q_transposed=False):
    bsz, seq, _ = x.shape
    nt = seq // tm
    const2 = lambda b, t: (0, 0)
    const3 = lambda b, t: (0, 0, 0)
    state_idx = (lambda b, t: (0, 0, 0)) if shared_state else (lambda b, t: (b, 0, 0))
    in_specs = [
        pl.BlockSpec((1, tm, D_MODEL), lambda b, t: (b, t, 0)),
        pl.BlockSpec((1, 1, LRU_W), state_idx),
        pl.BlockSpec((1, SUBLANES, LRU_W), state_idx),
        (pl.BlockSpec((3, LANES, tm), lambda b, t: (0, 0, t)) if q_transposed
         else pl.BlockSpec((3, tm, LANES), lambda b, t: (0, t, 0))),
        pl.BlockSpec((3, tm, LANES), lambda b, t: (0, t, 0)),
        pl.BlockSpec((1, D_MODEL), const2),
        pl.BlockSpec((D_MODEL, IN_W_PAD), const2),
        pl.BlockSpec((1, Q_LORA), const2),
        pl.BlockSpec((N_HEADS * HEAD_PAD, Q_LORA) if q_transposed else (Q_LORA, N_HEADS * HEAD_PAD), const2),
        pl.BlockSpec((1, KV_LORA), const2),
        pl.BlockSpec((KV_LORA, N_HEADS * HEAD_PAD), const2),
        pl.BlockSpec((ATTN_W, KV_LORA), const2),
        pl.BlockSpec((CONV_W, LRU_W), const2),
        pl.BlockSpec((1, LRU_W), const2),
        pl.BlockSpec((2, LRU_W // 2, LRU_W), const3),
        pl.BlockSpec((2, LRU_W), const2),
        pl.BlockSpec((1, LRU_W), const2),
        pl.BlockSpec((1, LRU_W), const2),
    ]
    out_shape = [
        jax.ShapeDtypeStruct((bsz, N_HEADS, HEAD_PAD, seq) if q_transposed
                             else (bsz, N_HEADS, seq, HEAD_PAD), jnp.bfloat16),
        jax.ShapeDtypeStruct((bsz, N_HEADS, seq, HEAD_PAD), jnp.bfloat16),
        jax.ShapeDtypeStruct((bsz, ATTN_W, seq), jnp.bfloat16),
        jax.ShapeDtypeStruct((bsz, seq, KV_LORA), jnp.float32),
        jax.ShapeDtypeStruct((bsz, seq, ROPE_PAD), jnp.float32),
        jax.ShapeDtypeStruct((bsz, seq, LRU_W), jnp.bfloat16),
        jax.ShapeDtypeStruct((bsz, 1, LRU_W), jnp.float32),
        jax.ShapeDtypeStruct((bsz, CONV_W - 1, LRU_W), jnp.float32),
    ]
    out_specs = [
        (pl.BlockSpec((1, N_HEADS, HEAD_PAD, tm), lambda b, t: (b, 0, 0, t)) if q_transposed
         else pl.BlockSpec((1, N_HEADS, tm, HEAD_PAD), lambda b, t: (b, 0, t, 0))),
        pl.BlockSpec((1, N_HEADS, tm, HEAD_PAD), lambda b, t: (b, 0, t, 0)),
        pl.BlockSpec((1, ATTN_W, tm), lambda b, t: (b, 0, t)),
        pl.BlockSpec((1, tm, KV_LORA), lambda b, t: (b, t, 0)),
        pl.BlockSpec((1, tm, ROPE_PAD), lambda b, t: (b, t, 0)),
        pl.BlockSpec((1, tm, LRU_W), lambda b, t: (b, t, 0)),
        pl.BlockSpec((1, 1, LRU_W), lambda b, t: (b, 0, 0)),
        pl.BlockSpec((1, CONV_W - 1, LRU_W), lambda b, t: (b, 0, 0)),
    ]
    return pl.pallas_call(
        functools.partial(_mixer_kernel, tm=tm, q_transposed=q_transposed),
        grid=(bsz, nt),
        in_specs=in_specs,
        out_specs=out_specs,
        out_shape=out_shape,
        scratch_shapes=[pltpu.VMEM((SUBLANES, LRU_W), jnp.float32),
                        pltpu.VMEM((1, LRU_W), jnp.float32)],
        compiler_params=pltpu.CompilerParams(
            dimension_semantics=("arbitrary", "arbitrary"), vmem_limit_bytes=VMEM_LIMIT_BYTES),
        name="mixer",
    )(x, h0, buf0p, tabq, tabk, lw["g_mix"], lw["w_in"], lw["g_q"],
      lw["w_uq_t"] if q_transposed else lw["w_uq"], lw["g_kv"],
      lw["w_uk"], lw["w_uvt"], lw["conv_w"], lw["conv_b"], lw["wg"], lw["bg"], lw["lam"], lw["g_lru"])


def _attn_sample_kernel(q_ref, ckv_m_ref, ckv_c_ref, ckv_s_ref, kpe_m_ref, kpe_c_ref, kpe_s_ref,
                        w_uk_ref, w_uvt_ref, o_ref, *, n_keys_pad):
    ckv_parts = [ckv_m_ref[0], ckv_c_ref[0, 0], ckv_s_ref[0]]
    kpe_parts = [kpe_m_ref[0], kpe_c_ref[0, 0], kpe_s_ref[0]]
    n_valid = sum(p.shape[0] for p in ckv_parts)
    pad = n_keys_pad - n_valid
    ckv = jnp.concatenate(ckv_parts + [jnp.zeros((pad, KV_LORA), jnp.float32)], axis=0)
    kpe = jnp.concatenate(kpe_parts + [jnp.zeros((pad, ROPE_PAD), jnp.float32)], axis=0)
    ckv_b = ckv.astype(jnp.bfloat16)
    kpe_shift = pltpu.roll(kpe, QK_NOPE, 1)
    kk = _dot(ckv_b, w_uk_ref[...])
    vt = _dot_nt(w_uvt_ref[...], ckv_b).astype(jnp.bfloat16)
    n_q = q_ref.shape[2]
    q_pad = jnp.zeros((LANES - n_q, HEAD_PAD), jnp.bfloat16)
    q_tiles = [jnp.concatenate([q_ref[0, h], q_pad], axis=0) for h in range(N_HEADS)]
    zero_q = jnp.zeros((LANES, HEAD_PAD), jnp.bfloat16)
    ones_k = jnp.ones((ONES_ROWS, n_keys_pad), jnp.bfloat16)
    kpe_pair = jnp.concatenate([kpe_shift, kpe_shift], axis=1)
    key_idx = lax.broadcasted_iota(jnp.int32, (n_keys_pad, 2 * LANES), 0)
    accs = []
    for pair in range(N_HEADS // 2):
        h_a, h_b = 2 * pair, 2 * pair + 1
        k_pair = (kk[:, h_a * HEAD_PAD:(h_b + 1) * HEAD_PAD] + kpe_pair).astype(jnp.bfloat16)
        q_blockdiag = jnp.concatenate([jnp.concatenate([q_tiles[h_a], zero_q], axis=1),
                                       jnp.concatenate([zero_q, q_tiles[h_b]], axis=1)], axis=0)
        st = _dot_nt(k_pair, q_blockdiag)
        st = jnp.where(key_idx < n_valid, st, NEG_INF)
        p = jnp.exp2(st - _col_max(st)).astype(jnp.bfloat16)
        vt_pair = jnp.concatenate([vt[h_a * V_DIM:(h_a + 1) * V_DIM, :], ones_k,
                                   vt[h_b * V_DIM:(h_b + 1) * V_DIM, :], ones_k], axis=0)
        acc = _dot(vt_pair, p)
        accs.append(acc[:V_AUG, :LANES])
        accs.append(acc[V_AUG:, LANES:])
    _store_normalised(accs, o_ref)


def _attn_sample_call(q, ckv_m, ckv_cache, ckv_s, kpe_m, kpe_cache, kpe_s, lw, *, layer):
    bsz, _, dec, _ = q.shape
    past = ckv_cache.shape[2]
    n_meta = ckv_m.shape[1]
    n_keys_pad = -(-(n_meta + past + dec) // LANES) * LANES
    return pl.pallas_call(
        functools.partial(_attn_sample_kernel, n_keys_pad=n_keys_pad),
        grid=(bsz,),
        in_specs=[
            pl.BlockSpec((1, N_HEADS, dec, HEAD_PAD), lambda b: (b, 0, 0, 0)),
            pl.BlockSpec((1, n_meta, KV_LORA), lambda b: (0, 0, 0)),
            pl.BlockSpec((1, 1, past, KV_LORA), lambda b: (layer, b, 0, 0)),
            pl.BlockSpec((1, dec, KV_LORA), lambda b: (b, 0, 0)),
            pl.BlockSpec((1, n_meta, ROPE_PAD), lambda b: (0, 0, 0)),
            pl.BlockSpec((1, 1, past, ROPE_PAD), lambda b: (layer, b, 0, 0)),
            pl.BlockSpec((1, dec, ROPE_PAD), lambda b: (b, 0, 0)),
            pl.BlockSpec((KV_LORA, N_HEADS * HEAD_PAD), lambda b: (0, 0)),
            pl.BlockSpec((ATTN_W, KV_LORA), lambda b: (0, 0)),
        ],
        out_specs=pl.BlockSpec((1, LANES, ATTN_W), lambda b: (b, 0, 0)),
        out_shape=jax.ShapeDtypeStruct((bsz, LANES, ATTN_W), jnp.bfloat16),
        compiler_params=pltpu.CompilerParams(
            dimension_semantics=("arbitrary",), vmem_limit_bytes=VMEM_LIMIT_BYTES),
        name="attn_sample",
    )(q, ckv_m, ckv_cache, ckv_s, kpe_m, kpe_cache, kpe_s, lw["w_uk"], lw["w_uvt"])


def _col_max(st):
    rows, cols = st.shape
    slab = SUBLANES
    if rows % slab == 0 and rows > slab:
        st = jnp.max(st.reshape(rows // slab, slab, cols), axis=0)
    return jnp.max(st, axis=0, keepdims=True)


def _dense_state(q_tiles, k_tiles, vt_tiles, n_valid, q_transposed=False):
    n_keys = k_tiles[0].shape[0]
    tq = q_tiles[0].shape[1 if q_transposed else 0]
    n_used = -(-n_valid // ONES_ROWS) * ONES_ROWS
    ones_p = jnp.ones((ONES_ROWS, n_keys), jnp.bfloat16)
    ms, accs = [], []
    for q_t, k_t, vt_t in zip(q_tiles, k_tiles, vt_tiles):
        st = (_dot if q_transposed else _dot_nt)(k_t[:n_used], q_t)
        if n_valid < n_used:
            key_idx = lax.broadcasted_iota(jnp.int32, (n_used, tq), 0)
            st = jnp.where(key_idx < n_valid, st, NEG_INF)
        m = _col_max(st)
        p = jnp.exp2(st - m).astype(jnp.bfloat16)
        if n_used < n_keys:
            p = jnp.concatenate([p, jnp.zeros((n_keys - n_used, tq), jnp.bfloat16)], axis=0)
        ms.append(m)
        accs.append(_dot(jnp.concatenate([vt_t, ones_p], axis=0), p))
    return ms, accs


def _prefix_state(q_tiles, kp_ref, vtp_ref, n_valid, q_transposed=False):
    heads = range(len(q_tiles))
    return _dense_state(q_tiles, [kp_ref[0, hh] for hh in heads],
                        [vtp_ref[0, hh * V_DIM:(hh + 1) * V_DIM, :] for hh in heads], n_valid,
                        q_transposed=q_transposed)


def _store_normalised(accs, o_ref):
    outs = [acc[:V_DIM, :] / acc[V_DIM:V_DIM + 1, :] for acc in accs]
    o_ref[0] = jnp.concatenate(outs, axis=0).T.astype(jnp.bfloat16)


def _attn_dense_kernel(q_ref, kp_ref, vtp_ref, o_ref, *, n_valid):
    q_tiles = [q_ref[0, hh] for hh in range(q_ref.shape[1])]
    _, accs = _prefix_state(q_tiles, kp_ref, vtp_ref, n_valid)
    _store_normalised(accs, o_ref)


def _attn_causal_kernel(q_ref, kp_ref, vtp_ref, k_ref, vt_ref, o_ref,
                        sa_scr, sb_scr, ma_scr, mb_scr, m_scr, acc_scr,
                        *, tq, tk, n_prefix_valid):
    qi = pl.program_id(2)
    n_q = pl.num_programs(2)

    def q_cols(hh, q_idx, q0, qn):
        return q_ref[0, hh, :, pl.ds(pl.multiple_of(q_idx * tq + q0, Q_SLICE), qn)]

    m_scr[...] = jnp.full(m_scr.shape, NEG_INF, jnp.float32)
    acc_scr[...] = jnp.zeros(acc_scr.shape, jnp.float32)
    ones_k = jnp.ones((ONES_ROWS, tk), jnp.bfloat16)

    def scores(tile, s_ref, mx_ref, q_idx=qi, q0=0, qn=Q_SLICE, kn=tk, heads=range(HEADS_PER_STEP)):
        start = pl.multiple_of(tile * tk, tk)
        for hh in heads:
            st = _dot(k_ref[0, hh, pl.ds(start, kn), :], q_cols(hh, q_idx, q0, qn))
            s_ref[hh, q0 // Q_SLICE, 0:kn, :] = st
            mx_ref[hh, :, q0:q0 + qn] = _col_max(st)

    def softmax_pv(tile, s_ref, mx_ref, mask=None, q0=0, qn=Q_SLICE, kn=tk, heads=range(HEADS_PER_STEP)):
        start = pl.multiple_of(tile * tk, tk)
        for hh in heads:
            st = s_ref[hh, q0 // Q_SLICE, 0:kn, :]
            if mask is None:
                m_blk = mx_ref[hh, :, q0:q0 + qn]
            else:
                st = jnp.where(mask, st, NEG_INF)
                m_blk = _col_max(st)
            m_old = m_scr[hh, :, q0:q0 + qn]
            m_new = jnp.maximum(m_old, m_blk)
            p = jnp.exp2(st - m_new).astype(jnp.bfloat16)
            vt_t = vt_ref[0, hh * V_DIM:(hh + 1) * V_DIM, pl.ds(start, kn)]
            vt_aug = jnp.concatenate([vt_t, ones_k[:, :kn]], axis=0)
            acc_scr[hh, :, q0:q0 + qn] = (jnp.exp2(m_old - m_new) * acc_scr[hh, :, q0:q0 + qn]
                                          + _dot(vt_aug, p))
            m_scr[hh, :, q0:q0 + qn] = m_new

    @pl.when(qi == 0)
    def _():
        for q0 in range(0, tq, Q_SLICE):
            scores(0, sa_scr, ma_scr, q0=q0)

    def tile_pair(t0):
        for hh in range(HEADS_PER_STEP):
            for q0 in range(0, tq, Q_SLICE):
                scores(t0 + 1, sb_scr, mb_scr, heads=(hh,), q0=q0, qn=Q_SLICE)
                softmax_pv(t0, sa_scr, ma_scr, heads=(hh,), q0=q0, qn=Q_SLICE)
        for hh in range(HEADS_PER_STEP):
            for q0 in range(0, tq, Q_SLICE):
                scores(t0 + 2, sa_scr, ma_scr, heads=(hh,), q0=q0, qn=Q_SLICE)
                softmax_pv(t0 + 1, sb_scr, mb_scr, heads=(hh,), q0=q0, qn=Q_SLICE)

    def body(jj, carry):
        tile_pair(4 * jj)
        tile_pair(4 * jj + 2)
        return carry

    lax.fori_loop(0, qi // 2, body, 0)

    @pl.when(qi % 2 == 1)
    def _():
        tile_pair(2 * (qi - 1))

    d1 = 2 * qi
    key_chunk = lax.broadcasted_iota(jnp.int32, (tk, Q_SLICE), 0) // CHUNK
    qry_local = lax.broadcasted_iota(jnp.int32, (tk, Q_SLICE), 1)

    def visible_keys(q_off):
        return min(tk, q_off + Q_SLICE)

    def chunk_mask(q_off):
        if q_off >= tk:
            return None
        return (key_chunk <= (qry_local + q_off) // CHUNK)[:visible_keys(q_off)]

    for hh in range(HEADS_PER_STEP):
        for q0 in range(0, tq, Q_SLICE):
            if q0 >= tk:
                scores(d1 + 1, sb_scr, mb_scr, heads=(hh,), q0=q0, qn=Q_SLICE, kn=visible_keys(q0 - tk))
            softmax_pv(d1, sa_scr, ma_scr, mask=chunk_mask(q0), heads=(hh,), q0=q0, qn=Q_SLICE,
                       kn=visible_keys(q0))
    q_next = jnp.minimum(qi + 1, n_q - 1)
    for hh in range(HEADS_PER_STEP):
        for q0 in range(0, tq, Q_SLICE):
            scores(0, sa_scr, ma_scr, q_idx=q_next, heads=(hh,), q0=q0, qn=Q_SLICE)
            if q0 >= tk:
                softmax_pv(d1 + 1, sb_scr, mb_scr, mask=chunk_mask(q0 - tk), heads=(hh,),
                           q0=q0, qn=Q_SLICE, kn=visible_keys(q0 - tk))
    ms_p, accs_p = _prefix_state([q_cols(hh, qi, 0, tq) for hh in range(HEADS_PER_STEP)],
                                 kp_ref, vtp_ref, n_prefix_valid, q_transposed=True)
    accs = []
    for hh in range(HEADS_PER_STEP):
        m_run = m_scr[hh]
        m_all = jnp.maximum(m_run, ms_p[hh])
        accs.append(jnp.exp2(m_run - m_all) * acc_scr[hh] + jnp.exp2(ms_p[hh] - m_all) * accs_p[hh])
    _store_normalised(accs, o_ref)


def _attn_dense_call(q, kp, vtp, *, n_prefix_valid):
    bsz, _, tq, _ = q.shape
    n_prefix = kp.shape[2]
    return pl.pallas_call(
        functools.partial(_attn_dense_kernel, n_valid=n_prefix_valid),
        grid=(bsz,),
        in_specs=[
            pl.BlockSpec((1, N_HEADS, tq, HEAD_PAD), lambda b: (b, 0, 0, 0)),
            pl.BlockSpec((1, N_HEADS, n_prefix, HEAD_PAD), lambda b: (b, 0, 0, 0)),
            pl.BlockSpec((1, ATTN_W, n_prefix), lambda b: (b, 0, 0)),
        ],
        out_specs=pl.BlockSpec((1, tq, ATTN_W), lambda b: (b, 0, 0)),
        out_shape=jax.ShapeDtypeStruct((bsz, tq, ATTN_W), jnp.bfloat16),
        compiler_params=pltpu.CompilerParams(
            dimension_semantics=("arbitrary",), vmem_limit_bytes=VMEM_LIMIT_BYTES),
        name="attn_dense",
    )(q, kp, vtp)


def _attn_causal_call(q, kp, vtp, k, vt, *, tq, n_prefix_valid):
    bsz, _, _, seq = q.shape
    n_prefix = kp.shape[2]
    tk = tq // 2
    f32 = jnp.float32
    return pl.pallas_call(
        functools.partial(_attn_causal_kernel, tq=tq, tk=tk, n_prefix_valid=n_prefix_valid),
        grid=(bsz, N_HEADS // HEADS_PER_STEP, seq // tq),
        in_specs=[
            pl.BlockSpec((1, HEADS_PER_STEP, HEAD_PAD, seq), lambda b, hp, i: (b, hp, 0, 0)),
            pl.BlockSpec((1, HEADS_PER_STEP, n_prefix, HEAD_PAD), lambda b, hp, i: (0, hp, 0, 0)),
            pl.BlockSpec((1, HEADS_PER_STEP * V_DIM, n_prefix), lambda b, hp, i: (0, hp, 0)),
            pl.BlockSpec((1, HEADS_PER_STEP, seq, HEAD_PAD), lambda b, hp, i: (b, hp, 0, 0)),
            pl.BlockSpec((1, HEADS_PER_STEP * V_DIM, seq), lambda b, hp, i: (b, hp, 0)),
        ],
        out_specs=pl.BlockSpec((1, tq, HEADS_PER_STEP * V_DIM), lambda b, hp, i: (b, i, hp)),
        out_shape=jax.ShapeDtypeStruct((bsz, seq, ATTN_W), jnp.bfloat16),
        scratch_shapes=[pltpu.VMEM((HEADS_PER_STEP, tq // Q_SLICE, tk, Q_SLICE), f32),
                        pltpu.VMEM((HEADS_PER_STEP, tq // Q_SLICE, tk, Q_SLICE), f32),
                        pltpu.VMEM((HEADS_PER_STEP, 1, tq), f32),
                        pltpu.VMEM((HEADS_PER_STEP, 1, tq), f32),
                        pltpu.VMEM((HEADS_PER_STEP, 1, tq), f32),
                        pltpu.VMEM((HEADS_PER_STEP, V_AUG, tq), f32)],
        compiler_params=pltpu.CompilerParams(
            dimension_semantics=("arbitrary", "arbitrary", "arbitrary"),
            vmem_limit_bytes=VMEM_LIMIT_BYTES),
        name="attn_causal",
    )(q, kp, vtp, k, vt)


def _out_mlp_kernel(o_ref, y_ref, x_ref, g_attn_ref, w_out_ref, g_mlp_ref, w_up_ref, w_down_ref,
                    g_fin_ref, out_ref, hn_scr, *, final):
    c = pl.program_id(1)

    @pl.when(c == 0)
    def _():
        on = _rms(o_ref[...].astype(jnp.float32), g_attn_ref[...]).astype(jnp.bfloat16)
        h = (x_ref[...] + _dot(on, w_out_ref[0:ATTN_W, :])
             + _dot(y_ref[...], w_out_ref[ATTN_W:ATTN_W + LRU_W, :]))
        hn_scr[...] = _rms(h, g_mlp_ref[...]).astype(jnp.bfloat16)
        out_ref[...] = h

    u = _dot(hn_scr[...], w_up_ref[...].astype(jnp.bfloat16))
    u = jnp.square(jnp.maximum(u, 0.0)).astype(jnp.bfloat16)
    out_ref[...] += _dot(u, w_down_ref[...].astype(jnp.bfloat16))

    if final:
        @pl.when(c == pl.num_programs(1) - 1)
        def _():
            out_ref[...] = _rms(out_ref[...], g_fin_ref[...])


def _out_mlp_call(o, y, x, lw, big, g_fin, *, layer, tm, final):
    rows = x.shape[0]
    const = lambda r, c: (0, 0)
    return pl.pallas_call(
        functools.partial(_out_mlp_kernel, final=final),
        grid=(rows // tm, D_FF // F_CHUNK),
        in_specs=[
            pl.BlockSpec((tm, ATTN_W), lambda r, c: (r, 0)),
            pl.BlockSpec((tm, LRU_W), lambda r, c: (r, 0)),
            pl.BlockSpec((tm, D_MODEL), lambda r, c: (r, 0)),
            pl.BlockSpec((1, ATTN_W), const),
            pl.BlockSpec((None, D_MODEL, D_MODEL), lambda r, c: (layer, 0, 0)),
            pl.BlockSpec((1, D_MODEL), const),
            pl.BlockSpec((None, D_MODEL, F_CHUNK), lambda r, c: (layer, 0, c)),
            pl.BlockSpec((None, F_CHUNK, D_MODEL), lambda r, c: (layer, c, 0)),
            pl.BlockSpec((1, D_MODEL), const),
        ],
        out_specs=pl.BlockSpec((tm, D_MODEL), lambda r, c: (r, 0)),
        out_shape=jax.ShapeDtypeStruct((rows, D_MODEL), jnp.float32),
        scratch_shapes=[pltpu.VMEM((tm, D_MODEL), jnp.bfloat16)],
        compiler_params=pltpu.CompilerParams(
            dimension_semantics=("arbitrary", "arbitrary"), vmem_limit_bytes=VMEM_LIMIT_BYTES),
        name="out_mlp",
    )(o, y, x, lw["g_attn"], big["w_out"], lw["g_mlp"], big["w_up"], big["w_down"], g_fin)


def _rope_tables(pos, rope_lane0, pass_lanes, scale):
    half = QK_ROPE // 2
    n = pos.shape[0]
    inv = ROPE_THETA ** (-jnp.arange(0, QK_ROPE, 2, dtype=jnp.float32) / QK_ROPE)
    ang = pos.astype(jnp.float32)[:, None] * inv[None, :]
    cos, sin = jnp.cos(ang) * scale, jnp.sin(ang) * scale
    fill = lambda width, value=0.0: jnp.full((n, width), value, jnp.float32)
    tail = LANES - rope_lane0 - QK_ROPE
    tab_c = jnp.concatenate([fill(pass_lanes, scale), fill(rope_lane0 - pass_lanes), cos, cos, fill(tail)],
                            axis=1)
    tab_m = jnp.concatenate([fill(rope_lane0), -sin, fill(half + tail)], axis=1)
    tab_p = jnp.concatenate([fill(rope_lane0 + half), sin, fill(tail)], axis=1)
    return jnp.stack([tab_c, tab_m, tab_p])


def _key_tables(pos):
    return _rope_tables(pos, 0, 0, 1.0)


def _query_tables(pos):
    return _rope_tables(pos, QK_NOPE, QK_NOPE, SM_SCALE * LOG2_E)


def _block_diag_gates(w_a, w_x):
    per_half = LRU_BLOCKS // 2
    halves = []
    for s in range(2):
        bd_a = jax.scipy.linalg.block_diag(*[w_a[s * per_half + i] for i in range(per_half)])
        bd_x = jax.scipy.linalg.block_diag(*[w_x[s * per_half + i] for i in range(per_half)])
        halves.append(jnp.concatenate([bd_a, bd_x], axis=1))
    return jnp.stack(halves).astype(jnp.bfloat16)


def _layer_weights(l, norm_mix_g, w_in, q_norm_g, w_uq, kv_norm_g, w_ukv, conv_w, conv_b,
                   w_gate_a, b_gate_a, w_gate_x, b_gate_x, lru_lambda, attn_out_g, lru_out_g,
                   norm_mlp_g):
    bf = jnp.bfloat16
    row = lambda v: v.reshape(1, -1).astype(jnp.float32)
    split = COL_KR + QK_ROPE
    w_in_p = (jnp.pad(w_in[l][:, :split], ((0, 0), (0, IN_W_PAD - split)))
              + jnp.pad(w_in[l][:, split:], ((0, 0), (split + ROPE_PAD - QK_ROPE, 0)))).astype(bf)
    w_uq_p = jnp.pad(w_uq[l].reshape(Q_LORA, N_HEADS, QK_NOPE + QK_ROPE),
                     ((0, 0), (0, 0), (0, HEAD_PAD - QK_NOPE - QK_ROPE))).reshape(Q_LORA, -1).astype(bf)
    w_ukv_h = w_ukv[l].reshape(KV_LORA, N_HEADS, QK_NOPE + V_DIM)
    w_uk_p = jnp.pad(w_ukv_h[:, :, :QK_NOPE],
                     ((0, 0), (0, 0), (0, HEAD_PAD - QK_NOPE))).reshape(KV_LORA, -1).astype(bf)
    w_uvt = w_ukv_h[:, :, QK_NOPE:].reshape(KV_LORA, ATTN_W).T.astype(bf)
    return {
        "g_mix": row(norm_mix_g[l]), "w_in": w_in_p, "g_q": row(q_norm_g[l]), "w_uq": w_uq_p, "w_uq_t": w_uq_p.T,
        "g_kv": row(kv_norm_g[l]), "w_uk": w_uk_p, "w_uvt": w_uvt,
        "conv_w": conv_w[l].astype(jnp.float32), "conv_b": row(conv_b[l]),
        "wg": _block_diag_gates(w_gate_a[l], w_gate_x[l]),
        "bg": jnp.stack([b_gate_a[l], b_gate_x[l]]).astype(jnp.float32),
        "lam": row(lru_lambda[l]), "g_lru": row(lru_out_g[l]), "g_attn": row(attn_out_g[l]),
        "g_mlp": row(norm_mlp_g[l]),
    }


def _pad_buf(buf):
    return jnp.pad(buf, ((0, 0), (SUBLANES - (CONV_W - 1), 0), (0, 0)))


def _pad_axis(a, axis, size):
    pad = [(0, 0)] * a.ndim
    pad[axis] = (0, size - a.shape[axis])
    return jnp.pad(a, pad)


def kernel(x_prompt, x_sample, cache_ckv, cache_kpe, state_lru_h, state_conv, meta_tokens,
           norm_mix_g, w_in, q_norm_g, w_uq, kv_norm_g, w_ukv, conv_w, conv_b,
           w_gate_a, b_gate_a, w_gate_x, b_gate_x, lru_lambda, attn_out_g, lru_out_g,
           w_out, norm_mlp_g, w_up, w_down, final_norm_g):
    b_p, seq, _ = x_prompt.shape
    b_s, dec_seq, _ = x_sample.shape
    past_len = cache_ckv.shape[2]
    assert (past_len + dec_seq - 1) // CHUNK == past_len // CHUNK
    tm_p, tq_p, tm_o = PROMPT_MIXER_ROWS, PROMPT_QUERY_ROWS, PROMPT_MLP_ROWS
    assert seq % tm_p == 0 and seq % tq_p == 0 and (b_p * seq) % tm_o == 0
    assert (tq_p // 2) % CHUNK == 0 and tq_p % Q_SLICE == 0

    meta_pos = jnp.arange(-N_META, 0, dtype=jnp.int32)
    prompt_pos = jnp.arange(seq, dtype=jnp.int32)
    sample_pos = past_len + jnp.arange(dec_seq, dtype=jnp.int32)
    tabk_m, tabk_p, tabk_s = _key_tables(meta_pos), _key_tables(prompt_pos), _key_tables(sample_pos)
    tabq_m, tabq_p, tabq_s = _query_tables(meta_pos), _query_tables(prompt_pos), _query_tables(sample_pos)
    tabq_p_t = jnp.swapaxes(tabq_p, 1, 2)
    g_fin = final_norm_g.reshape(1, -1).astype(jnp.float32)

    cache_kpe_pad = _pad_axis(cache_kpe, 3, ROPE_PAD)
    big = {"w_out": w_out.astype(jnp.bfloat16), "w_up": w_up, "w_down": w_down}
    tq_small = LANES

    h_meta = meta_tokens[None].astype(jnp.float32)
    h_p, h_s = x_prompt, x_sample
    zero_h = jnp.zeros((1, 1, LRU_W), jnp.float32)
    zero_buf = jnp.zeros((1, SUBLANES, LRU_W), jnp.float32)
    outs = {name: [] for name in ("ckv_p", "kpe_p", "lru_p", "conv_p", "ckv_s", "kpe_s", "lru_s", "conv_s")}
    for l in range(DEPTH):
        lw = _layer_weights(l, norm_mix_g, w_in, q_norm_g, w_uq, kv_norm_g, w_ukv, conv_w, conv_b,
                            w_gate_a, b_gate_a, w_gate_x, b_gate_x, lru_lambda, attn_out_g,
                            lru_out_g, norm_mlp_g)
        last = l + 1 == DEPTH
        mq, mk, mvt, m_ckv, m_kpe, m_y, m_h, m_buf = _mixer_call(
            h_meta, zero_h, zero_buf, tabq_m, tabk_m, lw, tm=N_META, shared_state=True)
        pq, pk, pvt, p_ckv, p_kpe, p_y, p_h, p_buf = _mixer_call(
            h_p, m_h, _pad_buf(m_buf), tabq_p_t, tabk_p, lw, tm=tm_p, shared_state=True,
            q_transposed=True)
        mk_p, mvt_p = _pad_axis(mk, 2, LANES), _pad_axis(mvt, 2, LANES)
        p_o = _attn_causal_call(pq, mk_p, mvt_p, pk, pvt, tq=tq_p, n_prefix_valid=N_META)
        h_p = _out_mlp_call(p_o.reshape(b_p * seq, ATTN_W), p_y.reshape(b_p * seq, LRU_W),
                            h_p.reshape(b_p * seq, D_MODEL), lw, big, g_fin, layer=l, tm=tm_o,
                            final=last).reshape(b_p, seq, D_MODEL)
        sq, sk, svt, s_ckv, s_kpe, s_y, s_h, s_buf = _mixer_call(
            h_s, state_lru_h[l][:, None, :], _pad_buf(state_conv[l]), tabq_s, tabk_s, lw,
            tm=dec_seq, shared_state=False)
        s_o = _attn_sample_call(sq, m_ckv, cache_ckv, s_ckv, m_kpe, cache_kpe_pad, s_kpe, lw,
                                layer=l)[:, :dec_seq]
        h_s = _out_mlp_call(s_o.reshape(b_s * dec_seq, ATTN_W), s_y.reshape(b_s * dec_seq, LRU_W),
                            h_s.reshape(b_s * dec_seq, D_MODEL), lw, big, g_fin, layer=l,
                            tm=b_s * dec_seq, final=last).reshape(b_s, dec_seq, D_MODEL)
        if not last:
            m_o = _attn_dense_call(_pad_axis(mq, 2, tq_small), mk_p, mvt_p,
                                   n_prefix_valid=N_META)[:, :N_META]
            h_meta = _out_mlp_call(m_o.reshape(N_META, ATTN_W), m_y.reshape(N_META, LRU_W),
                                   h_meta.reshape(N_META, D_MODEL), lw, big, g_fin, layer=l,
                                   tm=N_META, final=False).reshape(1, N_META, D_MODEL)
        outs["ckv_p"].append(p_ckv); outs["kpe_p"].append(p_kpe[:, :, :QK_ROPE])
        outs["lru_p"].append(p_h[:, 0]); outs["conv_p"].append(p_buf)
        outs["ckv_s"].append(s_ckv); outs["kpe_s"].append(s_kpe[:, :, :QK_ROPE])
        outs["lru_s"].append(s_h[:, 0]); outs["conv_s"].append(s_buf)

    return (h_p, h_s, jnp.stack(outs["ckv_p"]), jnp.stack(outs["kpe_p"]), jnp.stack(outs["lru_p"]),
            jnp.stack(outs["conv_p"]), jnp.stack(outs["ckv_s"]), jnp.stack(outs["kpe_s"]),
            jnp.stack(outs["lru_s"]), jnp.stack(outs["conv_s"]))
```

```python
import functools

import jax
import jax.numpy as jnp
from jax import lax
from jax.experimental import pallas as pl
from jax.experimental.pallas import tpu as pltpu

D_MODEL = 1024
DEPTH = 2
CHUNK = 64
N_META = 16
ATTN_W = 512
LRU_W = 512
V_DIM = 64
N_HEADS = 8
QK_NOPE = 64
QK_ROPE = 32
Q_LORA = 768
KV_LORA = 256
LRU_BLOCKS = 8
LRU_BW = LRU_W // LRU_BLOCKS
LRU_C = 8.0
CONV_W = 4
D_FF = 4 * D_MODEL
ROPE_THETA = 10000.0
EPS = 1e-6
SM_SCALE = (QK_NOPE + QK_ROPE) ** -0.5
NEG_INF = -1e30
LOG2_E = 1.4426950408889634

LANES = 128
SUBLANES = 8
HEAD_PAD = LANES
VMEM_LIMIT_BYTES = 56 * 1024 * 1024

ROPE_PAD = LANES
COL_CQ = 0
COL_CKV = Q_LORA
COL_KR = Q_LORA + KV_LORA
COL_XR = COL_KR + ROPE_PAD
COL_GR = COL_XR + LRU_W
IN_W_PAD = COL_GR + LRU_W

HEADS_PER_STEP = 2
ONES_ROWS = 16
V_AUG = V_DIM + ONES_ROWS
F_CHUNK = 1024
Q_SLICE = 256
PROMPT_MIXER_ROWS = 512
PROMPT_QUERY_ROWS = 1024
PROMPT_MLP_ROWS = 1024


def _rms(x, g):
    return x * lax.rsqrt(jnp.mean(x * x, axis=-1, keepdims=True) + EPS) * g


def _rope_lanes(x, tab_ref, axis=1):
    half = QK_ROPE // 2
    return (x * tab_ref[0] + pltpu.roll(x, LANES - half, axis) * tab_ref[1]
            + pltpu.roll(x, half, axis) * tab_ref[2])


def _dot(a, b):
    return jnp.dot(a, b, preferred_element_type=jnp.float32)


def _dot_nt(a, b):
    return lax.dot_general(a, b, (((1,), (1,)), ((), ())), preferred_element_type=jnp.float32)


def _mixer_kernel(x_ref, h0_ref, buf0_ref, tabq_ref, tabk_ref, g_mix_ref, w_in_ref, g_q_ref,
                  w_uq_ref, g_kv_ref, w_uk_ref, w_uvt_ref, conv_w_ref, conv_b_ref, wg_ref, bg_ref,
                  lam_ref, g_lru_ref,
                  q_ref, k_ref, vt_ref, ckv_ref, kpe_ref, y_ref, hlast_ref, buf_ref,
                  tail_scr, hcar_scr, *, tm, q_transposed):
    t = pl.program_id(1)

    @pl.when(t == 0)
    def _():
        tail_scr[...] = buf0_ref[0]
        hcar_scr[...] = h0_ref[0]

    x = x_ref[0]
    xn = _rms(x, g_mix_ref[...]).astype(jnp.bfloat16)
    z_lru = _dot(xn, w_in_ref[:, COL_XR:IN_W_PAD])

    x_r = z_lru[:, :LRU_W]
    g_r = z_lru[:, LRU_W:]
    n_slabs = tm // SUBLANES
    rows8 = lax.broadcasted_iota(jnp.int32, (SUBLANES, LRU_W), 0)
    slabs = [tail_scr[...]] + [x_r[g * SUBLANES:(g + 1) * SUBLANES, :] for g in range(n_slabs)]
    xc = conv_b_ref[...] + x_r * conv_w_ref[CONV_W - 1:CONV_W, :]
    for back in range(1, CONV_W):
        rolled = [pltpu.roll(s, back, 0) for s in slabs]
        shifted = jnp.concatenate(
            [jnp.where(rows8 < back, rolled[g], rolled[g + 1]) for g in range(n_slabs)], axis=0)
        xc = xc + shifted * conv_w_ref[CONV_W - 1 - back:CONV_W - back, :]
    tail_scr[...] = slabs[-1]
    buf_ref[0] = tail_scr[SUBLANES - (CONV_W - 1):SUBLANES, :]

    half_w = LRU_W // 2
    gates = [_dot(xc[:, s * half_w:(s + 1) * half_w].astype(jnp.bfloat16), wg_ref[s]) for s in range(2)]
    pre_a = jnp.concatenate([gates[0][:, :half_w], gates[1][:, :half_w]], axis=1)
    pre_x = jnp.concatenate([gates[0][:, half_w:], gates[1][:, half_w:]], axis=1)
    r = jax.nn.sigmoid(pre_a + bg_ref[0:1, :])
    gi = jax.nn.sigmoid(pre_x + bg_ref[1:2, :])
    neg_lam = -lam_ref[...]
    softplus = jnp.maximum(neg_lam, 0.0) + jnp.log1p(jnp.exp(-jnp.abs(neg_lam)))
    log_a = (-LRU_C) * r * softplus
    a = jnp.exp(log_a)
    one_m_a2 = -jnp.tanh(log_a) * (a * a + 1.0)
    root = jnp.where(one_m_a2 > 0.0, one_m_a2 * lax.rsqrt(one_m_a2), 0.0)
    b = root * (gi * xc)

    carry = jnp.broadcast_to(hcar_scr[...], (SUBLANES, LRU_W))
    h_slabs = []
    for g in range(n_slabs):
        a_g = a[g * SUBLANES:(g + 1) * SUBLANES, :]
        b_g = b[g * SUBLANES:(g + 1) * SUBLANES, :]
        shift = 1
        while shift < SUBLANES:
            keep = rows8 >= shift
            a_prev = jnp.where(keep, pltpu.roll(a_g, shift, 0), 1.0)
            b_prev = jnp.where(keep, pltpu.roll(b_g, shift, 0), 0.0)
            b_g = a_g * b_prev + b_g
            a_g = a_g * a_prev
            shift *= 2
        h_g = a_g * carry + b_g
        h_slabs.append(h_g)
        carry = jnp.broadcast_to(h_g[SUBLANES - 1:SUBLANES, :], (SUBLANES, LRU_W))
    hh = jnp.concatenate(h_slabs, axis=0)
    h_last = carry[0:1, :]
    hcar_scr[...] = h_last
    hlast_ref[0] = h_last
    y = hh * jax.nn.gelu(g_r)
    y_ref[0] = _rms(y, g_lru_ref[...]).astype(jnp.bfloat16)

    z = _dot(xn, w_in_ref[:, 0:COL_XR])
    cqn = _rms(z[:, COL_CQ:COL_CQ + Q_LORA], g_q_ref[...]).astype(jnp.bfloat16)
    if q_transposed:
        q_t = _dot_nt(w_uq_ref[...], cqn)
        for h in range(N_HEADS):
            qh = q_t[h * HEAD_PAD:(h + 1) * HEAD_PAD, :]
            q_ref[0, h] = _rope_lanes(qh, tabq_ref, axis=0).astype(jnp.bfloat16)
    else:
        q = _dot(cqn, w_uq_ref[...])
        for h in range(N_HEADS):
            qh = q[:, h * HEAD_PAD:(h + 1) * HEAD_PAD]
            q_ref[0, h] = _rope_lanes(qh, tabq_ref).astype(jnp.bfloat16)

    ckv = _rms(z[:, COL_CKV:COL_CKV + KV_LORA], g_kv_ref[...])
    ckv_ref[0] = ckv
    ckv_b = ckv.astype(jnp.bfloat16)
    kpe = _rope_lanes(z[:, COL_KR:COL_KR + ROPE_PAD], tabk_ref)
    kpe_ref[0] = kpe
    kpe_shift = pltpu.roll(kpe, QK_NOPE, 1)
    kk = _dot(ckv_b, w_uk_ref[...])
    for h in range(N_HEADS):
        k_ref[0, h] = (kk[:, h * HEAD_PAD:(h + 1) * HEAD_PAD] + kpe_shift).astype(jnp.bfloat16)
    vt_ref[0] = _dot_nt(w_uvt_ref[...], ckv_b).astype(jnp.bfloat16)


def _mixer_call(x, h0, buf0p, tabq, tabk, lw, *, tm, shared_state, q_transposed=False):
    bsz, seq, _ = x.shape
    nt = seq // tm
    const2 = lambda b, t: (0, 0)
    const3 = lambda b, t: (0, 0, 0)
    state_idx = (lambda b, t: (0, 0, 0)) if shared_state else (lambda b, t: (b, 0, 0))
    in_specs = [
        pl.BlockSpec((1, tm, D_MODEL), lambda b, t: (b, t, 0)),
        pl.BlockSpec((1, 1, LRU_W), state_idx),
        pl.BlockSpec((1, SUBLANES, LRU_W), state_idx),
        (pl.BlockSpec((3, LANES, tm), lambda b, t: (0, 0, t)) if q_transposed
         else pl.BlockSpec((3, tm, LANES), lambda b, t: (0, t, 0))),
        pl.BlockSpec((3, tm, LANES), lambda b, t: (0, t, 0)),
        pl.BlockSpec((1, D_MODEL), const2),
        pl.BlockSpec((D_MODEL, IN_W_PAD), const2),
        pl.BlockSpec((1, Q_LORA), const2),
        pl.BlockSpec((N_HEADS * HEAD_PAD, Q_LORA) if q_transposed else (Q_LORA, N_HEADS * HEAD_PAD), const2),
        pl.BlockSpec((1, KV_LORA), const2),
        pl.BlockSpec((KV_LORA, N_HEADS * HEAD_PAD), const2),
        pl.BlockSpec((ATTN_W, KV_LORA), const2),
        pl.BlockSpec((CONV_W, LRU_W), const2),
        pl.BlockSpec((1, LRU_W), const2),
        pl.BlockSpec((2, LRU_W // 2, LRU_W), const3),
        pl.BlockSpec((2, LRU_W), const2),
        pl.BlockSpec((1, LRU_W), const2),
        pl.BlockSpec((1, LRU_W), const2),
    ]
    out_shape = [
        jax.ShapeDtypeStruct((bsz, N_HEADS, HEAD_PAD, seq) if q_transposed
                             else (bsz, N_HEADS, seq, HEAD_PAD), jnp.bfloat16),
        jax.ShapeDtypeStruct((bsz, N_HEADS, seq, HEAD_PAD), jnp.bfloat16),
        jax.ShapeDtypeStruct((bsz, ATTN_W, seq), jnp.bfloat16),
        jax.ShapeDtypeStruct((bsz, seq, KV_LORA), jnp.float32),
        jax.ShapeDtypeStruct((bsz, seq, ROPE_PAD), jnp.float32),
        jax.ShapeDtypeStruct((bsz, seq, LRU_W), jnp.bfloat16),
        jax.ShapeDtypeStruct((bsz, 1, LRU_W), jnp.float32),
        jax.ShapeDtypeStruct((bsz, CONV_W - 1, LRU_W), jnp.float32),
    ]
    out_specs = [
        (pl.BlockSpec((1, N_HEADS, HEAD_PAD, tm), lambda b, t: (b, 0, 0, t)) if q_transposed
         else pl.BlockSpec((1, N_HEADS, tm, HEAD_PAD), lambda b, t: (b, 0, t, 0))),
        pl.BlockSpec((1, N_HEADS, tm, HEAD_PAD), lambda b, t: (b, 0, t, 0)),
        pl.BlockSpec((1, ATTN_W, tm), lambda b, t: (b, 0, t)),
        pl.BlockSpec((1, tm, KV_LORA), lambda b, t: (b, t, 0)),
        pl.BlockSpec((1, tm, ROPE_PAD), lambda b, t: (b, t, 0)),
        pl.BlockSpec((1, tm, LRU_W), lambda b, t: (b, t, 0)),
        pl.BlockSpec((1, 1, LRU_W), lambda b, t: (b, 0, 0)),
        pl.BlockSpec((1, CONV_W - 1, LRU_W), lambda b, t: (b, 0, 0)),
    ]
    return pl.pallas_call(
        functools.partial(_mixer_kernel, tm=tm, q_transposed=q_transposed),
        grid=(bsz, nt),
        in_specs=in_specs,
        out_specs=out_specs,
        out_shape=out_shape,
        scratch_shapes=[pltpu.VMEM((SUBLANES, LRU_W), jnp.float32),
                        pltpu.VMEM((1, LRU_W), jnp.float32)],
        compiler_params=pltpu.CompilerParams(
            dimension_semantics=("arbitrary", "arbitrary"), vmem_limit_bytes=VMEM_LIMIT_BYTES),
        name="mixer",
    )(x, h0, buf0p, tabq, tabk, lw["g_mix"], lw["w_in"], lw["g_q"],
      lw["w_uq_t"] if q_transposed else lw["w_uq"], lw["g_kv"],
      lw["w_uk"], lw["w_uvt"], lw["conv_w"], lw["conv_b"], lw["wg"], lw["bg"], lw["lam"], lw["g_lru"])


def _attn_sample_kernel(q_ref, ckv_m_ref, ckv_c_ref, ckv_s_ref, kpe_m_ref, kpe_c_ref, kpe_s_ref,
                        w_uk_ref, w_uvt_ref, o_ref, *, n_keys_pad):
    ckv_parts = [ckv_m_ref[0], ckv_c_ref[0, 0], ckv_s_ref[0]]
    kpe_parts = [kpe_m_ref[0], kpe_c_ref[0, 0], kpe_s_ref[0]]
    n_valid = sum(p.shape[0] for p in ckv_parts)
    pad = n_keys_pad - n_valid
    ckv = jnp.concatenate(ckv_parts + [jnp.zeros((pad, KV_LORA), jnp.float32)], axis=0)
    kpe = jnp.concatenate(kpe_parts + [jnp.zeros((pad, ROPE_PAD), jnp.float32)], axis=0)
    ckv_b = ckv.astype(jnp.bfloat16)
    kpe_shift = pltpu.roll(kpe, QK_NOPE, 1)
    kk = _dot(ckv_b, w_uk_ref[...])
    vt = _dot_nt(w_uvt_ref[...], ckv_b).astype(jnp.bfloat16)
    n_q = q_ref.shape[2]
    q_pad = jnp.zeros((LANES - n_q, HEAD_PAD), jnp.bfloat16)
    q_tiles = [jnp.concatenate([q_ref[0, h], q_pad], axis=0) for h in range(N_HEADS)]
    zero_q = jnp.zeros((LANES, HEAD_PAD), jnp.bfloat16)
    ones_k = jnp.ones((ONES_ROWS, n_keys_pad), jnp.bfloat16)
    kpe_pair = jnp.concatenate([kpe_shift, kpe_shift], axis=1)
    key_idx = lax.broadcasted_iota(jnp.int32, (n_keys_pad, 2 * LANES), 0)
    accs = []
    for pair in range(N_HEADS // 2):
        h_a, h_b = 2 * pair, 2 * pair + 1
        k_pair = (kk[:, h_a * HEAD_PAD:(h_b + 1) * HEAD_PAD] + kpe_pair).astype(jnp.bfloat16)
        q_blockdiag = jnp.concatenate([jnp.concatenate([q_tiles[h_a], zero_q], axis=1),
                                       jnp.concatenate([zero_q, q_tiles[h_b]], axis=1)], axis=0)
        st = _dot_nt(k_pair, q_blockdiag)
        st = jnp.where(key_idx < n_valid, st, NEG_INF)
        p = jnp.exp2(st - _col_max(st)).astype(jnp.bfloat16)
        vt_pair = jnp.concatenate([vt[h_a * V_DIM:(h_a + 1) * V_DIM, :], ones_k,
                                   vt[h_b * V_DIM:(h_b + 1) * V_DIM, :], ones_k], axis=0)
        acc = _dot(vt_pair, p)
        accs.append(acc[:V_AUG, :LANES])
        accs.append(acc[V_AUG:, LANES:])
    _store_normalised(accs, o_ref)


def _attn_sample_call(q, ckv_m, ckv_cache, ckv_s, kpe_m, kpe_cache, kpe_s, lw, *, layer):
    bsz, _, dec, _ = q.shape
    past = ckv_cache.shape[2]
    n_meta = ckv_m.shape[1]
    n_keys_pad = -(-(n_meta + past + dec) // LANES) * LANES
    return pl.pallas_call(
        functools.partial(_attn_sample_kernel, n_keys_pad=n_keys_pad),
        grid=(bsz,),
        in_specs=[
            pl.BlockSpec((1, N_HEADS, dec, HEAD_PAD), lambda b: (b, 0, 0, 0)),
            pl.BlockSpec((1, n_meta, KV_LORA), lambda b: (0, 0, 0)),
            pl.BlockSpec((1, 1, past, KV_LORA), lambda b: (layer, b, 0, 0)),
            pl.BlockSpec((1, dec, KV_LORA), lambda b: (b, 0, 0)),
            pl.BlockSpec((1, n_meta, ROPE_PAD), lambda b: (0, 0, 0)),
            pl.BlockSpec((1, 1, past, ROPE_PAD), lambda b: (layer, b, 0, 0)),
            pl.BlockSpec((1, dec, ROPE_PAD), lambda b: (b, 0, 0)),
            pl.BlockSpec((KV_LORA, N_HEADS * HEAD_PAD), lambda b: (0, 0)),
            pl.BlockSpec((ATTN_W, KV_LORA), lambda b: (0, 0)),
        ],
        out_specs=pl.BlockSpec((1, LANES, ATTN_W), lambda b: (b, 0, 0)),
        out_shape=jax.ShapeDtypeStruct((bsz, LANES, ATTN_W), jnp.bfloat16),
        compiler_params=pltpu.CompilerParams(
            dimension_semantics=("arbitrary",), vmem_limit_bytes=VMEM_LIMIT_BYTES),
        name="attn_sample",
    )(q, ckv_m, ckv_cache, ckv_s, kpe_m, kpe_cache, kpe_s, lw["w_uk"], lw["w_uvt"])


def _col_max(st):
    rows, cols = st.shape
    slab = SUBLANES
    if rows % slab == 0 and rows > slab:
        st = jnp.max(st.reshape(rows // slab, slab, cols), axis=0)
    return jnp.max(st, axis=0, keepdims=True)


def _dense_state(q_tiles, k_tiles, vt_tiles, n_valid, q_transposed=False):
    n_keys = k_tiles[0].shape[0]
    tq = q_tiles[0].shape[1 if q_transposed else 0]
    n_used = -(-n_valid // ONES_ROWS) * ONES_ROWS
    ones_p = jnp.ones((ONES_ROWS, n_keys), jnp.bfloat16)
    ms, accs = [], []
    for q_t, k_t, vt_t in zip(q_tiles, k_tiles, vt_tiles):
        st = (_dot if q_transposed else _dot_nt)(k_t[:n_used], q_t)
        if n_valid < n_used:
            key_idx = lax.broadcasted_iota(jnp.int32, (n_used, tq), 0)
            st = jnp.where(key_idx < n_valid, st, NEG_INF)
        m = _col_max(st)
        p = jnp.exp2(st - m).astype(jnp.bfloat16)
        if n_used < n_keys:
            p = jnp.concatenate([p, jnp.zeros((n_keys - n_used, tq), jnp.bfloat16)], axis=0)
        ms.append(m)
        accs.append(_dot(jnp.concatenate([vt_t, ones_p], axis=0), p))
    return ms, accs


def _prefix_state(q_tiles, kp_ref, vtp_ref, n_valid, q_transposed=False):
    heads = range(len(q_tiles))
    return _dense_state(q_tiles, [kp_ref[0, hh] for hh in heads],
                        [vtp_ref[0, hh * V_DIM:(hh + 1) * V_DIM, :] for hh in heads], n_valid,
                        q_transposed=q_transposed)


def _store_normalised(accs, o_ref):
    outs = [acc[:V_DIM, :] / acc[V_DIM:V_DIM + 1, :] for acc in accs]
    o_ref[0] = jnp.concatenate(outs, axis=0).T.astype(jnp.bfloat16)


def _attn_dense_kernel(q_ref, kp_ref, vtp_ref, o_ref, *, n_valid):
    q_tiles = [q_ref[0, hh] for hh in range(q_ref.shape[1])]
    _, accs = _prefix_state(q_tiles, kp_ref, vtp_ref, n_valid)
    _store_normalised(accs, o_ref)


def _attn_causal_kernel(q_ref, kp_ref, vtp_ref, k_ref, vt_ref, o_ref,
                        sa_scr, sb_scr, ma_scr, mb_scr, m_scr, acc_scr,
                        *, tq, tk, n_prefix_valid):
    qi = pl.program_id(2)
    n_q = pl.num_programs(2)

    def q_cols(hh, q_idx, q0, qn):
        return q_ref[0, hh, :, pl.ds(pl.multiple_of(q_idx * tq + q0, Q_SLICE), qn)]

    m_scr[...] = jnp.full(m_scr.shape, NEG_INF, jnp.float32)
    acc_scr[...] = jnp.zeros(acc_scr.shape, jnp.float32)
    ones_k = jnp.ones((ONES_ROWS, tk), jnp.bfloat16)

    def scores(tile, s_ref, mx_ref, q_idx=qi, q0=0, qn=Q_SLICE, kn=tk, heads=range(HEADS_PER_STEP)):
        start = pl.multiple_of(tile * tk, tk)
        for hh in heads:
            st = _dot(k_ref[0, hh, pl.ds(start, kn), :], q_cols(hh, q_idx, q0, qn))
            s_ref[hh, q0 // Q_SLICE, 0:kn, :] = st
            mx_ref[hh, :, q0:q0 + qn] = _col_max(st)

    def softmax_pv(tile, s_ref, mx_ref, mask=None, q0=0, qn=Q_SLICE, kn=tk, heads=range(HEADS_PER_STEP)):
        start = pl.multiple_of(tile * tk, tk)
        for hh in heads:
            st = s_ref[hh, q0 // Q_SLICE, 0:kn, :]
            if mask is None:
                m_blk = mx_ref[hh, :, q0:q0 + qn]
            else:
                st = jnp.where(mask, st, NEG_INF)
                m_blk = _col_max(st)
            m_old = m_scr[hh, :, q0:q0 + qn]
            m_new = jnp.maximum(m_old, m_blk)
            p = jnp.exp2(st - m_new).astype(jnp.bfloat16)
            vt_t = vt_ref[0, hh * V_DIM:(hh + 1) * V_DIM, pl.ds(start, kn)]
            vt_aug = jnp.concatenate([vt_t, ones_k[:, :kn]], axis=0)
            acc_scr[hh, :, q0:q0 + qn] = (jnp.exp2(m_old - m_new) * acc_scr[hh, :, q0:q0 + qn]
                                          + _dot(vt_aug, p))
            m_scr[hh, :, q0:q0 + qn] = m_new

    @pl.when(qi == 0)
    def _():
        for q0 in range(0, tq, Q_SLICE):
            scores(0, sa_scr, ma_scr, q0=q0)

    def tile_pair(t0):
        for hh in range(HEADS_PER_STEP):
            for q0 in range(0, tq, Q_SLICE):
                scores(t0 + 1, sb_scr, mb_scr, heads=(hh,), q0=q0, qn=Q_SLICE)
                softmax_pv(t0, sa_scr, ma_scr, heads=(hh,), q0=q0, qn=Q_SLICE)
        for hh in range(HEADS_PER_STEP):
            for q0 in range(0, tq, Q_SLICE):
                scores(t0 + 2, sa_scr, ma_scr, heads=(hh,), q0=q0, qn=Q_SLICE)
                softmax_pv(t0 + 1, sb_scr, mb_scr, heads=(hh,), q0=q0, qn=Q_SLICE)

    def body(jj, carry):
        tile_pair(4 * jj)
        tile_pair(4 * jj + 2)
        return carry

    lax.fori_loop(0, qi // 2, body, 0)

    @pl.when(qi % 2 == 1)
    def _():
        tile_pair(2 * (qi - 1))

    d1 = 2 * qi
    key_chunk = lax.broadcasted_iota(jnp.int32, (tk, Q_SLICE), 0) // CHUNK
    qry_local = lax.broadcasted_iota(jnp.int32, (tk, Q_SLICE), 1)

    def visible_keys(q_off):
        return min(tk, q_off + Q_SLICE)

    def chunk_mask(q_off):
        if q_off >= tk:
            return None
        return (key_chunk <= (qry_local + q_off) // CHUNK)[:visible_keys(q_off)]

    for hh in range(HEADS_PER_STEP):
        for q0 in range(0, tq, Q_SLICE):
            if q0 >= tk:
                scores(d1 + 1, sb_scr, mb_scr, heads=(hh,), q0=q0, qn=Q_SLICE, kn=visible_keys(q0 - tk))
            softmax_pv(d1, sa_scr, ma_scr, mask=chunk_mask(q0), heads=(hh,), q0=q0, qn=Q_SLICE,
                       kn=visible_keys(q0))
    q_next = jnp.minimum(qi + 1, n_q - 1)
    for hh in range(HEADS_PER_STEP):
        for q0 in range(0, tq, Q_SLICE):
            scores(0, sa_scr, ma_scr, q_idx=q_next, heads=(hh,), q0=q0, qn=Q_SLICE)
            if q0 >= tk:
                softmax_pv(d1 + 1, sb_scr, mb_scr, mask=chunk_mask(q0 - tk), heads=(hh,),
                           q0=q0, qn=Q_SLICE, kn=visible_keys(q0 - tk))
    ms_p, accs_p = _prefix_state([q_cols(hh, qi, 0, tq) for hh in range(HEADS_PER_STEP)],
                                 kp_ref, vtp_ref, n_prefix_valid, q_transposed=True)
    accs = []
    for hh in range(HEADS_PER_STEP):
        m_run = m_scr[hh]
        m_all = jnp.maximum(m_run, ms_p[hh])
        accs.append(jnp.exp2(m_run - m_all) * acc_scr[hh] + jnp.exp2(ms_p[hh] - m_all) * accs_p[hh])
    _store_normalised(accs, o_ref)


def _attn_dense_call(q, kp, vtp, *, n_prefix_valid):
    bsz, _, tq, _ = q.shape
    n_prefix = kp.shape[2]
    return pl.pallas_call(
        functools.partial(_attn_dense_kernel, n_valid=n_prefix_valid),
        grid=(bsz,),
        in_specs=[
            pl.BlockSpec((1, N_HEADS, tq, HEAD_PAD), lambda b: (b, 0, 0, 0)),
            pl.BlockSpec((1, N_HEADS, n_prefix, HEAD_PAD), lambda b: (b, 0, 0, 0)),
            pl.BlockSpec((1, ATTN_W, n_prefix), lambda b: (b, 0, 0)),
        ],
        out_specs=pl.BlockSpec((1, tq, ATTN_W), lambda b: (b, 0, 0)),
        out_shape=jax.ShapeDtypeStruct((bsz, tq, ATTN_W), jnp.bfloat16),
        compiler_params=pltpu.CompilerParams(
            dimension_semantics=("arbitrary",), vmem_limit_bytes=VMEM_LIMIT_BYTES),
        name="attn_dense",
    )(q, kp, vtp)


def _attn_causal_call(q, kp, vtp, k, vt, *, tq, n_prefix_valid):
    bsz, _, _, seq = q.shape
    n_prefix = kp.shape[2]
    tk = tq // 2
    f32 = jnp.float32
    return pl.pallas_call(
        functools.partial(_attn_causal_kernel, tq=tq, tk=tk, n_prefix_valid=n_prefix_valid),
        grid=(bsz, N_HEADS // HEADS_PER_STEP, seq // tq),
        in_specs=[
            pl.BlockSpec((1, HEADS_PER_STEP, HEAD_PAD, seq), lambda b, hp, i: (b, hp, 0, 0)),
            pl.BlockSpec((1, HEADS_PER_STEP, n_prefix, HEAD_PAD), lambda b, hp, i: (0, hp, 0, 0)),
            pl.BlockSpec((1, HEADS_PER_STEP * V_DIM, n_prefix), lambda b, hp, i: (0, hp, 0)),
            pl.BlockSpec((1, HEADS_PER_STEP, seq, HEAD_PAD), lambda b, hp, i: (b, hp, 0, 0)),
            pl.BlockSpec((1, HEADS_PER_STEP * V_DIM, seq), lambda b, hp, i: (b, hp, 0)),
        ],
        out_specs=pl.BlockSpec((1, tq, HEADS_PER_STEP * V_DIM), lambda b, hp, i: (b, i, hp)),
        out_shape=jax.ShapeDtypeStruct((bsz, seq, ATTN_W), jnp.bfloat16),
        scratch_shapes=[pltpu.VMEM((HEADS_PER_STEP, tq // Q_SLICE, tk, Q_SLICE), f32),
                        pltpu.VMEM((HEADS_PER_STEP, tq // Q_SLICE, tk, Q_SLICE), f32),
                        pltpu.VMEM((HEADS_PER_STEP, 1, tq), f32),
                        pltpu.VMEM((HEADS_PER_STEP, 1, tq), f32),
                        pltpu.VMEM((HEADS_PER_STEP, 1, tq), f32),
                        pltpu.VMEM((HEADS_PER_STEP, V_AUG, tq), f32)],
        compiler_params=pltpu.CompilerParams(
            dimension_semantics=("arbitrary", "arbitrary", "arbitrary"),
            vmem_limit_bytes=VMEM_LIMIT_BYTES),
        name="attn_causal",
    )(q, kp, vtp, k, vt)


def _out_mlp_kernel(o_ref, y_ref, x_ref, g_attn_ref, w_out_ref, g_mlp_ref, w_up_ref, w_down_ref,
                    g_fin_ref, out_ref, hn_scr, *, final):
    c = pl.program_id(1)

    @pl.when(c == 0)
    def _():
        on = _rms(o_ref[...].astype(jnp.float32), g_attn_ref[...]).astype(jnp.bfloat16)
        h = (x_ref[...] + _dot(on, w_out_ref[0:ATTN_W, :])
             + _dot(y_ref[...], w_out_ref[ATTN_W:ATTN_W + LRU_W, :]))
        hn_scr[...] = _rms(h, g_mlp_ref[...]).astype(jnp.bfloat16)
        out_ref[...] = h

    u = _dot(hn_scr[...], w_up_ref[...].astype(jnp.bfloat16))
    u = jnp.square(jnp.maximum(u, 0.0)).astype(jnp.bfloat16)
    out_ref[...] += _dot(u, w_down_ref[...].astype(jnp.bfloat16))

    if final:
        @pl.when(c == pl.num_programs(1) - 1)
        def _():
            out_ref[...] = _rms(out_ref[...], g_fin_ref[...])


def _out_mlp_call(o, y, x, lw, big, g_fin, *, layer, tm, final):
    rows = x.shape[0]
    const = lambda r, c: (0, 0)
    return pl.pallas_call(
        functools.partial(_out_mlp_kernel, final=final),
        grid=(rows // tm, D_FF // F_CHUNK),
        in_specs=[
            pl.BlockSpec((tm, ATTN_W), lambda r, c: (r, 0)),
            pl.BlockSpec((tm, LRU_W), lambda r, c: (r, 0)),
            pl.BlockSpec((tm, D_MODEL), lambda r, c: (r, 0)),
            pl.BlockSpec((1, ATTN_W), const),
            pl.BlockSpec((None, D_MODEL, D_MODEL), lambda r, c: (layer, 0, 0)),
            pl.BlockSpec((1, D_MODEL), const),
            pl.BlockSpec((None, D_MODEL, F_CHUNK), lambda r, c: (layer, 0, c)),
            pl.BlockSpec((None, F_CHUNK, D_MODEL), lambda r, c: (layer, c, 0)),
            pl.BlockSpec((1, D_MODEL), const),
        ],
        out_specs=pl.BlockSpec((tm, D_MODEL), lambda r, c: (r, 0)),
        out_shape=jax.ShapeDtypeStruct((rows, D_MODEL), jnp.float32),
        scratch_shapes=[pltpu.VMEM((tm, D_MODEL), jnp.bfloat16)],
        compiler_params=pltpu.CompilerParams(
            dimension_semantics=("arbitrary", "arbitrary"), vmem_limit_bytes=VMEM_LIMIT_BYTES),
        name="out_mlp",
    )(o, y, x, lw["g_attn"], big["w_out"], lw["g_mlp"], big["w_up"], big["w_down"], g_fin)


def _rope_tables(pos, rope_lane0, pass_lanes, scale):
    half = QK_ROPE // 2
    n = pos.shape[0]
    inv = ROPE_THETA ** (-jnp.arange(0, QK_ROPE, 2, dtype=jnp.float32) / QK_ROPE)
    ang = pos.astype(jnp.float32)[:, None] * inv[None, :]
    cos, sin = jnp.cos(ang) * scale, jnp.sin(ang) * scale
    fill = lambda width, value=0.0: jnp.full((n, width), value, jnp.float32)
    tail = LANES - rope_lane0 - QK_ROPE
    tab_c = jnp.concatenate([fill(pass_lanes, scale), fill(rope_lane0 - pass_lanes), cos, cos, fill(tail)],
                            axis=1)
    tab_m = jnp.concatenate([fill(rope_lane0), -sin, fill(half + tail)], axis=1)
    tab_p = jnp.concatenate([fill(rope_lane0 + half), sin, fill(tail)], axis=1)
    return jnp.stack([tab_c, tab_m, tab_p])


def _key_tables(pos):
    return _rope_tables(pos, 0, 0, 1.0)


def _query_tables(pos):
    return _rope_tables(pos, QK_NOPE, QK_NOPE, SM_SCALE * LOG2_E)


def _block_diag_gates(w_a, w_x):
    per_half = LRU_BLOCKS // 2
    halves = []
    for s in range(2):
        bd_a = jax.scipy.linalg.block_diag(*[w_a[s * per_half + i] for i in range(per_half)])
        bd_x = jax.scipy.linalg.block_diag(*[w_x[s * per_half + i] for i in range(per_half)])
        halves.append(jnp.concatenate([bd_a, bd_x], axis=1))
    return jnp.stack(halves).astype(jnp.bfloat16)


def _layer_weights(l, norm_mix_g, w_in, q_norm_g, w_uq, kv_norm_g, w_ukv, conv_w, conv_b,
                   w_gate_a, b_gate_a, w_gate_x, b_gate_x, lru_lambda, attn_out_g, lru_out_g,
                   norm_mlp_g):
    bf = jnp.bfloat16
    row = lambda v: v.reshape(1, -1).astype(jnp.float32)
    split = COL_KR + QK_ROPE
    w_in_p = (jnp.pad(w_in[l][:, :split], ((0, 0), (0, IN_W_PAD - split)))
              + jnp.pad(w_in[l][:, split:], ((0, 0), (split + ROPE_PAD - QK_ROPE, 0)))).astype(bf)
    w_uq_p = jnp.pad(w_uq[l].reshape(Q_LORA, N_HEADS, QK_NOPE + QK_ROPE),
                     ((0, 0), (0, 0), (0, HEAD_PAD - QK_NOPE - QK_ROPE))).reshape(Q_LORA, -1).astype(bf)
    w_ukv_h = w_ukv[l].reshape(KV_LORA, N_HEADS, QK_NOPE + V_DIM)
    w_uk_p = jnp.pad(w_ukv_h[:, :, :QK_NOPE],
                     ((0, 0), (0, 0), (0, HEAD_PAD - QK_NOPE))).reshape(KV_LORA, -1).astype(bf)
    w_uvt = w_ukv_h[:, :, QK_NOPE:].reshape(KV_LORA, ATTN_W).T.astype(bf)
    return {
        "g_mix": row(norm_mix_g[l]), "w_in": w_in_p, "g_q": row(q_norm_g[l]), "w_uq": w_uq_p, "w_uq_t": w_uq_p.T,
        "g_kv": row(kv_norm_g[l]), "w_uk": w_uk_p, "w_uvt": w_uvt,
        "conv_w": conv_w[l].astype(jnp.float32), "conv_b": row(conv_b[l]),
        "wg": _block_diag_gates(w_gate_a[l], w_gate_x[l]),
        "bg": jnp.stack([b_gate_a[l], b_gate_x[l]]).astype(jnp.float32),
        "lam": row(lru_lambda[l]), "g_lru": row(lru_out_g[l]), "g_attn": row(attn_out_g[l]),
        "g_mlp": row(norm_mlp_g[l]),
    }


def _pad_buf(buf):
    return jnp.pad(buf, ((0, 0), (SUBLANES - (CONV_W - 1), 0), (0, 0)))


def _pad_axis(a, axis, size):
    pad = [(0, 0)] * a.ndim
    pad[axis] = (0, size - a.shape[axis])
    return jnp.pad(a, pad)


def kernel(x_prompt, x_sample, cache_ckv, cache_kpe, state_lru_h, state_conv, meta_tokens,
           norm_mix_g, w_in, q_norm_g, w_uq, kv_norm_g, w_ukv, conv_w, conv_b,
           w_gate_a, b_gate_a, w_gate_x, b_gate_x, lru_lambda, attn_out_g, lru_out_g,
           w_out, norm_mlp_g, w_up, w_down, final_norm_g):
    b_p, seq, _ = x_prompt.shape
    b_s, dec_seq, _ = x_sample.shape
    past_len = cache_ckv.shape[2]
    assert (past_len + dec_seq - 1) // CHUNK == past_len // CHUNK
    tm_p, tq_p, tm_o = PROMPT_MIXER_ROWS, PROMPT_QUERY_ROWS, PROMPT_MLP_ROWS
    assert seq % tm_p == 0 and seq % tq_p == 0 and (b_p * seq) % tm_o == 0
    assert (tq_p // 2) % CHUNK == 0 and tq_p % Q_SLICE == 0

    meta_pos = jnp.arange(-N_META, 0, dtype=jnp.int32)
    prompt_pos = jnp.arange(seq, dtype=jnp.int32)
    sample_pos = past_len + jnp.arange(dec_seq, dtype=jnp.int32)
    tabk_m, tabk_p, tabk_s = _key_tables(meta_pos), _key_tables(prompt_pos), _key_tables(sample_pos)
    tabq_m, tabq_p, tabq_s = _query_tables(meta_pos), _query_tables(prompt_pos), _query_tables(sample_pos)
    tabq_p_t = jnp.swapaxes(tabq_p, 1, 2)
    g_fin = final_norm_g.reshape(1, -1).astype(jnp.float32)

    cache_kpe_pad = _pad_axis(cache_kpe, 3, ROPE_PAD)
    big = {"w_out": w_out.astype(jnp.bfloat16), "w_up": w_up, "w_down": w_down}
    tq_small = LANES

    h_meta = meta_tokens[None].astype(jnp.float32)
    h_p, h_s = x_prompt, x_sample
    zero_h = jnp.zeros((1, 1, LRU_W), jnp.float32)
    zero_buf = jnp.zeros((1, SUBLANES, LRU_W), jnp.float32)
    outs = {name: [] for name in ("ckv_p", "kpe_p", "lru_p", "conv_p", "ckv_s", "kpe_s", "lru_s", "conv_s")}
    for l in range(DEPTH):
        lw = _layer_weights(l, norm_mix_g, w_in, q_norm_g, w_uq, kv_norm_g, w_ukv, conv_w, conv_b,
                            w_gate_a, b_gate_a, w_gate_x, b_gate_x, lru_lambda, attn_out_g,
                            lru_out_g, norm_mlp_g)
        last = l + 1 == DEPTH
        mq, mk, mvt, m_ckv, m_kpe, m_y, m_h, m_buf = _mixer_call(
            h_meta, zero_h, zero_buf, tabq_m, tabk_m, lw, tm=N_META, shared_state=True)
        pq, pk, pvt, p_ckv, p_kpe, p_y, p_h, p_buf = _mixer_call(
            h_p, m_h, _pad_buf(m_buf), tabq_p_t, tabk_p, lw, tm=tm_p, shared_state=True,
            q_transposed=True)
        mk_p, mvt_p = _pad_axis(mk, 2, LANES), _pad_axis(mvt, 2, LANES)
        p_o = _attn_causal_call(pq, mk_p, mvt_p, pk, pvt, tq=tq_p, n_prefix_valid=N_META)
        h_p = _out_mlp_call(p_o.reshape(b_p * seq, ATTN_W), p_y.reshape(b_p * seq, LRU_W),
                            h_p.reshape(b_p * seq, D_MODEL), lw, big, g_fin, layer=l, tm=tm_o,
                            final=last).reshape(b_p, seq, D_MODEL)
        sq, sk, svt, s_ckv, s_kpe, s_y, s_h, s_buf = _mixer_call(
            h_s, state_lru_h[l][:, None, :], _pad_buf(state_conv[l]), tabq_s, tabk_s, lw,
            tm=dec_seq, shared_state=False)
        s_o = _attn_sample_call(sq, m_ckv, cache_ckv, s_ckv, m_kpe, cache_kpe_pad, s_kpe, lw,
                                layer=l)[:, :dec_seq]
        h_s = _out_mlp_call(s_o.reshape(b_s * dec_seq, ATTN_W), s_y.reshape(b_s * dec_seq, LRU_W),
                            h_s.reshape(b_s * dec_seq, D_MODEL), lw, big, g_fin, layer=l,
                            tm=b_s * dec_seq, final=last).reshape(b_s, dec_seq, D_MODEL)
        if not last:
            m_o = _attn_dense_call(_pad_axis(mq, 2, tq_small), mk_p, mvt_p,
                                   n_prefix_valid=N_META)[:, :N_META]
            h_meta = _out_mlp_call(m_o.reshape(N_META, ATTN_W), m_y.reshape(N_META, LRU_W),
                                   h_meta.reshape(N_META, D_MODEL), lw, big, g_fin, layer=l,
                                   tm=N_META, final=False).reshape(1, N_META, D_MODEL)
        outs["ckv_p"].append(p_ckv); outs["kpe_p"].append(p_kpe[:, :, :QK_ROPE])
        outs["lru_p"].append(p_h[:, 0]); outs["conv_p"].append(p_buf)
        outs["ckv_s"].append(s_ckv); outs["kpe_s"].append(s_kpe[:, :, :QK_ROPE])
        outs["lru_s"].append(s_h[:, 0]); outs["conv_s"].append(s_buf)

    return (h_p, h_s, jnp.stack(outs["ckv_p"]), jnp.stack(outs["kpe_p"]), jnp.stack(outs["lru_p"]),
            jnp.stack(outs["conv_p"]), jnp.stack(outs["ckv_s"]), jnp.stack(outs["kpe_s"]),
            jnp.stack(outs["lru_s"]), jnp.stack(outs["conv_s"]))
```

```python
import functools

import jax
import jax.numpy as jnp
from jax import lax
from jax.experimental import pallas as pl
from jax.experimental.pallas import tpu as pltpu

D_MODEL = 1024
DEPTH = 2
CHUNK = 64
N_META = 16
ATTN_W = 512
LRU_W = 512
V_DIM = 64
N_HEADS = 8
QK_NOPE = 64
QK_ROPE = 32
Q_LORA = 768
KV_LORA = 256
LRU_BLOCKS = 8
LRU_BW = LRU_W // LRU_BLOCKS
LRU_C = 8.0
CONV_W = 4
D_FF = 4 * D_MODEL
ROPE_THETA = 10000.0
EPS = 1e-6
SM_SCALE = (QK_NOPE + QK_ROPE) ** -0.5
NEG_INF = -1e30
LOG2_E = 1.4426950408889634

LANES = 128
SUBLANES = 8
HEAD_PAD = LANES
VMEM_LIMIT_BYTES = 56 * 1024 * 1024

ROPE_PAD = LANES
COL_CQ = 0
COL_CKV = Q_LORA
COL_KR = Q_LORA + KV_LORA
COL_XR = COL_KR + ROPE_PAD
COL_GR = COL_XR + LRU_W
IN_W_PAD = COL_GR + LRU_W

HEADS_PER_STEP = 2
ONES_ROWS = 16
V_AUG = V_DIM + ONES_ROWS
F_CHUNK = 1024
Q_SLICE = 256
PROMPT_MIXER_ROWS = 1024
PROMPT_QUERY_ROWS = 1024
PROMPT_MLP_ROWS = 1024


def _rms(x, g):
    return x * lax.rsqrt(jnp.mean(x * x, axis=-1, keepdims=True) + EPS) * g


def _rope_lanes(x, tab_ref, axis=1):
    half = QK_ROPE // 2
    return (x * tab_ref[0] + pltpu.roll(x, LANES - half, axis) * tab_ref[1]
            + pltpu.roll(x, half, axis) * tab_ref[2])


def _dot(a, b):
    return jnp.dot(a, b, preferred_element_type=jnp.float32)


def _dot_nt(a, b):
    return lax.dot_general(a, b, (((1,), (1,)), ((), ())), preferred_element_type=jnp.float32)


def _mixer_kernel(x_ref, h0_ref, buf0_ref, tabq_ref, tabk_ref, g_mix_ref, w_in_ref, g_q_ref,
                  w_uq_ref, g_kv_ref, w_uk_ref, w_uvt_ref, conv_w_ref, conv_b_ref, wg_ref, bg_ref,
                  lam_ref, g_lru_ref,
                  q_ref, k_ref, vt_ref, ckv_ref, kpe_ref, y_ref, hlast_ref, buf_ref,
                  tail_scr, hcar_scr, *, tm, q_transposed):
    t = pl.program_id(1)

    @pl.when(t == 0)
    def _():
        tail_scr[...] = buf0_ref[0]
        hcar_scr[...] = h0_ref[0]

    x = x_ref[0]
    xn = _rms(x, g_mix_ref[...]).astype(jnp.bfloat16)
    z_lru = _dot(xn, w_in_ref[:, COL_XR:IN_W_PAD])

    x_r = z_lru[:, :LRU_W]
    g_r = z_lru[:, LRU_W:]
    n_slabs = tm // SUBLANES
    rows8 = lax.broadcasted_iota(jnp.int32, (SUBLANES, LRU_W), 0)
    slabs = [tail_scr[...]] + [x_r[g * SUBLANES:(g + 1) * SUBLANES, :] for g in range(n_slabs)]
    xc = conv_b_ref[...] + x_r * conv_w_ref[CONV_W - 1:CONV_W, :]
    for back in range(1, CONV_W):
        rolled = [pltpu.roll(s, back, 0) for s in slabs]
        shifted = jnp.concatenate(
            [jnp.where(rows8 < back, rolled[g], rolled[g + 1]) for g in range(n_slabs)], axis=0)
        xc = xc + shifted * conv_w_ref[CONV_W - 1 - back:CONV_W - back, :]
    tail_scr[...] = slabs[-1]
    buf_ref[0] = tail_scr[SUBLANES - (CONV_W - 1):SUBLANES, :]

    half_w = LRU_W // 2
    gates = [_dot(xc[:, s * half_w:(s + 1) * half_w].astype(jnp.bfloat16), wg_ref[s]) for s in range(2)]
    pre_a = jnp.concatenate([gates[0][:, :half_w], gates[1][:, :half_w]], axis=1)
    pre_x = jnp.concatenate([gates[0][:, half_w:], gates[1][:, half_w:]], axis=1)
    r = jax.nn.sigmoid(pre_a + bg_ref[0:1, :])
    gi = jax.nn.sigmoid(pre_x + bg_ref[1:2, :])
    neg_lam = -lam_ref[...]
    softplus = jnp.maximum(neg_lam, 0.0) + jnp.log1p(jnp.exp(-jnp.abs(neg_lam)))
    log_a = (-LRU_C) * r * softplus
    a = jnp.exp(log_a)
    one_m_a2 = -jnp.tanh(log_a) * (a * a + 1.0)
    root = jnp.where(one_m_a2 > 0.0, one_m_a2 * lax.rsqrt(one_m_a2), 0.0)
    b = root * (gi * xc)

    carry = jnp.broadcast_to(hcar_scr[...], (SUBLANES, LRU_W))
    h_slabs = []
    for g in range(n_slabs):
        a_g = a[g * SUBLANES:(g + 1) * SUBLANES, :]
        b_g = b[g * SUBLANES:(g + 1) * SUBLANES, :]
        shift = 1
        while shift < SUBLANES:
            keep = rows8 >= shift
            a_prev = jnp.where(keep, pltpu.roll(a_g, shift, 0), 1.0)
            b_prev = jnp.where(keep, pltpu.roll(b_g, shift, 0), 0.0)
            b_g = a_g * b_prev + b_g
            a_g = a_g * a_prev
            shift *= 2
        h_g = a_g * carry + b_g
        h_slabs.append(h_g)
        carry = jnp.broadcast_to(h_g[SUBLANES - 1:SUBLANES, :], (SUBLANES, LRU_W))
    hh = jnp.concatenate(h_slabs, axis=0)
    h_last = carry[0:1, :]
    hcar_scr[...] = h_last
    hlast_ref[0] = h_last
    y = hh * jax.nn.gelu(g_r)
    y_ref[0] = _rms(y, g_lru_ref[...]).astype(jnp.bfloat16)

    z = _dot(xn, w_in_ref[:, 0:COL_XR])
    cqn = _rms(z[:, COL_CQ:COL_CQ + Q_LORA], g_q_ref[...]).astype(jnp.bfloat16)
    if q_transposed:
        q_t = _dot_nt(w_uq_ref[...], cqn)
        for h in range(N_HEADS):
            qh = q_t[h * HEAD_PAD:(h + 1) * HEAD_PAD, :]
            q_ref[0, h] = _rope_lanes(qh, tabq_ref, axis=0).astype(jnp.bfloat16)
    else:
        q = _dot(cqn, w_uq_ref[...])
        for h in range(N_HEADS):
            qh = q[:, h * HEAD_PAD:(h + 1) * HEAD_PAD]
            q_ref[0, h] = _rope_lanes(qh, tabq_ref).astype(jnp.bfloat16)

    ckv = _rms(z[:, COL_CKV:COL_CKV + KV_LORA], g_kv_ref[...])
    ckv_ref[0] = ckv
    ckv_b = ckv.astype(jnp.bfloat16)
    kpe = _rope_lanes(z[:, COL_KR:COL_KR + ROPE_PAD], tabk_ref)
    kpe_ref[0] = kpe
    kpe_shift = pltpu.roll(kpe, QK_NOPE, 1)
    kk = _dot(ckv_b, w_uk_ref[...])
    for h in range(N_HEADS):
        k_ref[0, h] = (kk[:, h * HEAD_PAD:(h + 1) * HEAD_PAD] + kpe_shift).astype(jnp.bfloat16)
    vt_ref[0] = _dot_nt(w_uvt_ref[...], ckv_b).astype(jnp.bfloat16)


def _mixer_call(x, h0, buf0p, tabq, tabk, lw, *, tm, shared_state, q_transposed=False):
    bsz, seq, _ = x.shape
    nt = seq // tm
    const2 = lambda b, t: (0, 0)
    const3 = lambda b, t: (0, 0, 0)
    state_idx = (lambda b, t: (0, 0, 0)) if shared_state else (lambda b, t: (b, 0, 0))
    in_specs = [
        pl.BlockSpec((1, tm, D_MODEL), lambda b, t: (b, t, 0)),
        pl.BlockSpec((1, 1, LRU_W), state_idx),
        pl.BlockSpec((1, SUBLANES, LRU_W), state_idx),
        (pl.BlockSpec((3, LANES, tm), lambda b, t: (0, 0, t)) if q_transposed
         else pl.BlockSpec((3, tm, LANES), lambda b, t: (0, t, 0))),
        pl.BlockSpec((3, tm, LANES), lambda b, t: (0, t, 0)),
        pl.BlockSpec((1, D_MODEL), const2),
        pl.BlockSpec((D_MODEL, IN_W_PAD), const2),
        pl.BlockSpec((1, Q_LORA), const2),
        pl.BlockSpec((N_HEADS * HEAD_PAD, Q_LORA) if q_transposed else (Q_LORA, N_HEADS * HEAD_PAD), const2),
        pl.BlockSpec((1, KV_LORA), const2),
        pl.BlockSpec((KV_LORA, N_HEADS * HEAD_PAD), const2),
        pl.BlockSpec((ATTN_W, KV_LORA), const2),
        pl.BlockSpec((CONV_W, LRU_W), const2),
        pl.BlockSpec((1, LRU_W), const2),
        pl.BlockSpec((2, LRU_W // 2, LRU_W), const3),
        pl.BlockSpec((2, LRU_W), const2),
        pl.BlockSpec((1, LRU_W), const2),
        pl.BlockSpec((1, LRU_W), const2),
    ]
    out_shape = [
        jax.ShapeDtypeStruct((bsz, N_HEADS, HEAD_PAD, seq) if q_transposed
                             else (bsz, N_HEADS, seq, HEAD_PAD), jnp.bfloat16),
        jax.ShapeDtypeStruct((bsz, N_HEADS, seq, HEAD_PAD), jnp.bfloat16),
        jax.ShapeDtypeStruct((bsz, ATTN_W, seq), jnp.bfloat16),
        jax.ShapeDtypeStruct((bsz, seq, KV_LORA), jnp.float32),
        jax.ShapeDtypeStruct((bsz, seq, ROPE_PAD), jnp.float32),
        jax.ShapeDtypeStruct((bsz, seq, LRU_W), jnp.bfloat16),
        jax.ShapeDtypeStruct((bsz, 1, LRU_W), jnp.float32),
        jax.ShapeDtypeStruct((bsz, CONV_W - 1, LRU_W), jnp.float32),
    ]
    out_specs = [
        (pl.BlockSpec((1, N_HEADS, HEAD_PAD, tm), lambda b, t: (b, 0, 0, t)) if q_transposed
         else pl.BlockSpec((1, N_HEADS, tm, HEAD_PAD), lambda b, t: (b, 0, t, 0))),
        pl.BlockSpec((1, N_HEADS, tm, HEAD_PAD), lambda b, t: (b, 0, t, 0)),
        pl.BlockSpec((1, ATTN_W, tm), lambda b, t: (b, 0, t)),
        pl.BlockSpec((1, tm, KV_LORA), lambda b, t: (b, t, 0)),
        pl.BlockSpec((1, tm, ROPE_PAD), lambda b, t: (b, t, 0)),
        pl.BlockSpec((1, tm, LRU_W), lambda b, t: (b, t, 0)),
        pl.BlockSpec((1, 1, LRU_W), lambda b, t: (b, 0, 0)),
        pl.BlockSpec((1, CONV_W - 1, LRU_W), lambda b, t: (b, 0, 0)),
    ]
    return pl.pallas_call(
        functools.partial(_mixer_kernel, tm=tm, q_transposed=q_transposed),
        grid=(bsz, nt),
        in_specs=in_specs,
        out_specs=out_specs,
        out_shape=out_shape,
        scratch_shapes=[pltpu.VMEM((SUBLANES, LRU_W), jnp.float32),
                        pltpu.VMEM((1, LRU_W), jnp.float32)],
        compiler_params=pltpu.CompilerParams(
            dimension_semantics=("arbitrary", "arbitrary"), vmem_limit_bytes=VMEM_LIMIT_BYTES),
        name="mixer",
    )(x, h0, buf0p, tabq, tabk, lw["g_mix"], lw["w_in"], lw["g_q"],
      lw["w_uq_t"] if q_transposed else lw["w_uq"], lw["g_kv"],
      lw["w_uk"], lw["w_uvt"], lw["conv_w"], lw["conv_b"], lw["wg"], lw["bg"], lw["lam"], lw["g_lru"])


def _attn_sample_kernel(q_ref, ckv_m_ref, ckv_c_ref, ckv_s_ref, kpe_m_ref, kpe_c_ref, kpe_s_ref,
                        w_uk_ref, w_uvt_ref, o_ref, *, n_keys_pad):
    ckv_parts = [ckv_m_ref[0], ckv_c_ref[0, 0], ckv_s_ref[0]]
    kpe_parts = [kpe_m_ref[0], kpe_c_ref[0, 0], kpe_s_ref[0]]
    n_valid = sum(p.shape[0] for p in ckv_parts)
    pad = n_keys_pad - n_valid
    ckv = jnp.concatenate(ckv_parts + [jnp.zeros((pad, KV_LORA), jnp.float32)], axis=0)
    kpe = jnp.concatenate(kpe_parts + [jnp.zeros((pad, ROPE_PAD), jnp.float32)], axis=0)
    ckv_b = ckv.astype(jnp.bfloat16)
    kpe_shift = pltpu.roll(kpe, QK_NOPE, 1)
    kk = _dot(ckv_b, w_uk_ref[...])
    vt = _dot_nt(w_uvt_ref[...], ckv_b).astype(jnp.bfloat16)
    n_q = q_ref.shape[2]
    q_pad = jnp.zeros((LANES - n_q, HEAD_PAD), jnp.bfloat16)
    q_tiles = [jnp.concatenate([q_ref[0, h], q_pad], axis=0) for h in range(N_HEADS)]
    zero_q = jnp.zeros((LANES, HEAD_PAD), jnp.bfloat16)
    ones_k = jnp.ones((ONES_ROWS, n_keys_pad), jnp.bfloat16)
    kpe_pair = jnp.concatenate([kpe_shift, kpe_shift], axis=1)
    key_idx = lax.broadcasted_iota(jnp.int32, (n_keys_pad, 2 * LANES), 0)
    accs = []
    for pair in range(N_HEADS // 2):
        h_a, h_b = 2 * pair, 2 * pair + 1
        k_pair = (kk[:, h_a * HEAD_PAD:(h_b + 1) * HEAD_PAD] + kpe_pair).astype(jnp.bfloat16)
        q_blockdiag = jnp.concatenate([jnp.concatenate([q_tiles[h_a], zero_q], axis=1),
                                       jnp.concatenate([zero_q, q_tiles[h_b]], axis=1)], axis=0)
        st = _dot_nt(k_pair, q_blockdiag)
        st = jnp.where(key_idx < n_valid, st, NEG_INF)
        p = jnp.exp2(st - _col_max(st)).astype(jnp.bfloat16)
        vt_pair = jnp.concatenate([vt[h_a * V_DIM:(h_a + 1) * V_DIM, :], ones_k,
                                   vt[h_b * V_DIM:(h_b + 1) * V_DIM, :], ones_k], axis=0)
        acc = _dot(vt_pair, p)
        accs.append(acc[:V_AUG, :LANES])
        accs.append(acc[V_AUG:, LANES:])
    _store_normalised(accs, o_ref)


def _attn_sample_call(q, ckv_m, ckv_cache, ckv_s, kpe_m, kpe_cache, kpe_s, lw, *, layer):
    bsz, _, dec, _ = q.shape
    past = ckv_cache.shape[2]
    n_meta = ckv_m.shape[1]
    n_keys_pad = -(-(n_meta + past + dec) // LANES) * LANES
    return pl.pallas_call(
        functools.partial(_attn_sample_kernel, n_keys_pad=n_keys_pad),
        grid=(bsz,),
        in_specs=[
            pl.BlockSpec((1, N_HEADS, dec, HEAD_PAD), lambda b: (b, 0, 0, 0)),
            pl.BlockSpec((1, n_meta, KV_LORA), lambda b: (0, 0, 0)),
            pl.BlockSpec((1, 1, past, KV_LORA), lambda b: (layer, b, 0, 0)),
            pl.BlockSpec((1, dec, KV_LORA), lambda b: (b, 0, 0)),
            pl.BlockSpec((1, n_meta, ROPE_PAD), lambda b: (0, 0, 0)),
            pl.BlockSpec((1, 1, past, ROPE_PAD), lambda b: (layer, b, 0, 0)),
            pl.BlockSpec((1, dec, ROPE_PAD), lambda b: (b, 0, 0)),
            pl.BlockSpec((KV_LORA, N_HEADS * HEAD_PAD), lambda b: (0, 0)),
            pl.BlockSpec((ATTN_W, KV_LORA), lambda b: (0, 0)),
        ],
        out_specs=pl.BlockSpec((1, LANES, ATTN_W), lambda b: (b, 0, 0)),
        out_shape=jax.ShapeDtypeStruct((bsz, LANES, ATTN_W), jnp.bfloat16),
        compiler_params=pltpu.CompilerParams(
            dimension_semantics=("arbitrary",), vmem_limit_bytes=VMEM_LIMIT_BYTES),
        name="attn_sample",
    )(q, ckv_m, ckv_cache, ckv_s, kpe_m, kpe_cache, kpe_s, lw["w_uk"], lw["w_uvt"])


def _col_max(st):
    rows, cols = st.shape
    slab = SUBLANES
    if rows % slab == 0 and rows > slab:
        st = jnp.max(st.reshape(rows // slab, slab, cols), axis=0)
    return jnp.max(st, axis=0, keepdims=True)


def _dense_state(q_tiles, k_tiles, vt_tiles, n_valid, q_transposed=False):
    n_keys = k_tiles[0].shape[0]
    tq = q_tiles[0].shape[1 if q_transposed else 0]
    n_used = -(-n_valid // ONES_ROWS) * ONES_ROWS
    ones_p = jnp.ones((ONES_ROWS, n_keys), jnp.bfloat16)
    ms, accs = [], []
    for q_t, k_t, vt_t in zip(q_tiles, k_tiles, vt_tiles):
        st = (_dot if q_transposed else _dot_nt)(k_t[:n_used], q_t)
        if n_valid < n_used:
            key_idx = lax.broadcasted_iota(jnp.int32, (n_used, tq), 0)
            st = jnp.where(key_idx < n_valid, st, NEG_INF)
        m = _col_max(st)
        p = jnp.exp2(st - m).astype(jnp.bfloat16)
        if n_used < n_keys:
            p = jnp.concatenate([p, jnp.zeros((n_keys - n_used, tq), jnp.bfloat16)], axis=0)
        ms.append(m)
        accs.append(_dot(jnp.concatenate([vt_t, ones_p], axis=0), p))
    return ms, accs


def _prefix_state(q_tiles, kp_ref, vtp_ref, n_valid, q_transposed=False):
    heads = range(len(q_tiles))
    return _dense_state(q_tiles, [kp_ref[0, hh] for hh in heads],
                        [vtp_ref[0, hh * V_DIM:(hh + 1) * V_DIM, :] for hh in heads], n_valid,
                        q_transposed=q_transposed)


def _store_normalised(accs, o_ref):
    outs = [acc[:V_DIM, :] / acc[V_DIM:V_DIM + 1, :] for acc in accs]
    o_ref[0] = jnp.concatenate(outs, axis=0).T.astype(jnp.bfloat16)


def _attn_dense_kernel(q_ref, kp_ref, vtp_ref, o_ref, *, n_valid):
    q_tiles = [q_ref[0, hh] for hh in range(q_ref.shape[1])]
    _, accs = _prefix_state(q_tiles, kp_ref, vtp_ref, n_valid)
    _store_normalised(accs, o_ref)


def _attn_causal_kernel(q_ref, kp_ref, vtp_ref, k_ref, vt_ref, o_ref,
                        sa_scr, sb_scr, ma_scr, mb_scr, m_scr, acc_scr,
                        *, tq, tk, n_prefix_valid):
    qi = pl.program_id(2)
    n_q = pl.num_programs(2)

    def q_cols(hh, q_idx, q0, qn):
        return q_ref[0, hh, :, pl.ds(pl.multiple_of(q_idx * tq + q0, Q_SLICE), qn)]

    m_scr[...] = jnp.full(m_scr.shape, NEG_INF, jnp.float32)
    acc_scr[...] = jnp.zeros(acc_scr.shape, jnp.float32)
    ones_k = jnp.ones((ONES_ROWS, tk), jnp.bfloat16)

    def scores(tile, s_ref, mx_ref, q_idx=qi, q0=0, qn=Q_SLICE, kn=tk, heads=range(HEADS_PER_STEP)):
        start = pl.multiple_of(tile * tk, tk)
        for hh in heads:
            st = _dot(k_ref[0, hh, pl.ds(start, kn), :], q_cols(hh, q_idx, q0, qn))
            s_ref[hh, q0 // Q_SLICE, 0:kn, :] = st
            mx_ref[hh, :, q0:q0 + qn] = _col_max(st)

    def softmax_pv(tile, s_ref, mx_ref, mask=None, q0=0, qn=Q_SLICE, kn=tk, heads=range(HEADS_PER_STEP)):
        start = pl.multiple_of(tile * tk, tk)
        for hh in heads:
            st = s_ref[hh, q0 // Q_SLICE, 0:kn, :]
            if mask is None:
                m_blk = mx_ref[hh, :, q0:q0 + qn]
            else:
                st = jnp.where(mask, st, NEG_INF)
                m_blk = _col_max(st)
            m_old = m_scr[hh, :, q0:q0 + qn]
            m_new = jnp.maximum(m_old, m_blk)
            p = jnp.exp2(st - m_new).astype(jnp.bfloat16)
            vt_t = vt_ref[0, hh * V_DIM:(hh + 1) * V_DIM, pl.ds(start, kn)]
            vt_aug = jnp.concatenate([vt_t, ones_k[:, :kn]], axis=0)
            acc_scr[hh, :, q0:q0 + qn] = (jnp.exp2(m_old - m_new) * acc_scr[hh, :, q0:q0 + qn]
                                          + _dot(vt_aug, p))
            m_scr[hh, :, q0:q0 + qn] = m_new

    @pl.when(qi == 0)
    def _():
        for q0 in range(0, tq, Q_SLICE):
            scores(0, sa_scr, ma_scr, q0=q0)

    def tile_pair(t0):
        for hh in range(HEADS_PER_STEP):
            for q0 in range(0, tq, Q_SLICE):
                scores(t0 + 1, sb_scr, mb_scr, heads=(hh,), q0=q0, qn=Q_SLICE)
                softmax_pv(t0, sa_scr, ma_scr, heads=(hh,), q0=q0, qn=Q_SLICE)
        for hh in range(HEADS_PER_STEP):
            for q0 in range(0, tq, Q_SLICE):
                scores(t0 + 2, sa_scr, ma_scr, heads=(hh,), q0=q0, qn=Q_SLICE)
                softmax_pv(t0 + 1, sb_scr, mb_scr, heads=(hh,), q0=q0, qn=Q_SLICE)

    def body(jj, carry):
        tile_pair(4 * jj)
        tile_pair(4 * jj + 2)
        return carry

    lax.fori_loop(0, qi // 2, body, 0)

    @pl.when(qi % 2 == 1)
    def _():
        tile_pair(2 * (qi - 1))

    d1 = 2 * qi
    key_chunk = lax.broadcasted_iota(jnp.int32, (tk, Q_SLICE), 0) // CHUNK
    qry_local = lax.broadcasted_iota(jnp.int32, (tk, Q_SLICE), 1)

    def visible_keys(q_off):
        return min(tk, q_off + Q_SLICE)

    def chunk_mask(q_off):
        if q_off >= tk:
            return None
        return (key_chunk <= (qry_local + q_off) // CHUNK)[:visible_keys(q_off)]

    for hh in range(HEADS_PER_STEP):
        for q0 in range(0, tq, Q_SLICE):
            if q0 >= tk:
                scores(d1 + 1, sb_scr, mb_scr, heads=(hh,), q0=q0, qn=Q_SLICE, kn=visible_keys(q0 - tk))
            softmax_pv(d1, sa_scr, ma_scr, mask=chunk_mask(q0), heads=(hh,), q0=q0, qn=Q_SLICE,
                       kn=visible_keys(q0))
    q_next = jnp.minimum(qi + 1, n_q - 1)
    for hh in range(HEADS_PER_STEP):
        for q0 in range(0, tq, Q_SLICE):
            scores(0, sa_scr, ma_scr, q_idx=q_next, heads=(hh,), q0=q0, qn=Q_SLICE)
            if q0 >= tk:
                softmax_pv(d1 + 1, sb_scr, mb_scr, mask=chunk_mask(q0 - tk), heads=(hh,),
                           q0=q0, qn=Q_SLICE, kn=visible_keys(q0 - tk))
    ms_p, accs_p = _prefix_state([q_cols(hh, qi, 0, tq) for hh in range(HEADS_PER_STEP)],
                                 kp_ref, vtp_ref, n_prefix_valid, q_transposed=True)
    accs = []
    for hh in range(HEADS_PER_STEP):
        m_run = m_scr[hh]
        m_all = jnp.maximum(m_run, ms_p[hh])
        accs.append(jnp.exp2(m_run - m_all) * acc_scr[hh] + jnp.exp2(ms_p[hh] - m_all) * accs_p[hh])
    _store_normalised(accs, o_ref)


def _attn_dense_call(q, kp, vtp, *, n_prefix_valid):
    bsz, _, tq, _ = q.shape
    n_prefix = kp.shape[2]
    return pl.pallas_call(
        functools.partial(_attn_dense_kernel, n_valid=n_prefix_valid),
        grid=(bsz,),
        in_specs=[
            pl.BlockSpec((1, N_HEADS, tq, HEAD_PAD), lambda b: (b, 0, 0, 0)),
            pl.BlockSpec((1, N_HEADS, n_prefix, HEAD_PAD), lambda b: (b, 0, 0, 0)),
            pl.BlockSpec((1, ATTN_W, n_prefix), lambda b: (b, 0, 0)),
        ],
        out_specs=pl.BlockSpec((1, tq, ATTN_W), lambda b: (b, 0, 0)),
        out_shape=jax.ShapeDtypeStruct((bsz, tq, ATTN_W), jnp.bfloat16),
        compiler_params=pltpu.CompilerParams(
            dimension_semantics=("arbitrary",), vmem_limit_bytes=VMEM_LIMIT_BYTES),
        name="attn_dense",
    )(q, kp, vtp)


def _attn_causal_call(q, kp, vtp, k, vt, *, tq, n_prefix_valid):
    bsz, _, _, seq = q.shape
    n_prefix = kp.shape[2]
    tk = tq // 2
    f32 = jnp.float32
    return pl.pallas_call(
        functools.partial(_attn_causal_kernel, tq=tq, tk=tk, n_prefix_valid=n_prefix_valid),
        grid=(bsz, N_HEADS // HEADS_PER_STEP, seq // tq),
        in_specs=[
            pl.BlockSpec((1, HEADS_PER_STEP, HEAD_PAD, seq), lambda b, hp, i: (b, hp, 0, 0)),
            pl.BlockSpec((1, HEADS_PER_STEP, n_prefix, HEAD_PAD), lambda b, hp, i: (0, hp, 0, 0)),
            pl.BlockSpec((1, HEADS_PER_STEP * V_DIM, n_prefix), lambda b, hp, i: (0, hp, 0)),
            pl.BlockSpec((1, HEADS_PER_STEP, seq, HEAD_PAD), lambda b, hp, i: (b, hp, 0, 0)),
            pl.BlockSpec((1, HEADS_PER_STEP * V_DIM, seq), lambda b, hp, i: (b, hp, 0)),
        ],
        out_specs=pl.BlockSpec((1, tq, HEADS_PER_STEP * V_DIM), lambda b, hp, i: (b, i, hp)),
        out_shape=jax.ShapeDtypeStruct((bsz, seq, ATTN_W), jnp.bfloat16),
        scratch_shapes=[pltpu.VMEM((HEADS_PER_STEP, tq // Q_SLICE, tk, Q_SLICE), f32),
                        pltpu.VMEM((HEADS_PER_STEP, tq // Q_SLICE, tk, Q_SLICE), f32),
                        pltpu.VMEM((HEADS_PER_STEP, 1, tq), f32),
                        pltpu.VMEM((HEADS_PER_STEP, 1, tq), f32),
                        pltpu.VMEM((HEADS_PER_STEP, 1, tq), f32),
                        pltpu.VMEM((HEADS_PER_STEP, V_AUG, tq), f32)],
        compiler_params=pltpu.CompilerParams(
            dimension_semantics=("arbitrary", "arbitrary", "arbitrary"),
            vmem_limit_bytes=VMEM_LIMIT_BYTES),
        name="attn_causal",
    )(q, kp, vtp, k, vt)


def _out_mlp_kernel(o_ref, y_ref, x_ref, g_attn_ref, w_out_ref, g_mlp_ref, w_up_ref, w_down_ref,
                    g_fin_ref, out_ref, hn_scr, *, final):
    c = pl.program_id(1)

    @pl.when(c == 0)
    def _():
        on = _rms(o_ref[...].astype(jnp.float32), g_attn_ref[...]).astype(jnp.bfloat16)
        h = (x_ref[...] + _dot(on, w_out_ref[0:ATTN_W, :])
             + _dot(y_ref[...], w_out_ref[ATTN_W:ATTN_W + LRU_W, :]))
        hn_scr[...] = _rms(h, g_mlp_ref[...]).astype(jnp.bfloat16)
        out_ref[...] = h

    u = _dot(hn_scr[...], w_up_ref[...].astype(jnp.bfloat16))
    u = jnp.square(jnp.maximum(u, 0.0)).astype(jnp.bfloat16)
    out_ref[...] += _dot(u, w_down_ref[...].astype(jnp.bfloat16))

    if final:
        @pl.when(c == pl.num_programs(1) - 1)
        def _():
            out_ref[...] = _rms(out_ref[...], g_fin_ref[...])


def _out_mlp_call(o, y, x, lw, big, g_fin, *, layer, tm, final):
    rows = x.shape[0]
    const = lambda r, c: (0, 0)
    return pl.pallas_call(
        functools.partial(_out_mlp_kernel, final=final),
        grid=(rows // tm, D_FF // F_CHUNK),
        in_specs=[
            pl.BlockSpec((tm, ATTN_W), lambda r, c: (r, 0)),
            pl.BlockSpec((tm, LRU_W), lambda r, c: (r, 0)),
            pl.BlockSpec((tm, D_MODEL), lambda r, c: (r, 0)),
            pl.BlockSpec((1, ATTN_W), const),
            pl.BlockSpec((None, D_MODEL, D_MODEL), lambda r, c: (layer, 0, 0)),
            pl.BlockSpec((1, D_MODEL), const),
            pl.BlockSpec((None, D_MODEL, F_CHUNK), lambda r, c: (layer, 0, c)),
            pl.BlockSpec((None, F_CHUNK, D_MODEL), lambda r, c: (layer, c, 0)),
            pl.BlockSpec((1, D_MODEL), const),
        ],
        out_specs=pl.BlockSpec((tm, D_MODEL), lambda r, c: (r, 0)),
        out_shape=jax.ShapeDtypeStruct((rows, D_MODEL), jnp.float32),
        scratch_shapes=[pltpu.VMEM((tm, D_MODEL), jnp.bfloat16)],
        compiler_params=pltpu.CompilerParams(
            dimension_semantics=("arbitrary", "arbitrary"), vmem_limit_bytes=VMEM_LIMIT_BYTES),
        name="out_mlp",
    )(o, y, x, lw["g_attn"], big["w_out"], lw["g_mlp"], big["w_up"], big["w_down"], g_fin)


def _rope_tables(pos, rope_lane0, pass_lanes, scale):
    half = QK_ROPE // 2
    n = pos.shape[0]
    inv = ROPE_THETA ** (-jnp.arange(0, QK_ROPE, 2, dtype=jnp.float32) / QK_ROPE)
    ang = pos.astype(jnp.float32)[:, None] * inv[None, :]
    cos, sin = jnp.cos(ang) * scale, jnp.sin(ang) * scale
    fill = lambda width, value=0.0: jnp.full((n, width), value, jnp.float32)
    tail = LANES - rope_lane0 - QK_ROPE
    tab_c = jnp.concatenate([fill(pass_lanes, scale), fill(rope_lane0 - pass_lanes), cos, cos, fill(tail)],
                            axis=1)
    tab_m = jnp.concatenate([fill(rope_lane0), -sin, fill(half + tail)], axis=1)
    tab_p = jnp.concatenate([fill(rope_lane0 + half), sin, fill(tail)], axis=1)
    return jnp.stack([tab_c, tab_m, tab_p])


def _key_tables(pos):
    return _rope_tables(pos, 0, 0, 1.0)


def _query_tables(pos):
    return _rope_tables(pos, QK_NOPE, QK_NOPE, SM_SCALE * LOG2_E)


def _block_diag_gates(w_a, w_x):
    per_half = LRU_BLOCKS // 2
    halves = []
    for s in range(2):
        bd_a = jax.scipy.linalg.block_diag(*[w_a[s * per_half + i] for i in range(per_half)])
        bd_x = jax.scipy.linalg.block_diag(*[w_x[s * per_half + i] for i in range(per_half)])
        halves.append(jnp.concatenate([bd_a, bd_x], axis=1))
    return jnp.stack(halves).astype(jnp.bfloat16)


def _layer_weights(l, norm_mix_g, w_in, q_norm_g, w_uq, kv_norm_g, w_ukv, conv_w, conv_b,
                   w_gate_a, b_gate_a, w_gate_x, b_gate_x, lru_lambda, attn_out_g, lru_out_g,
                   norm_mlp_g):
    bf = jnp.bfloat16
    row = lambda v: v.reshape(1, -1).astype(jnp.float32)
    split = COL_KR + QK_ROPE
    w_in_p = (jnp.pad(w_in[l][:, :split], ((0, 0), (0, IN_W_PAD - split)))
              + jnp.pad(w_in[l][:, split:], ((0, 0), (split + ROPE_PAD - QK_ROPE, 0)))).astype(bf)
    w_uq_p = jnp.pad(w_uq[l].reshape(Q_LORA, N_HEADS, QK_NOPE + QK_ROPE),
                     ((0, 0), (0, 0), (0, HEAD_PAD - QK_NOPE - QK_ROPE))).reshape(Q_LORA, -1).astype(bf)
    w_ukv_h = w_ukv[l].reshape(KV_LORA, N_HEADS, QK_NOPE + V_DIM)
    w_uk_p = jnp.pad(w_ukv_h[:, :, :QK_NOPE],
                     ((0, 0), (0, 0), (0, HEAD_PAD - QK_NOPE))).reshape(KV_LORA, -1).astype(bf)
    w_uvt = w_ukv_h[:, :, QK_NOPE:].reshape(KV_LORA, ATTN_W).T.astype(bf)
    return {
        "g_mix": row(norm_mix_g[l]), "w_in": w_in_p, "g_q": row(q_norm_g[l]), "w_uq": w_uq_p, "w_uq_t": w_uq_p.T,
        "g_kv": row(kv_norm_g[l]), "w_uk": w_uk_p, "w_uvt": w_uvt,
        "conv_w": conv_w[l].astype(jnp.float32), "conv_b": row(conv_b[l]),
        "wg": _block_diag_gates(w_gate_a[l], w_gate_x[l]),
        "bg": jnp.stack([b_gate_a[l], b_gate_x[l]]).astype(jnp.float32),
        "lam": row(lru_lambda[l]), "g_lru": row(lru_out_g[l]), "g_attn": row(attn_out_g[l]),
        "g_mlp": row(norm_mlp_g[l]),
    }


def _pad_buf(buf):
    return jnp.pad(buf, ((0, 0), (SUBLANES - (CONV_W - 1), 0), (0, 0)))


def _pad_axis(a, axis, size):
    pad = [(0, 0)] * a.ndim
    pad[axis] = (0, size - a.shape[axis])
    return jnp.pad(a, pad)


def kernel(x_prompt, x_sample, cache_ckv, cache_kpe, state_lru_h, state_conv, meta_tokens,
           norm_mix_g, w_in, q_norm_g, w_uq, kv_norm_g, w_ukv, conv_w, conv_b,
           w_gate_a, b_gate_a, w_gate_x, b_gate_x, lru_lambda, attn_out_g, lru_out_g,
           w_out, norm_mlp_g, w_up, w_down, final_norm_g):
    b_p, seq, _ = x_prompt.shape
    b_s, dec_seq, _ = x_sample.shape
    past_len = cache_ckv.shape[2]
    assert (past_len + dec_seq - 1) // CHUNK == past_len // CHUNK
    tm_p, tq_p, tm_o = PROMPT_MIXER_ROWS, PROMPT_QUERY_ROWS, PROMPT_MLP_ROWS
    assert seq % tm_p == 0 and seq % tq_p == 0 and (b_p * seq) % tm_o == 0
    assert (tq_p // 2) % CHUNK == 0 and tq_p % Q_SLICE == 0

    meta_pos = jnp.arange(-N_META, 0, dtype=jnp.int32)
    prompt_pos = jnp.arange(seq, dtype=jnp.int32)
    sample_pos = past_len + jnp.arange(dec_seq, dtype=jnp.int32)
    tabk_m, tabk_p, tabk_s = _key_tables(meta_pos), _key_tables(prompt_pos), _key_tables(sample_pos)
    tabq_m, tabq_p, tabq_s = _query_tables(meta_pos), _query_tables(prompt_pos), _query_tables(sample_pos)
    tabq_p_t = jnp.swapaxes(tabq_p, 1, 2)
    g_fin = final_norm_g.reshape(1, -1).astype(jnp.float32)

    cache_kpe_pad = _pad_axis(cache_kpe, 3, ROPE_PAD)
    big = {"w_out": w_out.astype(jnp.bfloat16), "w_up": w_up, "w_down": w_down}
    tq_small = LANES

    h_meta = meta_tokens[None].astype(jnp.float32)
    h_p, h_s = x_prompt, x_sample
    zero_h = jnp.zeros((1, 1, LRU_W), jnp.float32)
    zero_buf = jnp.zeros((1, SUBLANES, LRU_W), jnp.float32)
    outs = {name: [] for name in ("ckv_p", "kpe_p", "lru_p", "conv_p", "ckv_s", "kpe_s", "lru_s", "conv_s")}
    for l in range(DEPTH):
        lw = _layer_weights(l, norm_mix_g, w_in, q_norm_g, w_uq, kv_norm_g, w_ukv, conv_w, conv_b,
                            w_gate_a, b_gate_a, w_gate_x, b_gate_x, lru_lambda, attn_out_g,
                            lru_out_g, norm_mlp_g)
        last = l + 1 == DEPTH
        mq, mk, mvt, m_ckv, m_kpe, m_y, m_h, m_buf = _mixer_call(
            h_meta, zero_h, zero_buf, tabq_m, tabk_m, lw, tm=N_META, shared_state=True)
        pq, pk, pvt, p_ckv, p_kpe, p_y, p_h, p_buf = _mixer_call(
            h_p, m_h, _pad_buf(m_buf), tabq_p_t, tabk_p, lw, tm=tm_p, shared_state=True,
            q_transposed=True)
        mk_p, mvt_p = _pad_axis(mk, 2, LANES), _pad_axis(mvt, 2, LANES)
        p_o = _attn_causal_call(pq, mk_p, mvt_p, pk, pvt, tq=tq_p, n_prefix_valid=N_META)
        h_p = _out_mlp_call(p_o.reshape(b_p * seq, ATTN_W), p_y.reshape(b_p * seq, LRU_W),
                            h_p.reshape(b_p * seq, D_MODEL), lw, big, g_fin, layer=l, tm=tm_o,
                            final=last).reshape(b_p, seq, D_MODEL)
        sq, sk, svt, s_ckv, s_kpe, s_y, s_h, s_buf = _mixer_call(
            h_s, state_lru_h[l][:, None, :], _pad_buf(state_conv[l]), tabq_s, tabk_s, lw,
            tm=dec_seq, shared_state=False)
        s_o = _attn_sample_call(sq, m_ckv, cache_ckv, s_ckv, m_kpe, cache_kpe_pad, s_kpe, lw,
                                layer=l)[:, :dec_seq]
        h_s = _out_mlp_call(s_o.reshape(b_s * dec_seq, ATTN_W), s_y.reshape(b_s * dec_seq, LRU_W),
                            h_s.reshape(b_s * dec_seq, D_MODEL), lw, big, g_fin, layer=l,
                            tm=b_s * dec_seq, final=last).reshape(b_s, dec_seq, D_MODEL)
        if not last:
            m_o = _attn_dense_call(_pad_axis(mq, 2, tq_small), mk_p, mvt_p,
                                   n_prefix_valid=N_META)[:, :N_META]
            h_meta = _out_mlp_call(m_o.reshape(N_META, ATTN_W), m_y.reshape(N_META, LRU_W),
                                   h_meta.reshape(N_META, D_MODEL), lw, big, g_fin, layer=l,
                                   tm=N_META, final=False).reshape(1, N_META, D_MODEL)
        outs["ckv_p"].append(p_ckv); outs["kpe_p"].append(p_kpe[:, :, :QK_ROPE])
        outs["lru_p"].append(p_h[:, 0]); outs["conv_p"].append(p_buf)
        outs["ckv_s"].append(s_ckv); outs["kpe_s"].append(s_kpe[:, :, :QK_ROPE])
        outs["lru_s"].append(s_h[:, 0]); outs["conv_s"].append(s_buf)

    return (h_p, h_s, jnp.stack(outs["ckv_p"]), jnp.stack(outs["kpe_p"]), jnp.stack(outs["lru_p"]),
            jnp.stack(outs["conv_p"]), jnp.stack(outs["ckv_s"]), jnp.stack(outs["kpe_s"]),
            jnp.stack(outs["lru_s"]), jnp.stack(outs["conv_s"]))
```
